```python
import math
import jax, jax.numpy as jnp
from jax import lax
import numpy as np

D_MODEL = 1024
BATCH = 8
SEQ = 4096
DEPTH = 2

MEM_LEN = 256
EPS = 1e-6

RW_HEAD = 64
RW_HEADS = D_MODEL // RW_HEAD
RW_DECAY_LORA = 64
RW_AAA_LORA = 64
RW_MV_LORA = 32
RW_GN_EPS = 64e-5

GDN_HEAD = 128
GDN_HEADS = D_MODEL // GDN_HEAD
GDN_CONV = 4
GDN_CHUNK = 64

MLA_HEADS = 8
MLA_HEAD = D_MODEL // MLA_HEADS
MLA_Q_RANK = 256
MLA_KV_RANK = 256
IDX_HEADS = 8
IDX_HEAD = 64
TOPK_MAX = 256
Q_BLOCK = 128

XA_HEADS = 4
XA_HEAD = D_MODEL // XA_HEADS

D_FF = 2816
N_EXPERTS = 8
TOP_K = 2
D_FF_EXPERT = 3584
N_DENSE = (DEPTH + 1) // 2
N_MOE = DEPTH // 2

A_SPLITS = (D_MODEL, D_MODEL, D_MODEL, RW_DECAY_LORA, RW_AAA_LORA)
B_SPLITS = (3 * D_MODEL, GDN_HEADS, GDN_HEADS)
C_SPLITS = (MLA_Q_RANK, MLA_KV_RANK, IDX_HEAD, IDX_HEADS)
A_IN = sum(A_SPLITS)
B_IN = sum(B_SPLITS)
C_IN = sum(C_SPLITS)
G_IN = 3 * D_MODEL
D_IN = A_IN + B_IN + C_IN + G_IN

kernel_name = 'hybrid_rwkv7_gdn_dsa_moe_block'


def _split(t, sizes):
    offs = np.cumsum(sizes)[:-1].tolist()
    return jnp.split(t, offs, axis=-1)


def rms_norm(x, g, eps=EPS):
    xf = x.astype(jnp.float32)
    y = xf * lax.rsqrt(jnp.mean(xf * xf, axis=-1, keepdims=True) + eps)
    return (y * g).astype(x.dtype)


def layer_norm(x, g, b, eps=EPS):
    xf = x.astype(jnp.float32)
    mu = jnp.mean(xf, axis=-1, keepdims=True)
    var = jnp.mean(jnp.square(xf - mu), axis=-1, keepdims=True)
    return ((xf - mu) * lax.rsqrt(var + eps) * g + b).astype(x.dtype)


def l2norm(x, eps=EPS):
    return x * lax.rsqrt(jnp.sum(x * x, axis=-1, keepdims=True) + eps)


def token_shift(x):
    return jnp.pad(x, ((0, 0), (1, 0), (0, 0)))[:, :-1]


def causal_dwconv(x, w):
    K, C = w.shape
    return lax.conv_general_dilated(x, w[:, None, :], window_strides=(1,), padding=[(K - 1, 0)],
                                    dimension_numbers=('NWC', 'WIO', 'NWC'), feature_group_count=C)


def rwkv7_branch(p_a, v_first, mu, w0, w2, a0, a2, k_k, k_a, r_k, gn_g, gn_b, vres):
    B, T, _ = p_a.shape
    H, N = RW_HEADS, RW_HEAD
    p_a = p_a + (token_shift(p_a) - p_a) * mu
    r, k, v, xw, xa = _split(p_a, A_SPLITS)
    log_w = -jnp.exp(-jax.nn.softplus(-(w0 + jnp.tanh(xw) @ w2).astype(jnp.float32)) - 0.5)
    a = jax.nn.sigmoid(a0 + xa @ a2)
    if vres is None:
        v_first = v
    else:
        v0, v1, v2 = vres
        v = v + (v_first - v) * jax.nn.sigmoid(v0 + (v @ v1) @ v2)
    heads = lambda t: t.reshape(B, T, H, N).astype(jnp.float32)
    kk = l2norm(heads(k * k_k))
    k = k * (1 + (a - 1) * k_a)
    r4, k4, v4, a4 = heads(r), heads(k), heads(v), heads(a)
    w4 = jnp.exp(log_w).reshape(B, T, H, N)

    def step(S, inp):
        r_t, w_t, k_t, v_t, kk_t, a_t = inp
        sa = jnp.einsum('bhvk,bhk->bhv', S, -kk_t)
        S = (S * w_t[:, :, None, :] + sa[..., None] * (kk_t * a_t)[:, :, None, :]
             + v_t[..., None] * k_t[:, :, None, :])
        return S, jnp.einsum('bhvk,bhk->bhv', S, r_t)

    xs = tuple(jnp.moveaxis(t, 1, 0) for t in (r4, w4, k4, v4, kk, a4))
    _, o = lax.scan(step, jnp.zeros((B, H, N, N), jnp.float32), xs)
    o = jnp.moveaxis(o, 0, 1)
    mean = jnp.mean(o, axis=-1, keepdims=True)
    var = jnp.mean(jnp.square(o - mean), axis=-1, keepdims=True)
    o = ((o - mean) * lax.rsqrt(var + RW_GN_EPS)).reshape(B, T, D_MODEL) * gn_g + gn_b
    bonus = jnp.sum(r4 * k4 * r_k, axis=-1, keepdims=True) * v4
    return (o + bonus.reshape(B, T, D_MODEL)).astype(p_a.dtype), v_first


def gated_deltanet_branch(p_b, conv_w, A_log, dt_bias, norm_g):
    B, T, _ = p_b.shape
    H, Dh, C = GDN_HEADS, GDN_HEAD, GDN_CHUNK
    n_chunks = T // C
    qkv, a_in, b_in = _split(p_b, B_SPLITS)
    qkv = jax.nn.silu(causal_dwconv(qkv, conv_w)).astype(jnp.float32)
    q, k, v = _split(qkv, (D_MODEL, D_MODEL, D_MODEL))
    q = l2norm(q.reshape(B, T, H, Dh)) * Dh ** -0.5
    k = l2norm(k.reshape(B, T, H, Dh))
    v = v.reshape(B, T, H, Dh)
    beta = jax.nn.sigmoid(b_in.astype(jnp.float32))
    g = -jnp.exp(A_log.astype(jnp.float32)) * jax.nn.softplus(a_in.astype(jnp.float32) + dt_bias)
    to_chunks = lambda t: jnp.moveaxis(t.reshape(B, n_chunks, C, H, -1), 3, 1)
    q, k, v = to_chunks(q), to_chunks(k), to_chunks(v)
    beta = to_chunks(beta[..., None])
    g = jnp.cumsum(to_chunks(g[..., None])[..., 0], axis=-1)
    incl = jnp.tril(jnp.ones((C, C), bool))
    strict = jnp.tril(jnp.ones((C, C), bool), -1)
    decay = jnp.exp(jnp.where(incl, g[..., :, None] - g[..., None, :], -jnp.inf))
    kb = k * beta
    Lmat = jnp.where(strict, jnp.einsum('bhnid,bhnjd->bhnij', kb, k) * decay, 0.0)
    rhs = jnp.concatenate([v * beta, kb * jnp.exp(g)[..., None]], axis=-1)
    sol = lax.linalg.triangular_solve(Lmat + jnp.eye(C, dtype=jnp.float32), rhs,
                                      left_side=True, lower=True, unit_diagonal=True)
    u, wc = sol[..., :Dh], sol[..., Dh:]
    A_intra = jnp.where(incl, jnp.einsum('bhnid,bhnjd->bhnij', q, k) * decay, 0.0)

    def step(S, inp):
        q_c, k_c, u_c, w_c, g_c, A_c = inp
        v_new = u_c - jnp.einsum('bhck,bhkv->bhcv', w_c, S)
        o_c = (jnp.einsum('bhck,bhkv->bhcv', q_c * jnp.exp(g_c)[..., None], S)
               + jnp.einsum('bhij,bhjv->bhiv', A_c, v_new))
        g_end = g_c[..., -1:]
        S = S * jnp.exp(g_end)[..., None] + jnp.einsum(
            'bhck,bhcv->bhkv', k_c * jnp.exp(g_end - g_c)[..., None], v_new)
        return S, o_c

    xs = tuple(jnp.moveaxis(t, 2, 0) for t in (q, k, u, wc, g, A_intra))
    _, o = lax.scan(step, jnp.zeros((B, H, Dh, Dh), jnp.float32), xs)
    o = jnp.transpose(o, (1, 0, 3, 2, 4)).reshape(B, T, H, Dh)
    return rms_norm(o, norm_g).reshape(B, T, D_MODEL).astype(p_b.dtype)


def dsa_mla_branch(p_c, q_norm, w_uq, kv_norm, w_uk, w_uv, w_qI, kI_g, kI_b):
    B, T, _ = p_c.shape
    c_q, c_kv, k_I, w_I = _split(p_c, C_SPLITS)
    c_q = rms_norm(c_q, q_norm)
    c_kv = rms_norm(c_kv, kv_norm)
    k_I = layer_norm(k_I, kI_g, kI_b)
    w_I = w_I * (IDX_HEADS ** -0.5 * IDX_HEAD ** -0.5)
    n_sel = min(TOPK_MAX, T // 4)
    key_pos = jnp.arange(T)

    def block(i):
        s0 = i * Q_BLOCK
        cq = lax.dynamic_slice_in_dim(c_q, s0, Q_BLOCK, axis=1)
        wI = lax.dynamic_slice_in_dim(w_I, s0, Q_BLOCK, axis=1)
        q_pos = s0 + jnp.arange(Q_BLOCK)
        qI = (cq @ w_qI).reshape(B, Q_BLOCK, IDX_HEADS, IDX_HEAD)
        score = jax.nn.relu(jnp.einsum('bqhd,bsd->bqhs', qI, k_I))
        score = jnp.einsum('bqhs,bqh->bqs', score, wI).astype(jnp.float32)
        causal = key_pos[None, :] <= q_pos[:, None]
        score = jnp.where(causal[None], score, -jnp.inf)
        _, sel = lax.top_k(score, n_sel)
        valid = sel <= q_pos[None, :, None]
        kv_sel = jax.vmap(lambda c, s: c[s])(c_kv, sel)
        q = (cq @ w_uq).reshape(B, Q_BLOCK, MLA_HEADS, MLA_HEAD)
        q_lat = jnp.einsum('bqhd,rhd->bqhr', q, w_uk) * MLA_HEAD ** -0.5
        logits = jnp.einsum('bqhr,bqkr->bqhk', q_lat, kv_sel).astype(jnp.float32)
        logits = jnp.where(valid[:, :, None, :], logits, -jnp.inf)
        prob = jax.nn.softmax(logits, axis=-1).astype(kv_sel.dtype)
        o_lat = jnp.einsum('bqhk,bqkr->bqhr', prob, kv_sel)
        return jnp.einsum('bqhr,rhd->bqhd', o_lat, w_uv)

    out = lax.map(block, jnp.arange(T // Q_BLOCK))
    return jnp.moveaxis(out, 0, 1).reshape(B, T, MLA_HEADS * MLA_HEAD)


def memory_xattn(h, mem_n, w_q, w_kv, w_o):
    B, T, _ = h.shape
    M = mem_n.shape[1]
    q = (h @ w_q).reshape(B, T, XA_HEADS, XA_HEAD)
    k, v = _split(mem_n @ w_kv, (D_MODEL, D_MODEL))
    k = k.reshape(B, M, XA_HEADS, XA_HEAD)
    v = v.reshape(B, M, XA_HEADS, XA_HEAD)
    logits = jnp.einsum('bthd,bmhd->bhtm', q, k).astype(jnp.float32) * XA_HEAD ** -0.5
    p = jax.nn.softmax(logits, axis=-1).astype(v.dtype)
    o = jnp.einsum('bhtm,bmhd->bthd', p, v).reshape(B, T, D_MODEL)
    return o @ w_o


def swiglu(h, w_gate, w_up, w_down):
    return (jax.nn.silu(h @ w_gate) * (h @ w_up)) @ w_down


def moe_swiglu(h, router, w_gate, w_up, w_down):
    logits = (h @ router).astype(jnp.float32)
    top_val, top_idx = lax.top_k(logits, TOP_K)
    top_p = jax.nn.softmax(top_val, axis=-1)
    gates = jnp.einsum('btk,btke->bte', top_p, jax.nn.one_hot(top_idx, N_EXPERTS, dtype=jnp.float32))
    out = jnp.zeros(h.shape, jnp.float32)
    for e in range(N_EXPERTS):
        out = out + gates[..., e:e + 1] * swiglu(h, w_gate[e], w_up[e], w_down[e])
    return out.astype(h.dtype)


def setup_inputs(seed: int = 0) -> dict:
    key = jax.random.key(seed)
    ks = iter(jax.random.split(key, 64))
    f32 = jnp.float32

    def nrm(shape, scale):
        return jax.random.normal(next(ks), shape, f32) * scale

    def gain(shape):
        return 1.0 + nrm(shape, 0.02)

    L, D = DEPTH, D_MODEL
    dt = jnp.exp(jax.random.uniform(next(ks), (L, GDN_HEADS)) * (math.log(0.1) - math.log(0.001))
                 + math.log(0.001))
    return {
        'x': nrm((BATCH, SEQ, D), 1.0),
        'mem': nrm((BATCH, MEM_LEN, D), 1.0),
        'norm_mix': gain((L, D)),
        'w_in': nrm((L, D, D_IN), D ** -0.5),
        'rw_mu': jax.random.uniform(next(ks), (L, A_IN), f32),
        'rw_w0': nrm((L, D), 0.5),
        'rw_w2': nrm((L, RW_DECAY_LORA, D), 0.1 * RW_DECAY_LORA ** -0.5),
        'rw_a0': nrm((L, D), 0.1),
        'rw_a2': nrm((L, RW_AAA_LORA, D), 0.1 * RW_AAA_LORA ** -0.5),
        'rw_k_k': 0.85 + nrm((L, D), 0.05),
        'rw_k_a': 1.0 + nrm((L, D), 0.05),
        'rw_r_k': nrm((L, RW_HEADS, RW_HEAD), 0.1),
        'rw_gn_g': gain((L, D)),
        'rw_gn_b': nrm((L, D), 0.01),
        'rw_v0': nrm((L - 1, D), 0.5),
        'rw_v1': nrm((L - 1, D, RW_MV_LORA), D ** -0.5),
        'rw_v2': nrm((L - 1, RW_MV_LORA, D), 0.5 * RW_MV_LORA ** -0.5),
        'gdn_conv': nrm((L, GDN_CONV, 3 * D), 0.5),
        'gdn_A_log': jnp.log(jax.random.uniform(next(ks), (L, GDN_HEADS), f32, 1.0, 16.0)),
        'gdn_dt_bias': dt + jnp.log(-jnp.expm1(-dt)),
        'gdn_norm_g': gain((L, GDN_HEAD)),
        'mla_q_norm': gain((L, MLA_Q_RANK)),
        'mla_w_uq': nrm((L, MLA_Q_RANK, MLA_HEADS * MLA_HEAD), MLA_Q_RANK ** -0.5),
        'mla_kv_norm': gain((L, MLA_KV_RANK)),
        'mla_w_uk': nrm((L, MLA_KV_RANK, MLA_HEADS, MLA_HEAD), MLA_KV_RANK ** -0.5),
        'mla_w_uv': nrm((L, MLA_KV_RANK, MLA_HEADS, MLA_HEAD), MLA_KV_RANK ** -0.5),
        'idx_w_q': nrm((L, MLA_Q_RANK, IDX_HEADS * IDX_HEAD), MLA_Q_RANK ** -0.5),
        'idx_k_g': gain((L, IDX_HEAD)),
        'idx_k_b': nrm((L, IDX_HEAD), 0.01),
        'w_mix_out': nrm((L, D, D), D ** -0.5),
        'mem_norm': gain((D,)),
        'norm_xattn': gain((L, D)),
        'xa_w_q': nrm((L, D, D), D ** -0.5),
        'xa_w_kv': nrm((L, D, 2 * D), D ** -0.5),
        'xa_w_o': nrm((L, D, D), D ** -0.5),
        'norm_ffn': gain((L, D)),
        'ffn_w_gate': nrm((N_DENSE, D, D_FF), D ** -0.5),
        'ffn_w_up': nrm((N_DENSE, D, D_FF), D ** -0.5),
        'ffn_w_down': nrm((N_DENSE, D_FF, D), D_FF ** -0.5),
        'moe_router': nrm((N_MOE, D, N_EXPERTS), D ** -0.5),
        'moe_w_gate': nrm((N_MOE, N_EXPERTS, D, D_FF_EXPERT), D ** -0.5),
        'moe_w_up': nrm((N_MOE, N_EXPERTS, D, D_FF_EXPERT), D ** -0.5),
        'moe_w_down': nrm((N_MOE, N_EXPERTS, D_FF_EXPERT, D), D_FF_EXPERT ** -0.5),
        'final_norm': gain((D,)),
    }


def reference(x, mem, norm_mix, w_in, rw_mu, rw_w0, rw_w2, rw_a0, rw_a2, rw_k_k, rw_k_a, rw_r_k,
              rw_gn_g, rw_gn_b, rw_v0, rw_v1, rw_v2, gdn_conv, gdn_A_log, gdn_dt_bias, gdn_norm_g,
              mla_q_norm, mla_w_uq, mla_kv_norm, mla_w_uk, mla_w_uv, idx_w_q, idx_k_g, idx_k_b,
              w_mix_out, mem_norm, norm_xattn, xa_w_q, xa_w_kv, xa_w_o, norm_ffn, ffn_w_gate,
              ffn_w_up, ffn_w_down, moe_router, moe_w_gate, moe_w_up, moe_w_down, final_norm):
    mem_n = rms_norm(mem, mem_norm)
    v_first = None
    for l in range(DEPTH):
        h = rms_norm(x, norm_mix[l])
        p = h @ w_in[l]
        p_a, p_b, p_c, p_g = _split(p, (A_IN, B_IN, C_IN, G_IN))
        vres = None if l == 0 else (rw_v0[l - 1], rw_v1[l - 1], rw_v2[l - 1])
        y_a, v_first = rwkv7_branch(p_a, v_first, rw_mu[l], rw_w0[l], rw_w2[l], rw_a0[l], rw_a2[l],
                                    rw_k_k[l], rw_k_a[l], rw_r_k[l], rw_gn_g[l], rw_gn_b[l], vres)
        y_b = gated_deltanet_branch(p_b, gdn_conv[l], gdn_A_log[l], gdn_dt_bias[l], gdn_norm_g[l])
        y_c = dsa_mla_branch(p_c, mla_q_norm[l], mla_w_uq[l], mla_kv_norm[l], mla_w_uk[l], mla_w_uv[l],
                             idx_w_q[l], idx_k_g[l], idx_k_b[l])
        g_a, g_b, g_c = _split(jax.nn.sigmoid(p_g), (D_MODEL, D_MODEL, D_MODEL))
        x = x + (g_a * y_a + g_b * y_b + g_c * y_c) @ w_mix_out[l]
        x = x + memory_xattn(rms_norm(x, norm_xattn[l]), mem_n, xa_w_q[l], xa_w_kv[l], xa_w_o[l])
        h = rms_norm(x, norm_ffn[l])
        i = l // 2
        if l % 2 == 0:
            x = x + swiglu(h, ffn_w_gate[i], ffn_w_up[i], ffn_w_down[i])
        else:
            x = x + moe_swiglu(h, moe_router[i], moe_w_gate[i], moe_w_up[i], moe_w_down[i])
    return rms_norm(x, final_norm)
```

```python
import functools
import math

import jax
import jax.numpy as jnp
from jax import lax
from jax.experimental import pallas as pl
from jax.experimental.pallas import tpu as pltpu

F32 = jnp.float32
BF16 = jnp.bfloat16
HIGHEST = lax.Precision.HIGHEST

D_MODEL = 1024
EPS = 1e-6
LANES = 128
VMEM_LIMIT = 48 * 1024 * 1024

RW_HEAD = 64
RW_HEADS = D_MODEL // RW_HEAD
RW_DECAY_LORA = 64
RW_AAA_LORA = 64
RW_MV_LORA = 32
RW_GN_EPS = 64e-5
RW_CHUNK = 64

GDN_HEAD = 128
GDN_HEADS = D_MODEL // GDN_HEAD
GDN_CONV = 4
GDN_CHUNK = 64

MLA_HEADS = 8
MLA_HEAD = D_MODEL // MLA_HEADS
MLA_Q_RANK = 256
MLA_KV_RANK = 256
IDX_HEADS = 8
IDX_HEAD = 64
TOPK_MAX = 256

XA_HEADS = 4
XA_HEAD = D_MODEL // XA_HEADS

N_EXPERTS = 8
TOP_K = 2

A_SPLITS = (D_MODEL, D_MODEL, D_MODEL, RW_DECAY_LORA, RW_AAA_LORA)
B_SPLITS = (3 * D_MODEL, GDN_HEADS, GDN_HEADS)
C_SPLITS = (MLA_Q_RANK, MLA_KV_RANK, IDX_HEAD, IDX_HEADS)
A_IN = sum(A_SPLITS)
B_IN = sum(B_SPLITS)
C_IN = sum(C_SPLITS)
G_IN = 3 * D_MODEL

NEG_BIG = -1e30


def _cparams(*sem):
    return pltpu.CompilerParams(dimension_semantics=sem, vmem_limit_bytes=VMEM_LIMIT)


def _round_up(n, m):
    return (n + m - 1) // m * m


def _pick_tile(n, pref):
    t = min(n, pref)
    while n % t:
        t -= 8
    return t


def _dot(a, b):
    return jnp.dot(a.astype(BF16), b.astype(BF16), preferred_element_type=F32)


def _dot_nt(a, b):
    return lax.dot_general(a.astype(BF16), b.astype(BF16), (((1,), (1,)), ((), ())),
                           preferred_element_type=F32)


def _dot_tn(a, b):
    return lax.dot_general(a.astype(BF16), b.astype(BF16), (((0,), (0,)), ((), ())),
                           preferred_element_type=F32)


def _dot_hi(a, b):
    return jnp.dot(a, b, preferred_element_type=F32, precision=HIGHEST)


def _nilpotent_inverse(m, eye, size):
    x = eye + m
    p = _dot_hi(m, m)
    span = 2
    while span < size:
        x = x + _dot_hi(x, p)
        span *= 2
        if span < size:
            p = _dot_hi(p, p)
    return x


def _mm_kernel(*refs, has_norm, has_res):
    it = iter(refs)
    x_ref = next(it)
    w_ref = next(it)
    g_ref = next(it) if has_norm else None
    r_ref = next(it) if has_res else None
    o_ref = next(it)
    xn_ref = next(it)

    @pl.when(pl.program_id(1) == 0)
    def _():
        x = x_ref[...].astype(F32)
        if has_norm:
            x = x * lax.rsqrt(jnp.mean(x * x, axis=-1, keepdims=True) + EPS) * g_ref[...]
        xn_ref[...] = x.astype(BF16)

    acc = jnp.dot(xn_ref[...], w_ref[...], preferred_element_type=F32)
    if has_res:
        acc = acc + r_ref[...]
    o_ref[...] = acc.astype(o_ref.dtype)


def _mm(x, w, norm_g=None, residual=None, tm=512, tn=1024):
    M, K = x.shape
    N = w.shape[1]
    Np = _round_up(N, LANES)
    wb = w.astype(BF16)
    if Np != N:
        wb = jnp.pad(wb, ((0, 0), (0, Np - N)))
    tm = _pick_tile(M, tm)
    tn = min(Np, tn)
    while Np % tn:
        tn -= LANES
    args = [x, wb]
    in_specs = [pl.BlockSpec((tm, K), lambda i, j: (i, 0)),
                pl.BlockSpec((K, tn), lambda i, j: (0, j))]
    if norm_g is not None:
        args.append(norm_g.reshape(1, K).astype(F32))
        in_specs.append(pl.BlockSpec((1, K), lambda i, j: (0, 0)))
    if residual is not None:
        assert Np == N
        args.append(residual)
        in_specs.append(pl.BlockSpec((tm, tn), lambda i, j: (i, j)))
    out = pl.pallas_call(
        functools.partial(_mm_kernel, has_norm=norm_g is not None, has_res=residual is not None),
        grid=(M // tm, Np // tn),
        in_specs=in_specs,
        out_specs=pl.BlockSpec((tm, tn), lambda i, j: (i, j)),
        out_shape=jax.ShapeDtypeStruct((M, Np), F32),
        scratch_shapes=[pltpu.VMEM((tm, K), BF16)],
        compiler_params=_cparams("parallel", "arbitrary"),
        name="mm",
    )(*args)
    return out[:, :N] if Np != N else out


def _rwkv_kernel(r_ref, lw_ref, k_ref, v_ref, kk_ref, a_ref, rk_ref, gg_ref, gb_ref, o_ref, s_ref,
                 *, chunk, heads_per_block):
    C, HB, N = chunk, heads_per_block, RW_HEAD

    @pl.when(pl.program_id(2) == 0)
    def _():
        s_ref[...] = jnp.zeros_like(s_ref)

    n_chunks = r_ref.shape[1] // C
    row = lax.broadcasted_iota(jnp.int32, (C, C), 0)
    col = lax.broadcasted_iota(jnp.int32, (C, C), 1)
    incl = row >= col
    strict = row > col
    tril_f = incl.astype(F32)
    eye = (row == col).astype(F32)
    rk_all = rk_ref[...]
    gg_all = gg_ref[...]
    gb_all = gb_ref[...]

    def chunk_body(c, carry):
        sl = pl.ds(pl.multiple_of(c * C, C), C)
        r_all = r_ref[0, sl, :]
        lw_all = lw_ref[0, sl, :]
        k_all = k_ref[0, sl, :]
        v_all = v_ref[0, sl, :]
        kk_all = kk_ref[0, sl, :]
        a_all = a_ref[0, sl, :]
        outs = []
        for hh in range(HB):
            cs = slice(hh * N, (hh + 1) * N)
            r, lw, k, v, kk, a = (t[:, cs] for t in (r_all, lw_all, k_all, v_all, kk_all, a_all))
            cum = _dot_hi(tril_f, lw)
            g = jnp.exp(cum)
            g_prev = jnp.exp(cum - lw)
            g_inv = jnp.exp(-cum)
            g_end = g[C - 1:C, :]
            a_bar = -kk * g_prev
            b_til = kk * a * g_inv
            k_til = k * g_inv
            r_bar = r * g
            lhs = jnp.concatenate([a_bar, r_bar], axis=0)
            rhs = jnp.concatenate([b_til, k_til], axis=0)
            pair = _dot_nt(lhs, rhs)
            l_ab = jnp.where(strict, pair[:C, :C], 0.0)
            l_ak = jnp.where(strict, pair[:C, C:], 0.0)
            a_rb = jnp.where(incl, pair[C:, :C], 0.0)
            a_rk = jnp.where(incl, pair[C:, C:], 0.0)
            t_inv = _nilpotent_inverse(l_ab, eye, C)
            s = s_ref[hh]
            from_state = _dot_nt(lhs, s)
            u = _dot(t_inv, from_state[:C] + _dot(l_ak, v))
            o = from_state[C:] + _dot(a_rk, v) + _dot(a_rb, u)
            vu = jnp.concatenate([v, u], axis=0)
            kb_hat = jnp.concatenate([k_til * g_end, b_til * g_end], axis=0)
            s_ref[hh] = s * g_end + _dot_tn(vu, kb_hat)
            mean = jnp.mean(o, axis=-1, keepdims=True)
            var = jnp.mean(jnp.square(o - mean), axis=-1, keepdims=True)
            o = (o - mean) * lax.rsqrt(var + RW_GN_EPS) * gg_all[:, cs] + gb_all[:, cs]
            bonus = jnp.sum(r * k * rk_all[:, cs], axis=-1, keepdims=True) * v
            outs.append(o + bonus)
        o_ref[0, sl, :] = jnp.concatenate(outs, axis=-1) if HB > 1 else outs[0]
        return carry

    lax.fori_loop(0, n_chunks, chunk_body, 0)


def _rwkv_recurrence(r, lw, k, v, kk, a, r_k, gn_g, gn_b, tb=256, heads_per_block=2):
    B, T, D = r.shape
    HB = heads_per_block
    W = HB * RW_HEAD
    tb = _pick_tile(T, tb)
    seq = pl.BlockSpec((1, tb, W), lambda b, h, t: (b, t, h))
    vec = pl.BlockSpec((1, W), lambda b, h, t: (0, h))
    return pl.pallas_call(
        functools.partial(_rwkv_kernel, chunk=min(RW_CHUNK, tb), heads_per_block=HB),
        grid=(B, D // W, T // tb),
        in_specs=[seq] * 6 + [vec] * 3,
        out_specs=seq,
        out_shape=jax.ShapeDtypeStruct((B, T, D), F32),
        scratch_shapes=[pltpu.VMEM((HB, RW_HEAD, RW_HEAD), F32)],
        compiler_params=_cparams("parallel", "parallel", "arbitrary"),
        name="rwkv7",
    )(r, lw, k, v, kk, a, r_k.reshape(1, D), gn_g.reshape(1, D), gn_b.reshape(1, D))


def _gdn_kernel(q_ref, k_ref, v_ref, beta_ref, gcol_ref, grow_ref, ng_ref, o_ref, s_ref, *, chunk):
    C = chunk

    @pl.when(pl.program_id(2) == 0)
    def _():
        s_ref[...] = jnp.zeros_like(s_ref)

    n_chunks = q_ref.shape[1] // C
    row = lax.broadcasted_iota(jnp.int32, (C, C), 0)
    col = lax.broadcasted_iota(jnp.int32, (C, C), 1)
    incl = row >= col
    strict = row > col
    eye = (row == col).astype(F32)
    ng = ng_ref[...]

    def chunk_body(c, carry):
        sl = pl.ds(pl.multiple_of(c * C, C), C)
        q = q_ref[0, sl, :]
        k = k_ref[0, sl, :]
        v = v_ref[0, sl, :]
        beta = beta_ref[0, 0, sl, :]
        gcol = gcol_ref[0, 0, sl, :]
        grow = grow_ref[0, 0, 0, pl.ds(c, 1), :]
        kb = k * beta
        decay = jnp.exp(jnp.where(incl, gcol - grow, -jnp.inf))
        pair = _dot_nt(jnp.concatenate([kb, q], axis=0), k)
        l_mat = jnp.where(strict, pair[:C] * decay, 0.0)
        a_intra = pair[C:] * decay
        t_inv = _nilpotent_inverse(-l_mat, eye, C)
        eg = jnp.exp(gcol)
        sol = _dot(t_inv, jnp.concatenate([v * beta, kb * eg], axis=-1))
        u = sol[:, :GDN_HEAD]
        w = sol[:, GDN_HEAD:]
        s = s_ref[...]
        from_state = _dot(jnp.concatenate([w, q * eg], axis=0), s)
        v_new = u - from_state[:C]
        o = from_state[C:] + _dot(a_intra, v_new)
        g_end = gcol[C - 1:C, :]
        s_ref[...] = s * jnp.exp(g_end) + _dot_tn(k * jnp.exp(g_end - gcol), v_new)
        o = o * lax.rsqrt(jnp.mean(o * o, axis=-1, keepdims=True) + EPS) * ng
        o_ref[0, sl, :] = o
        return carry

    lax.fori_loop(0, n_chunks, chunk_body, 0)


def _gdn_recurrence(q, k, v, beta, gcum, norm_g, tb=256):
    B, T, D = q.shape
    H, Dh = GDN_HEADS, GDN_HEAD
    tb = _pick_tile(T, tb)
    C = min(GDN_CHUNK, tb)
    beta_c = jnp.transpose(beta, (0, 2, 1))[..., None]
    gcol = jnp.transpose(gcum, (0, 2, 1))[..., None]
    grow = jnp.transpose(gcum, (0, 2, 1)).reshape(B, H, T // tb, tb // C, C)
    seq = pl.BlockSpec((1, tb, Dh), lambda b, h, t: (b, t, h))
    colspec = pl.BlockSpec((1, 1, tb, 1), lambda b, h, t: (b, h, t, 0))
    rowspec = pl.BlockSpec((1, 1, 1, tb // C, C), lambda b, h, t: (b, h, t, 0, 0))
    return pl.pallas_call(
        functools.partial(_gdn_kernel, chunk=C),
        grid=(B, H, T // tb),
        in_specs=[seq, seq, seq, colspec, colspec, rowspec,
                  pl.BlockSpec((1, Dh), lambda b, h, t: (0, 0))],
        out_specs=seq,
        out_shape=jax.ShapeDtypeStruct((B, T, D), F32),
        scratch_shapes=[pltpu.VMEM((Dh, Dh), F32)],
        compiler_params=_cparams("parallel", "parallel", "arbitrary"),
        name="gdn",
    )(q, k, v, beta_c, gcol, grow, norm_g.reshape(1, Dh))


def _split(t, sizes):
    offs = []
    acc = 0
    for s in sizes[:-1]:
        acc += s
        offs.append(acc)
    return jnp.split(t, offs, axis=-1)


def _token_shift(x):
    return jnp.pad(x, ((0, 0), (1, 0), (0, 0)))[:, :-1]


def _l2norm(x):
    return x * lax.rsqrt(jnp.sum(x * x, axis=-1, keepdims=True) + EPS)


def _rwkv_branch(p_a, v_first, mu, w0, w2, a0, a2, k_k, k_a, r_k, gn_g, gn_b, vres):
    B, T, _ = p_a.shape
    M = B * T
    H, N = RW_HEADS, RW_HEAD
    p_a = p_a + (_token_shift(p_a) - p_a) * mu
    r, k, v, xw, xa = _split(p_a, A_SPLITS)
    lora = lambda t, w: _mm(t.reshape(M, -1), w).reshape(B, T, -1)
    log_w = -jnp.exp(-jax.nn.softplus(-(w0 + lora(jnp.tanh(xw), w2))) - 0.5)
    a = jax.nn.sigmoid(a0 + lora(xa, a2))
    if vres is None:
        v_first = v
    else:
        v0, v1, v2 = vres
        v = v + (v_first - v) * jax.nn.sigmoid(v0 + lora(lora(v, v1), v2))
    kk = _l2norm((k * k_k).reshape(B, T, H, N)).reshape(B, T, D_MODEL)
    k = k * (1 + (a - 1) * k_a)
    y = _rwkv_recurrence(r, log_w, k, v, kk, a, r_k, gn_g, gn_b)
    return y, v_first


def _gdn_branch(p_b, conv_w, A_log, dt_bias, norm_g):
    B, T, _ = p_b.shape
    H, Dh, C = GDN_HEADS, GDN_HEAD, GDN_CHUNK
    qkv, a_in, b_in = _split(p_b, B_SPLITS)
    xp = jnp.pad(qkv, ((0, 0), (GDN_CONV - 1, 0), (0, 0)))
    conv = sum(xp[:, j:j + T] * conv_w[j] for j in range(GDN_CONV))
    qkv = jax.nn.silu(conv)
    q, k, v = _split(qkv, (D_MODEL, D_MODEL, D_MODEL))
    q = (_l2norm(q.reshape(B, T, H, Dh)) * Dh ** -0.5).reshape(B, T, D_MODEL)
    k = _l2norm(k.reshape(B, T, H, Dh)).reshape(B, T, D_MODEL)
    beta = jax.nn.sigmoid(b_in)
    g = -jnp.exp(A_log) * jax.nn.softplus(a_in + dt_bias)
    Cc = min(C, T)
    gcum = jnp.cumsum(g.reshape(B, T // Cc, Cc, H), axis=2).reshape(B, T, H)
    return _gdn_recurrence(q, k, v, beta, gcum, norm_g)


def _dsa_kernel(cq_ref, wi_ref, ki_ref, ckv_ref, wqi_ref, wuq_ref, wuk_ref, wuv_ref, o_ref,
                key_scr, m_scr, l_scr, acc_scr, *, n_sel, tq, tk, pos_bits):
    H = IDX_HEADS
    q0 = pl.program_id(1) * tq
    nk = (q0 + tq + tk - 1) // tk
    fold = tk // LANES
    int_min = jnp.int32(-2 ** 31)

    cq = cq_ref[0].astype(BF16)
    q_idx = _dot(cq, wqi_ref[...])
    qi_rows = jnp.concatenate([q_idx[:, h * IDX_HEAD:(h + 1) * IDX_HEAD] for h in range(H)],
                              axis=0).astype(BF16)
    wi = wi_ref[0]
    wi_rows = jnp.concatenate([wi[:, h:h + 1] for h in range(H)], axis=0)
    q_pos = q0 + lax.broadcasted_iota(jnp.int32, (tq, tk), 0)
    lane_pos = lax.broadcasted_iota(jnp.int32, (tq, tk), 1)

    def score_tile(kt, carry):
        ki = ki_ref[0, pl.ds(pl.multiple_of(kt * tk, tk), tk), :]
        z = jnp.maximum(_dot_nt(qi_rows, ki), 0.0) * wi_rows
        s = jnp.sum(z.reshape(H, tq, tk), axis=0)
        s = jnp.where(s == 0.0, 0.0, s)
        s = jnp.where(kt * tk + lane_pos <= q_pos, s, -jnp.inf)
        bits = pltpu.bitcast(s, jnp.int32)
        key_scr[kt] = bits ^ ((bits >> 31) & jnp.int32(0x7FFFFFFF))
        return carry

    lax.fori_loop(0, nk, score_tile, 0)

    def count(pred):
        def body(kt, acc):
            hit = jnp.where(pred(key_scr[kt], kt * tk + lane_pos), 1.0, 0.0)
            part = hit[:, :LANES]
            for f in range(1, fold):
                part = part + hit[:, f * LANES:(f + 1) * LANES]
            return acc + part
        acc = lax.fori_loop(0, nk, body, jnp.zeros((tq, LANES), F32))
        return jnp.sum(acc, axis=-1, keepdims=True)

    want = jnp.float32(n_sel)
    thr = jnp.where(count(lambda key, pos: key >= 0) >= want, jnp.int32(0), int_min)

    def thr_bit(i, thr):
        cand = thr | jnp.left_shift(jnp.int32(1), 30 - i)
        return jnp.where(count(lambda key, pos: key >= cand) >= want, cand, thr)

    thr = lax.fori_loop(0, 31, thr_bit, thr)
    need = want - count(lambda key, pos: key > thr)

    def pos_bit(i, last):
        cand = last + jnp.left_shift(jnp.int32(1), pos_bits - 1 - i)
        below = count(lambda key, pos: jnp.logical_and(key == thr, pos < cand))
        return jnp.where(below < need, cand, last)

    last = lax.fori_loop(0, pos_bits, pos_bit, jnp.zeros((tq, 1), jnp.int32))

    q = _dot(cq, wuq_ref[...])
    q_lat = jnp.concatenate(
        [_dot_nt(q[:, h * MLA_HEAD:(h + 1) * MLA_HEAD], wuk_ref[h]) for h in range(MLA_HEADS)],
        axis=0) * MLA_HEAD ** -0.5
    q_lat = q_lat.astype(BF16)
    m_scr[...] = jnp.full(m_scr.shape, NEG_BIG, F32)
    l_scr[...] = jnp.zeros(l_scr.shape, F32)
    acc_scr[...] = jnp.zeros(acc_scr.shape, F32)

    def attend_tile(kt, carry):
        ckv = ckv_ref[0, pl.ds(pl.multiple_of(kt * tk, tk), tk), :]
        key = key_scr[kt]
        pos = kt * tk + lane_pos
        sel = jnp.logical_or(key > thr, jnp.logical_and(key == thr, pos <= last))
        sel = jnp.logical_and(sel, pos <= q_pos)
        bias = jnp.where(sel, 0.0, NEG_BIG)
        logits = _dot_nt(q_lat, ckv).reshape(MLA_HEADS, tq, tk) + bias[None]
        m_old = m_scr[...]
        m_new = jnp.maximum(m_old, jnp.max(logits, axis=-1, keepdims=True))
        p = jnp.exp(logits - m_new)
        alpha = jnp.exp(m_old - m_new)
        l_scr[...] = alpha * l_scr[...] + jnp.sum(p, axis=-1, keepdims=True)
        pv = _dot(p.reshape(MLA_HEADS * tq, tk), ckv).reshape(MLA_HEADS, tq, MLA_KV_RANK)
        acc_scr[...] = alpha * acc_scr[...] + pv
        m_scr[...] = m_new
        return carry

    lax.fori_loop(0, nk, attend_tile, 0)
    o_lat = acc_scr[...] / l_scr[...]
    o_ref[0] = jnp.concatenate([_dot(o_lat[h], wuv_ref[h]) for h in range(MLA_HEADS)], axis=-1)


def _dsa_attention(c_q, w_i, k_i, c_kv, w_qi, w_uq, w_uk, w_uv, tq=128, tk=512):
    B, T, _ = c_q.shape
    tq = _pick_tile(T, tq)
    tk = _pick_tile(T, tk)
    n_sel = min(TOPK_MAX, T // 4)
    assert tk % LANES == 0 and tk >= n_sel and tk % tq == 0
    H = MLA_HEADS
    full = lambda *shape: pl.BlockSpec(shape, lambda b, i: (0,) * len(shape))
    return pl.pallas_call(
        functools.partial(_dsa_kernel, n_sel=n_sel, tq=tq, tk=tk, pos_bits=max(1, (T - 1).bit_length())),
        grid=(B, T // tq),
        in_specs=[pl.BlockSpec((1, tq, MLA_Q_RANK), lambda b, i: (b, i, 0)),
                  pl.BlockSpec((1, tq, IDX_HEADS), lambda b, i: (b, i, 0)),
                  pl.BlockSpec((1, T, IDX_HEAD), lambda b, i: (b, 0, 0)),
                  pl.BlockSpec((1, T, MLA_KV_RANK), lambda b, i: (b, 0, 0)),
                  full(MLA_Q_RANK, IDX_HEADS * IDX_HEAD),
                  full(MLA_Q_RANK, H * MLA_HEAD),
                  full(H, MLA_KV_RANK, MLA_HEAD),
                  full(H, MLA_KV_RANK, MLA_HEAD)],
        out_specs=pl.BlockSpec((1, tq, H * MLA_HEAD), lambda b, i: (b, i, 0)),
        out_shape=jax.ShapeDtypeStruct((B, T, H * MLA_HEAD), F32),
        scratch_shapes=[pltpu.VMEM((T // tk, tq, tk), jnp.int32),
                        pltpu.VMEM((H, tq, 1), F32),
                        pltpu.VMEM((H, tq, 1), F32),
                        pltpu.VMEM((H, tq, MLA_KV_RANK), F32)],
        compiler_params=_cparams("parallel", "arbitrary"),
        name="dsa",
    )(c_q, w_i, k_i, c_kv, w_qi.astype(BF16), w_uq.astype(BF16),
      jnp.transpose(w_uk, (1, 0, 2)).astype(BF16), jnp.transpose(w_uv, (1, 0, 2)).astype(BF16))


def _rms(x, g):
    return x * lax.rsqrt(jnp.mean(x * x, axis=-1, keepdims=True) + EPS) * g


def _dsa_branch(p_c, q_norm, w_uq, kv_norm, w_uk, w_uv, w_qi, ki_g, ki_b):
    c_q, c_kv, k_i, w_i = _split(p_c, C_SPLITS)
    c_q = _rms(c_q, q_norm)
    c_kv = _rms(c_kv, kv_norm)
    mu = jnp.mean(k_i, axis=-1, keepdims=True)
    var = jnp.mean(jnp.square(k_i - mu), axis=-1, keepdims=True)
    k_i = (k_i - mu) * lax.rsqrt(var + EPS) * ki_g + ki_b
    w_i = w_i * (IDX_HEADS ** -0.5 * IDX_HEAD ** -0.5)
    return _dsa_attention(c_q, w_i, k_i.astype(BF16), c_kv.astype(BF16), w_qi, w_uq, w_uk, w_uv)


def _mix_kernel(ya_ref, yb_ref, yc_ref, ga_ref, gb_ref, gc_ref, w_ref, x_ref, o_ref):
    mix = (jax.nn.sigmoid(ga_ref[...]) * ya_ref[...] + jax.nn.sigmoid(gb_ref[...]) * yb_ref[...]
           + jax.nn.sigmoid(gc_ref[...]) * yc_ref[...])
    o_ref[...] = x_ref[...] + jnp.dot(mix.astype(BF16), w_ref[...], preferred_element_type=F32)


def _mix_layer(x, y_a, y_b, y_c, p_g, w_out, tm=512):
    M, D = x.shape
    tm = _pick_tile(M, tm)
    tile = pl.BlockSpec((tm, D), lambda i: (i, 0))
    gate = lambda c: pl.BlockSpec((tm, D), lambda i: (i, c))
    return pl.pallas_call(
        _mix_kernel,
        grid=(M // tm,),
        in_specs=[tile, tile, tile, gate(0), gate(1), gate(2),
                  pl.BlockSpec((D, D), lambda i: (0, 0)), tile],
        out_specs=tile,
        out_shape=jax.ShapeDtypeStruct((M, D), F32),
        compiler_params=_cparams("parallel"),
        name="mix",
    )(y_a, y_b, y_c, p_g, p_g, p_g, w_out.astype(BF16), x)


def _xattn_kernel(x_ref, g_ref, wq_ref, k_ref, v_ref, wo_ref, o_ref):
    x = x_ref[0]
    h = x * lax.rsqrt(jnp.mean(x * x, axis=-1, keepdims=True) + EPS) * g_ref[...]
    q = jnp.dot(h.astype(BF16), wq_ref[...], preferred_element_type=F32)
    k = k_ref[0]
    v = v_ref[0]
    outs = []
    for hd in range(XA_HEADS):
        cs = slice(hd * XA_HEAD, (hd + 1) * XA_HEAD)
        logits = _dot_nt(q[:, cs], k[:, cs]) * XA_HEAD ** -0.5
        p = jnp.exp(logits - jnp.max(logits, axis=-1, keepdims=True))
        outs.append(_dot(p, v[:, cs]) / jnp.sum(p, axis=-1, keepdims=True))
    o = jnp.concatenate(outs, axis=-1)
    o_ref[0] = x + jnp.dot(o.astype(BF16), wo_ref[...], preferred_element_type=F32)


def _xattn_layer(x, mem, mem_norm, norm_g, w_q, w_kv, w_o, tq=512):
    B, T, D = x.shape
    Mm = mem.shape[1]
    kv = _mm(mem.reshape(B * Mm, D), w_kv, norm_g=mem_norm).reshape(B, Mm, 2 * D)
    tq = _pick_tile(T, tq)
    tile = pl.BlockSpec((1, tq, D), lambda b, i: (b, i, 0))
    wspec = pl.BlockSpec((D, D), lambda b, i: (0, 0))
    return pl.pallas_call(
        _xattn_kernel,
        grid=(B, T // tq),
        in_specs=[tile, pl.BlockSpec((1, D), lambda b, i: (0, 0)), wspec,
                  pl.BlockSpec((1, Mm, D), lambda b, i: (b, 0, 0)),
                  pl.BlockSpec((1, Mm, D), lambda b, i: (b, 0, 1)), wspec],
        out_specs=tile,
        out_shape=jax.ShapeDtypeStruct((B, T, D), F32),
        compiler_params=_cparams("parallel", "parallel"),
        name="xattn",
    )(x, norm_g.reshape(1, D), w_q.astype(BF16), kv, kv, w_o.astype(BF16))


def _ffn_kernel(*refs, n_experts):
    if n_experts:
        x_ref, g_ref, gates_ref, wg_ref, wu_ref, wd_ref, o_ref, h_scr, acc_scr = refs
        e, j = pl.program_id(1), pl.program_id(2)
        first = jnp.logical_and(e == 0, j == 0)
        last = jnp.logical_and(e == n_experts - 1, j == pl.num_programs(2) - 1)
        wg, wu, wd = wg_ref[0], wu_ref[0], wd_ref[0]
    else:
        x_ref, g_ref, wg_ref, wu_ref, wd_ref, o_ref, h_scr, acc_scr = refs
        j = pl.program_id(1)
        first = j == 0
        last = j == pl.num_programs(1) - 1
        wg, wu, wd = wg_ref[...], wu_ref[...], wd_ref[...]

    @pl.when(first)
    def _():
        x = x_ref[...]
        h = x * lax.rsqrt(jnp.mean(x * x, axis=-1, keepdims=True) + EPS) * g_ref[...]
        h_scr[...] = h.astype(BF16)
        acc_scr[...] = jnp.zeros_like(acc_scr)

    h = h_scr[...]
    gate = jnp.dot(h, wg, preferred_element_type=F32)
    up = jnp.dot(h, wu, preferred_element_type=F32)
    act = gate * jax.nn.sigmoid(gate) * up
    if n_experts:
        lane = lax.broadcasted_iota(jnp.int32, gates_ref.shape, 1)
        act = act * jnp.sum(jnp.where(lane == e, gates_ref[...], 0.0), axis=-1, keepdims=True)
    acc_scr[...] += jnp.dot(act.astype(BF16), wd, preferred_element_type=F32)

    @pl.when(last)
    def _():
        o_ref[...] = x_ref[...] + acc_scr[...]


def _ffn_tiles(M, F, tm, tf):
    tm = _pick_tile(M, tm)
    tf = min(F, tf)
    while F % tf or tf % LANES:
        tf -= LANES
    return tm, tf


def _ffn_layer(x, norm_g, w_gate, w_up, w_down, tm=512, tf=1408):
    B, T, D = x.shape
    M = B * T
    F = w_gate.shape[-1]
    tm, tf = _ffn_tiles(M, F, tm, tf)
    tile = pl.BlockSpec((tm, D), lambda i, j: (i, 0))
    out = pl.pallas_call(
        functools.partial(_ffn_kernel, n_experts=0),
        grid=(M // tm, F // tf),
        in_specs=[tile, pl.BlockSpec((1, D), lambda i, j: (0, 0)),
                  pl.BlockSpec((D, tf), lambda i, j: (0, j)),
                  pl.BlockSpec((D, tf), lambda i, j: (0, j)),
                  pl.BlockSpec((tf, D), lambda i, j: (j, 0))],
        out_specs=tile,
        out_shape=jax.ShapeDtypeStruct((M, D), F32),
        scratch_shapes=[pltpu.VMEM((tm, D), BF16), pltpu.VMEM((tm, D), F32)],
        compiler_params=_cparams("parallel", "arbitrary"),
        name="ffn",
    )(x.reshape(M, D), norm_g.reshape(1, D), w_gate.astype(BF16), w_up.astype(BF16), w_down.astype(BF16))
    return out.reshape(B, T, D)


def _router_kernel(x_ref, g_ref, w_ref, o_ref):
    x = x_ref[...]
    h = x * lax.rsqrt(jnp.mean(x * x, axis=-1, keepdims=True) + EPS) * g_ref[...]
    logits = _dot_hi(h, w_ref[...])
    lane = lax.broadcasted_iota(jnp.int32, logits.shape, 1).astype(F32)
    logits = jnp.where(lane < N_EXPERTS, logits, -jnp.inf)
    gates = jnp.zeros_like(logits)
    top = []
    for _ in range(TOP_K):
        m = jnp.max(logits, axis=-1, keepdims=True)
        idx = jnp.min(jnp.where(logits == m, lane, float(LANES)), axis=-1, keepdims=True)
        top.append((m, idx))
        logits = jnp.where(lane == idx, -jnp.inf, logits)
    m0 = top[0][0]
    ex = [jnp.exp(m - m0) for m, _ in top]
    denom = sum(ex)
    for (m, idx), e in zip(top, ex):
        gates = gates + jnp.where(lane == idx, e / denom, 0.0)
    o_ref[...] = gates


def _moe_layer(x, norm_g, router, w_gate, w_up, w_down, tm=512, tf=1792):
    B, T, D = x.shape
    M = B * T
    E, _, F = w_gate.shape
    tm, tf = _ffn_tiles(M, F, tm, tf)
    x2 = x.reshape(M, D)
    g2 = norm_g.reshape(1, D)
    gates = pl.pallas_call(
        _router_kernel,
        grid=(M // tm,),
        in_specs=[pl.BlockSpec((tm, D), lambda i: (i, 0)), pl.BlockSpec((1, D), lambda i: (0, 0)),
                  pl.BlockSpec((D, LANES), lambda i: (0, 0))],
        out_specs=pl.BlockSpec((tm, LANES), lambda i: (i, 0)),
        out_shape=jax.ShapeDtypeStruct((M, LANES), F32),
        compiler_params=_cparams("parallel"),
        name="router",
    )(x2, g2, jnp.pad(router, ((0, 0), (0, LANES - E))))
    tile = pl.BlockSpec((tm, D), lambda i, e, j: (i, 0))
    out = pl.pallas_call(
        functools.partial(_ffn_kernel, n_experts=E),
        grid=(M // tm, E, F // tf),
        in_specs=[tile, pl.BlockSpec((1, D), lambda i, e, j: (0, 0)),
                  pl.BlockSpec((tm, LANES), lambda i, e, j: (i, 0)),
                  pl.BlockSpec((1, D, tf), lambda i, e, j: (e, 0, j)),
                  pl.BlockSpec((1, D, tf), lambda i, e, j: (e, 0, j)),
                  pl.BlockSpec((1, tf, D), lambda i, e, j: (e, j, 0))],
        out_specs=tile,
        out_shape=jax.ShapeDtypeStruct((M, D), F32),
        scratch_shapes=[pltpu.VMEM((tm, D), BF16), pltpu.VMEM((tm, D), F32)],
        compiler_params=_cparams("parallel", "arbitrary", "arbitrary"),
        name="moe",
    )(x2, g2, gates, w_gate.astype(BF16), w_up.astype(BF16), w_down.astype(BF16))
    return out.reshape(B, T, D)


def _final_norm_kernel(x_ref, g_ref, o_ref):
    x = x_ref[...]
    o_ref[...] = x * lax.rsqrt(jnp.mean(x * x, axis=-1, keepdims=True) + EPS) * g_ref[...]


def _final_norm(x, g, tm=1024):
    M, D = x.shape
    tm = _pick_tile(M, tm)
    return pl.pallas_call(
        _final_norm_kernel,
        grid=(M // tm,),
        in_specs=[pl.BlockSpec((tm, D), lambda i: (i, 0)), pl.BlockSpec((1, D), lambda i: (0, 0))],
        out_specs=pl.BlockSpec((tm, D), lambda i: (i, 0)),
        out_shape=jax.ShapeDtypeStruct((M, D), F32),
        compiler_params=_cparams("parallel"),
        name="final_norm",
    )(x, g.reshape(1, D))


def kernel(x, mem, norm_mix, w_in, rw_mu, rw_w0, rw_w2, rw_a0, rw_a2, rw_k_k, rw_k_a, rw_r_k,
           rw_gn_g, rw_gn_b, rw_v0, rw_v1, rw_v2, gdn_conv, gdn_A_log, gdn_dt_bias, gdn_norm_g,
           mla_q_norm, mla_w_uq, mla_kv_norm, mla_w_uk, mla_w_uv, idx_w_q, idx_k_g, idx_k_b,
           w_mix_out, mem_norm, norm_xattn, xa_w_q, xa_w_kv, xa_w_o, norm_ffn, ffn_w_gate,
           ffn_w_up, ffn_w_down, moe_router, moe_w_gate, moe_w_up, moe_w_down, final_norm):
    B, T, D = x.shape
    M = B * T
    depth = w_in.shape[0]
    v_first = None
    col0 = (0, A_IN, A_IN + B_IN, A_IN + B_IN + C_IN)
    widths = (A_IN, B_IN, C_IN, G_IN)
    for l in range(depth):
        x2 = x.reshape(M, D)
        p_a, p_b, p_c, p_g = (
            _mm(x2, w_in[l][:, c:c + w], norm_g=norm_mix[l]) for c, w in zip(col0, widths))
        vres = None if l == 0 else (rw_v0[l - 1], rw_v1[l - 1], rw_v2[l - 1])
        y_a, v_first = _rwkv_branch(p_a.reshape(B, T, -1), v_first, rw_mu[l], rw_w0[l], rw_w2[l],
                                    rw_a0[l], rw_a2[l], rw_k_k[l], rw_k_a[l], rw_r_k[l],
                                    rw_gn_g[l], rw_gn_b[l], vres)
        y_b = _gdn_branch(p_b.reshape(B, T, -1), gdn_conv[l], gdn_A_log[l], gdn_dt_bias[l],
                          gdn_norm_g[l])
        y_c = _dsa_branch(p_c.reshape(B, T, -1), mla_q_norm[l], mla_w_uq[l], mla_kv_norm[l],
                          mla_w_uk[l], mla_w_uv[l], idx_w_q[l], idx_k_g[l], idx_k_b[l])
        x = _mix_layer(x2, y_a.reshape(M, D), y_b.reshape(M, D), y_c.reshape(M, D), p_g,
                       w_mix_out[l]).reshape(B, T, D)
        x = _xattn_layer(x, mem, mem_norm, norm_xattn[l], xa_w_q[l], xa_w_kv[l], xa_w_o[l])
        i = l // 2
        if l % 2 == 0:
            x = _ffn_layer(x, norm_ffn[l], ffn_w_gate[i], ffn_w_up[i], ffn_w_down[i])
        else:
            x = _moe_layer(x, norm_ffn[l], moe_router[i], moe_w_gate[i], moe_w_up[i], moe_w_down[i])
    return _final_norm(x.reshape(M, D), final_norm).reshape(B, T, D)
```

```python
import functools
import math

import jax
import jax.numpy as jnp
from jax import lax
from jax.experimental import pallas as pl
from jax.experimental.pallas import tpu as pltpu

F32 = jnp.float32
BF16 = jnp.bfloat16
HIGHEST = lax.Precision.HIGHEST

D_MODEL = 1024
EPS = 1e-6
LANES = 128
VMEM_LIMIT = 48 * 1024 * 1024

RW_HEAD = 64
RW_HEADS = D_MODEL // RW_HEAD
RW_DECAY_LORA = 64
RW_AAA_LORA = 64
RW_MV_LORA = 32
RW_GN_EPS = 64e-5
RW_CHUNK = 64

GDN_HEAD = 128
GDN_HEADS = D_MODEL // GDN_HEAD
GDN_CONV = 4
GDN_CHUNK = 64

MLA_HEADS = 8
MLA_HEAD = D_MODEL // MLA_HEADS
MLA_Q_RANK = 256
MLA_KV_RANK = 256
IDX_HEADS = 8
IDX_HEAD = 64
TOPK_MAX = 256

XA_HEADS = 4
XA_HEAD = D_MODEL // XA_HEADS

N_EXPERTS = 8
TOP_K = 2

A_SPLITS = (D_MODEL, D_MODEL, D_MODEL, RW_DECAY_LORA, RW_AAA_LORA)
B_SPLITS = (3 * D_MODEL, GDN_HEADS, GDN_HEADS)
C_SPLITS = (MLA_Q_RANK, MLA_KV_RANK, IDX_HEAD, IDX_HEADS)
A_IN = sum(A_SPLITS)
B_IN = sum(B_SPLITS)
C_IN = sum(C_SPLITS)
G_IN = 3 * D_MODEL

NEG_BIG = -1e30


def _cparams(*sem):
    return pltpu.CompilerParams(dimension_semantics=sem, vmem_limit_bytes=VMEM_LIMIT)


def _round_up(n, m):
    return (n + m - 1) // m * m


def _pick_tile(n, pref):
    t = min(n, pref)
    while n % t:
        t -= 8
    return t


def _dot(a, b):
    return jnp.dot(a.astype(BF16), b.astype(BF16), preferred_element_type=F32)


def _dot_nt(a, b):
    return lax.dot_general(a.astype(BF16), b.astype(BF16), (((1,), (1,)), ((), ())),
                           preferred_element_type=F32)


def _dot_tn(a, b):
    return lax.dot_general(a.astype(BF16), b.astype(BF16), (((0,), (0,)), ((), ())),
                           preferred_element_type=F32)


def _dot_hi(a, b):
    return jnp.dot(a, b, preferred_element_type=F32, precision=HIGHEST)


def _split_bf16(a):
    hi = a.astype(BF16)
    return hi, (a - hi.astype(F32)).astype(BF16)


def _dot_split(ah, al, bh, bl):
    n = ah.shape[0]
    top = jnp.dot(jnp.concatenate([ah, al], axis=0), bh, preferred_element_type=F32)
    return top[:n] + top[n:] + jnp.dot(ah, bl, preferred_element_type=F32)


def _nilpotent_inverse(ms, eye, size):
    splits = [_split_bf16(m) for m in ms]
    xs = [eye + m for m in ms]
    ps = [_dot_split(h, l, h, l) for h, l in splits]
    span = 2
    while span < size:
        splits = [_split_bf16(p) for p in ps]
        xsplits = [_split_bf16(x) for x in xs]
        xs = [x + _dot_split(xh, xl, h, l) for x, (xh, xl), (h, l) in zip(xs, xsplits, splits)]
        span *= 2
        if span < size:
            ps = [_dot_split(h, l, h, l) for h, l in splits]
    return xs


def _cumsum_rows(tril_bf16, x):
    hi = x.astype(BF16)
    r1 = x - hi.astype(F32)
    mid = r1.astype(BF16)
    lo = (r1 - mid.astype(F32)).astype(BF16)
    n = x.shape[1]
    parts = jnp.dot(tril_bf16, jnp.concatenate([hi, mid, lo], axis=1), preferred_element_type=F32)
    return parts[:, :n] + (parts[:, n:2 * n] + parts[:, 2 * n:])


def _mm_kernel(*refs, has_norm, has_res):
    it = iter(refs)
    x_ref = next(it)
    w_ref = next(it)
    g_ref = next(it) if has_norm else None
    r_ref = next(it) if has_res else None
    o_ref = next(it)
    xn_ref = next(it)

    @pl.when(pl.program_id(1) == 0)
    def _():
        x = x_ref[...].astype(F32)
        if has_norm:
            x = x * lax.rsqrt(jnp.mean(x * x, axis=-1, keepdims=True) + EPS) * g_ref[...]
        xn_ref[...] = x.astype(BF16)

    acc = jnp.dot(xn_ref[...], w_ref[...], preferred_element_type=F32)
    if has_res:
        acc = acc + r_ref[...]
    o_ref[...] = acc.astype(o_ref.dtype)


def _mm(x, w, norm_g=None, residual=None, tm=512, tn=1024):
    M, K = x.shape
    N = w.shape[1]
    Np = _round_up(N, LANES)
    wb = w.astype(BF16)
    if Np != N:
        wb = jnp.pad(wb, ((0, 0), (0, Np - N)))
    tm = _pick_tile(M, tm)
    tn = min(Np, tn)
    while Np % tn:
        tn -= LANES
    args = [x, wb]
    in_specs = [pl.BlockSpec((tm, K), lambda i, j: (i, 0)),
                pl.BlockSpec((K, tn), lambda i, j: (0, j))]
    if norm_g is not None:
        args.append(norm_g.reshape(1, K).astype(F32))
        in_specs.append(pl.BlockSpec((1, K), lambda i, j: (0, 0)))
    if residual is not None:
        assert Np == N
        args.append(residual)
        in_specs.append(pl.BlockSpec((tm, tn), lambda i, j: (i, j)))
    out = pl.pallas_call(
        functools.partial(_mm_kernel, has_norm=norm_g is not None, has_res=residual is not None),
        grid=(M // tm, Np // tn),
        in_specs=in_specs,
        out_specs=pl.BlockSpec((tm, tn), lambda i, j: (i, j)),
        out_shape=jax.ShapeDtypeStruct((M, Np), F32),
        scratch_shapes=[pltpu.VMEM((tm, K), BF16)],
        compiler_params=_cparams("parallel", "arbitrary"),
        name="mm",
    )(*args)
    return out[:, :N] if Np != N else out


def _rwkv_kernel(r_ref, lw_ref, k_ref, v_ref, kk_ref, a_ref, rk_ref, gg_ref, gb_ref, o_ref, s_ref,
                 *, chunk, heads_per_block):
    C, HB, N = chunk, heads_per_block, RW_HEAD

    @pl.when(pl.program_id(2) == 0)
    def _():
        s_ref[...] = jnp.zeros_like(s_ref)

    NP = HB // 2
    W = 2 * N
    n_chunks = r_ref.shape[1] // C
    row = lax.broadcasted_iota(jnp.int32, (C, C), 0)
    col = lax.broadcasted_iota(jnp.int32, (C, C), 1)
    incl = row >= col
    strict = row > col
    strict4 = jnp.concatenate([strict, strict, incl, incl], axis=0)
    tril_b = incl.astype(BF16)
    eye = (row == col).astype(F32)
    lane = lax.broadcasted_iota(jnp.int32, (1, W), 1)
    in_a = lane < N
    mask_a = in_a.astype(F32)
    mask_b = 1.0 - mask_a
    srow = lax.broadcasted_iota(jnp.int32, (W, W), 0)
    scol = lax.broadcasted_iota(jnp.int32, (W, W), 1)
    block_diag = ((srow < N) == (scol < N)).astype(F32)

    def halves(stacked):
        return jnp.where(in_a, stacked[:C], stacked[C:])

    def head_sum(x):
        sa = jnp.sum(x * mask_a, axis=-1, keepdims=True)
        sb = jnp.sum(x * mask_b, axis=-1, keepdims=True)
        return jnp.where(in_a, sa, sb)

    incl2 = jnp.concatenate([incl, incl], axis=0)

    def chunk_body(c, carry):
        sl = pl.ds(pl.multiple_of(c * C, C), C)
        P = range(NP)
        cols = [slice(p * W, (p + 1) * W) for p in P]
        lw = [lw_ref[0, sl, cs] for cs in cols]
        cum = [_cumsum_rows(tril_b, x) for x in lw]
        g = [jnp.exp(x) for x in cum]
        g_prev = [jnp.exp(x - y) for x, y in zip(cum, lw)]
        g_inv = [jnp.exp(-x) for x in cum]
        g_end = [x[C - 1:C, :] for x in g]
        kk = [kk_ref[0, sl, cs] for cs in cols]
        a_bar = [-x * y for x, y in zip(kk, g_prev)]
        b_til = [x * a_ref[0, sl, cs] * y for x, cs, y in zip(kk, cols, g_inv)]
        k_til = [k_ref[0, sl, cs] * y for cs, y in zip(cols, g_inv)]
        r_bar = [r_ref[0, sl, cs] * y for cs, y in zip(cols, g)]
        lhs = [jnp.concatenate([x, y], axis=0) for x, y in zip(a_bar, r_bar)]
        rhs = [jnp.concatenate([x, y], axis=0) for x, y in zip(b_til, k_til)]
        pair = [_dot_nt(jnp.concatenate([x * mask_a, x * mask_b], axis=0), y)
                for x, y in zip(lhs, rhs)]
        l_ab = [jnp.where(strict, pr[base:base + C, :C], 0.0) for pr in pair for base in (0, 2 * C)]
        t_inv = _nilpotent_inverse(l_ab, eye, C)
        on_v = [jnp.where(strict4, jnp.concatenate(
            [pr[:C, C:], pr[2 * C:3 * C, C:], pr[C:2 * C, C:], pr[3 * C:, C:]], axis=0), 0.0)
            for pr in pair]
        v = [v_ref[0, sl, cs] for cs in cols]
        from_v = [_dot(x, y) for x, y in zip(on_v, v)]
        s = [s_ref[p] for p in P]
        from_state = [_dot_nt(x, y) for x, y in zip(lhs, s)]
        u = [halves(_dot(jnp.concatenate([t_inv[2 * p], t_inv[2 * p + 1]], axis=0),
                         from_state[p][:C] + halves(from_v[p][:2 * C]))) for p in P]
        a_rb = [jnp.where(incl2, jnp.concatenate([pr[C:2 * C, :C], pr[3 * C:, :C]], axis=0), 0.0)
                for pr in pair]
        from_u = [_dot(x, y) for x, y in zip(a_rb, u)]
        upd = [_dot_tn(jnp.concatenate([v[p], u[p]], axis=0),
                       jnp.concatenate([k_til[p] * g_end[p], b_til[p] * g_end[p]], axis=0))
               for p in P]
        for p in P:
            s_ref[p] = s[p] * g_end[p] + upd[p] * block_diag
        for p in P:
            cs = cols[p]
            o = from_state[p][C:] + halves(from_v[p][2 * C:]) + halves(from_u[p])
            mean = head_sum(o) * (1.0 / N)
            cen = o - mean
            var = head_sum(cen * cen) * (1.0 / N)
            o = cen * lax.rsqrt(var + RW_GN_EPS) * gg_ref[:, cs] + gb_ref[:, cs]
            bonus = head_sum(r_ref[0, sl, cs] * k_ref[0, sl, cs] * rk_ref[:, cs]) * v[p]
            o_ref[0, sl, cs] = o + bonus
        return carry

    lax.fori_loop(0, n_chunks, chunk_body, 0)


def _rwkv_recurrence(r, lw, k, v, kk, a, r_k, gn_g, gn_b, tb=256, heads_per_block=16):
    B, T, D = r.shape
    HB = heads_per_block
    W = HB * RW_HEAD
    tb = _pick_tile(T, tb)
    seq = pl.BlockSpec((1, tb, W), lambda b, h, t: (b, t, h))
    vec = pl.BlockSpec((1, W), lambda b, h, t: (0, h))
    return pl.pallas_call(
        functools.partial(_rwkv_kernel, chunk=min(RW_CHUNK, tb), heads_per_block=HB),
        grid=(B, D // W, T // tb),
        in_specs=[seq] * 6 + [vec] * 3,
        out_specs=seq,
        out_shape=jax.ShapeDtypeStruct((B, T, D), F32),
        scratch_shapes=[pltpu.VMEM((HB // 2, 2 * RW_HEAD, 2 * RW_HEAD), F32)],
        compiler_params=_cparams("parallel", "parallel", "arbitrary"),
        name="rwkv7",
    )(r, lw, k, v, kk, a, r_k.reshape(1, D), gn_g.reshape(1, D), gn_b.reshape(1, D))


def _gdn_kernel(q_ref, k_ref, v_ref, beta_ref, gcol_ref, grow_ref, ng_ref, o_ref, s_ref, *, chunk):
    C = chunk

    @pl.when(pl.program_id(1) == 0)
    def _():
        s_ref[...] = jnp.zeros_like(s_ref)

    n_chunks = q_ref.shape[1] // C
    row = lax.broadcasted_iota(jnp.int32, (C, C), 0)
    col = lax.broadcasted_iota(jnp.int32, (C, C), 1)
    incl = row >= col
    strict = row > col
    eye = (row == col).astype(F32)
    ng = ng_ref[...]

    def chunk_body(c, carry):
        sl = pl.ds(pl.multiple_of(c * C, C), C)
        beta_all = beta_ref[0, sl, :]
        gcol_all = gcol_ref[0, sl, :]
        grow_all = grow_ref[0, 0, c]
        H = range(GDN_HEADS)
        cols = [slice(h * GDN_HEAD, (h + 1) * GDN_HEAD) for h in H]
        beta = [beta_all[:, h:h + 1] for h in H]
        gcol = [gcol_all[:, h:h + 1] for h in H]
        k = [k_ref[0, sl, cs] for cs in cols]
        kb = [x * y for x, y in zip(k, beta)]
        decay = [jnp.exp(jnp.where(incl, gcol[h] - grow_all[h:h + 1, :], -jnp.inf)) for h in H]
        pair = [_dot_nt(jnp.concatenate([kb[h], q_ref[0, sl, cols[h]]], axis=0), k[h])
                for h in H]
        t_inv = _nilpotent_inverse(
            [-jnp.where(strict, pair[h][:C] * decay[h], 0.0) for h in H], eye, C)
        eg = [jnp.exp(x) for x in gcol]
        sol = [_dot(t_inv[h], jnp.concatenate([v_ref[0, sl, cols[h]] * beta[h], kb[h] * eg[h]],
                                              axis=-1)) for h in H]
        s = [s_ref[h] for h in H]
        from_state = [_dot(jnp.concatenate([sol[h][:, GDN_HEAD:], q_ref[0, sl, cols[h]] * eg[h]],
                                           axis=0), s[h]) for h in H]
        v_new = [sol[h][:, :GDN_HEAD] - from_state[h][:C] for h in H]
        intra = [_dot(pair[h][C:] * decay[h], v_new[h]) for h in H]
        g_end = [x[C - 1:C, :] for x in gcol]
        upd = [_dot_tn(k[h] * jnp.exp(g_end[h] - gcol[h]), v_new[h]) for h in H]
        for h in H:
            s_ref[h] = s[h] * jnp.exp(g_end[h]) + upd[h]
        for h in H:
            o = from_state[h][C:] + intra[h]
            o = o * lax.rsqrt(jnp.mean(o * o, axis=-1, keepdims=True) + EPS) * ng
            o_ref[0, sl, cols[h]] = o
        return carry

    lax.fori_loop(0, n_chunks, chunk_body, 0)


def _gdn_recurrence(q, k, v, beta, gcum, norm_g, tb=256):
    B, T, D = q.shape
    H, Dh = GDN_HEADS, GDN_HEAD
    tb = _pick_tile(T, tb)
    C = min(GDN_CHUNK, tb)
    grow = jnp.transpose(gcum.reshape(B, T // tb, tb // C, C, H), (0, 1, 2, 4, 3))
    seq = pl.BlockSpec((1, tb, D), lambda b, t: (b, t, 0))
    colspec = pl.BlockSpec((1, tb, H), lambda b, t: (b, t, 0))
    rowspec = pl.BlockSpec((1, 1, tb // C, H, C), lambda b, t: (b, t, 0, 0, 0))
    return pl.pallas_call(
        functools.partial(_gdn_kernel, chunk=C),
        grid=(B, T // tb),
        in_specs=[seq, seq, seq, colspec, colspec, rowspec,
                  pl.BlockSpec((1, Dh), lambda b, t: (0, 0))],
        out_specs=seq,
        out_shape=jax.ShapeDtypeStruct((B, T, D), F32),
        scratch_shapes=[pltpu.VMEM((H, Dh, Dh), F32)],
        compiler_params=_cparams("parallel", "arbitrary"),
        name="gdn",
    )(q, k, v, beta, gcum, grow, norm_g.reshape(1, Dh))


def _split(t, sizes):
    offs = []
    acc = 0
    for s in sizes[:-1]:
        acc += s
        offs.append(acc)
    return jnp.split(t, offs, axis=-1)


def _token_shift(x):
    return jnp.pad(x, ((0, 0), (1, 0), (0, 0)))[:, :-1]


def _l2norm(x):
    return x * lax.rsqrt(jnp.sum(x * x, axis=-1, keepdims=True) + EPS)


def _rwkv_branch(p_a, v_first, mu, w0, w2, a0, a2, k_k, k_a, r_k, gn_g, gn_b, vres):
    B, T, _ = p_a.shape
    M = B * T
    H, N = RW_HEADS, RW_HEAD
    p_a = p_a + (_token_shift(p_a) - p_a) * mu
    r, k, v, xw, xa = _split(p_a, A_SPLITS)
    lora = lambda t, w: _mm(t.reshape(M, -1), w).reshape(B, T, -1)
    log_w = -jnp.exp(-jax.nn.softplus(-(w0 + lora(jnp.tanh(xw), w2))) - 0.5)
    a = jax.nn.sigmoid(a0 + lora(xa, a2))
    if vres is None:
        v_first = v
    else:
        v0, v1, v2 = vres
        v = v + (v_first - v) * jax.nn.sigmoid(v0 + lora(lora(v, v1), v2))
    kk = _l2norm((k * k_k).reshape(B, T, H, N)).reshape(B, T, D_MODEL)
    k = k * (1 + (a - 1) * k_a)
    y = _rwkv_recurrence(r, log_w, k, v, kk, a, r_k, gn_g, gn_b)
    return y, v_first


def _gdn_branch(p_b, conv_w, A_log, dt_bias, norm_g):
    B, T, _ = p_b.shape
    H, Dh, C = GDN_HEADS, GDN_HEAD, GDN_CHUNK
    qkv, a_in, b_in = _split(p_b, B_SPLITS)
    xp = jnp.pad(qkv, ((0, 0), (GDN_CONV - 1, 0), (0, 0)))
    conv = sum(xp[:, j:j + T] * conv_w[j] for j in range(GDN_CONV))
    qkv = jax.nn.silu(conv)
    q, k, v = _split(qkv, (D_MODEL, D_MODEL, D_MODEL))
    q = (_l2norm(q.reshape(B, T, H, Dh)) * Dh ** -0.5).reshape(B, T, D_MODEL)
    k = _l2norm(k.reshape(B, T, H, Dh)).reshape(B, T, D_MODEL)
    beta = jax.nn.sigmoid(b_in)
    g = -jnp.exp(A_log) * jax.nn.softplus(a_in + dt_bias)
    Cc = min(C, T)
    gcum = jnp.cumsum(g.reshape(B, T // Cc, Cc, H), axis=2).reshape(B, T, H)
    return _gdn_recurrence(q, k, v, beta, gcum, norm_g)


def _dsa_kernel(cq_ref, wi_ref, ki_ref, ckv_ref, wqi_ref, wuq_ref, wuk_ref, wuv_ref, o_ref,
                key_scr, m_scr, l_scr, acc_scr, *, n_sel, tq, tk, pos_bits):
    H = IDX_HEADS
    q0 = pl.program_id(1) * tq
    nk = (q0 + tq + tk - 1) // tk
    fold = tk // LANES
    int_min = jnp.int32(-2 ** 31)

    cq = cq_ref[0].astype(BF16)
    q_idx = _dot(cq, wqi_ref[...])
    qi_rows = jnp.concatenate([q_idx[:, h * IDX_HEAD:(h + 1) * IDX_HEAD] for h in range(H)],
                              axis=0).astype(BF16)
    wi = wi_ref[0]
    wi_rows = jnp.concatenate([wi[:, h:h + 1] for h in range(H)], axis=0)
    q_pos = q0 + lax.broadcasted_iota(jnp.int32, (tq, tk), 0)
    lane_pos = lax.broadcasted_iota(jnp.int32, (tq, tk), 1)

    def score_tile(kt, carry):
        ki = ki_ref[0, pl.ds(pl.multiple_of(kt * tk, tk), tk), :]
        z = jnp.maximum(_dot_nt(qi_rows, ki), 0.0) * wi_rows
        s = jnp.sum(z.reshape(H, tq, tk), axis=0)
        s = jnp.where(s == 0.0, 0.0, s)
        s = jnp.where(kt * tk + lane_pos <= q_pos, s, -jnp.inf)
        bits = pltpu.bitcast(s, jnp.int32)
        key_scr[kt] = bits ^ ((bits >> 31) & jnp.int32(0x7FFFFFFF))
        return carry

    lax.fori_loop(0, nk, score_tile, 0)

    def count(pred):
        def body(kt, acc):
            hit = jnp.where(pred(key_scr[kt], kt * tk + lane_pos), 1.0, 0.0)
            part = hit[:, :LANES]
            for f in range(1, fold):
                part = part + hit[:, f * LANES:(f + 1) * LANES]
            return acc + part
        acc = lax.fori_loop(0, nk, body, jnp.zeros((tq, LANES), F32))
        return jnp.sum(acc, axis=-1, keepdims=True)

    want = jnp.float32(n_sel)
    thr = jnp.where(count(lambda key, pos: key >= 0) >= want, jnp.int32(0), int_min)

    def thr_bit(i, thr):
        cand = thr | jnp.left_shift(jnp.int32(1), 30 - i)
        return jnp.where(count(lambda key, pos: key >= cand) >= want, cand, thr)

    thr = lax.fori_loop(0, 31, thr_bit, thr)
    need = want - count(lambda key, pos: key > thr)

    def pos_bit(i, last):
        cand = last + jnp.left_shift(jnp.int32(1), pos_bits - 1 - i)
        below = count(lambda key, pos: jnp.logical_and(key == thr, pos < cand))
        return jnp.where(below < need, cand, last)

    last = lax.fori_loop(0, pos_bits, pos_bit, jnp.zeros((tq, 1), jnp.int32))

    q = _dot(cq, wuq_ref[...])
    q_lat = jnp.concatenate(
        [_dot_nt(q[:, h * MLA_HEAD:(h + 1) * MLA_HEAD], wuk_ref[h]) for h in range(MLA_HEADS)],
        axis=0) * MLA_HEAD ** -0.5
    q_lat = q_lat.astype(BF16)
    m_scr[...] = jnp.full(m_scr.shape, NEG_BIG, F32)
    l_scr[...] = jnp.zeros(l_scr.shape, F32)
    acc_scr[...] = jnp.zeros(acc_scr.shape, F32)

    def attend_tile(kt, carry):
        ckv = ckv_ref[0, pl.ds(pl.multiple_of(kt * tk, tk), tk), :]
        key = key_scr[kt]
        pos = kt * tk + lane_pos
        sel = jnp.logical_or(key > thr, jnp.logical_and(key == thr, pos <= last))
        sel = jnp.logical_and(sel, pos <= q_pos)
        bias = jnp.where(sel, 0.0, NEG_BIG)
        logits = _dot_nt(q_lat, ckv).reshape(MLA_HEADS, tq, tk) + bias[None]
        m_old = m_scr[...]
        m_new = jnp.maximum(m_old, jnp.max(logits, axis=-1, keepdims=True))
        p = jnp.exp(logits - m_new)
        alpha = jnp.exp(m_old - m_new)
        l_scr[...] = alpha * l_scr[...] + jnp.sum(p, axis=-1, keepdims=True)
        pv = _dot(p.reshape(MLA_HEADS * tq, tk), ckv).reshape(MLA_HEADS, tq, MLA_KV_RANK)
        acc_scr[...] = alpha * acc_scr[...] + pv
        m_scr[...] = m_new
        return carry

    lax.fori_loop(0, nk, attend_tile, 0)
    o_lat = acc_scr[...] / l_scr[...]
    o_ref[0] = jnp.concatenate([_dot(o_lat[h], wuv_ref[h]) for h in range(MLA_HEADS)], axis=-1)


def _dsa_attention(c_q, w_i, k_i, c_kv, w_qi, w_uq, w_uk, w_uv, tq=128, tk=512):
    B, T, _ = c_q.shape
    tq = _pick_tile(T, tq)
    tk = _pick_tile(T, tk)
    n_sel = min(TOPK_MAX, T // 4)
    assert tk % LANES == 0 and tk >= n_sel and tk % tq == 0
    H = MLA_HEADS
    full = lambda *shape: pl.BlockSpec(shape, lambda b, i: (0,) * len(shape))
    return pl.pallas_call(
        functools.partial(_dsa_kernel, n_sel=n_sel, tq=tq, tk=tk, pos_bits=max(1, (T - 1).bit_length())),
        grid=(B, T // tq),
        in_specs=[pl.BlockSpec((1, tq, MLA_Q_RANK), lambda b, i: (b, i, 0)),
                  pl.BlockSpec((1, tq, IDX_HEADS), lambda b, i: (b, i, 0)),
                  pl.BlockSpec((1, T, IDX_HEAD), lambda b, i: (b, 0, 0)),
                  pl.BlockSpec((1, T, MLA_KV_RANK), lambda b, i: (b, 0, 0)),
                  full(MLA_Q_RANK, IDX_HEADS * IDX_HEAD),
                  full(MLA_Q_RANK, H * MLA_HEAD),
                  full(H, MLA_KV_RANK, MLA_HEAD),
                  full(H, MLA_KV_RANK, MLA_HEAD)],
        out_specs=pl.BlockSpec((1, tq, H * MLA_HEAD), lambda b, i: (b, i, 0)),
        out_shape=jax.ShapeDtypeStruct((B, T, H * MLA_HEAD), F32),
        scratch_shapes=[pltpu.VMEM((T // tk, tq, tk), jnp.int32),
                        pltpu.VMEM((H, tq, 1), F32),
                        pltpu.VMEM((H, tq, 1), F32),
                        pltpu.VMEM((H, tq, MLA_KV_RANK), F32)],
        compiler_params=_cparams("parallel", "arbitrary"),
        name="dsa",
    )(c_q, w_i, k_i, c_kv, w_qi.astype(BF16), w_uq.astype(BF16),
      jnp.transpose(w_uk, (1, 0, 2)).astype(BF16), jnp.transpose(w_uv, (1, 0, 2)).astype(BF16))


def _rms(x, g):
    return x * lax.rsqrt(jnp.mean(x * x, axis=-1, keepdims=True) + EPS) * g


def _dsa_branch(p_c, q_norm, w_uq, kv_norm, w_uk, w_uv, w_qi, ki_g, ki_b):
    c_q, c_kv, k_i, w_i = _split(p_c, C_SPLITS)
    c_q = _rms(c_q, q_norm)
    c_kv = _rms(c_kv, kv_norm)
    mu = jnp.mean(k_i, axis=-1, keepdims=True)
    var = jnp.mean(jnp.square(k_i - mu), axis=-1, keepdims=True)
    k_i = (k_i - mu) * lax.rsqrt(var + EPS) * ki_g + ki_b
    w_i = w_i * (IDX_HEADS ** -0.5 * IDX_HEAD ** -0.5)
    return _dsa_attention(c_q, w_i, k_i.astype(BF16), c_kv.astype(BF16), w_qi, w_uq, w_uk, w_uv)


def _mix_kernel(ya_ref, yb_ref, yc_ref, ga_ref, gb_ref, gc_ref, w_ref, x_ref, o_ref):
    mix = (jax.nn.sigmoid(ga_ref[...]) * ya_ref[...] + jax.nn.sigmoid(gb_ref[...]) * yb_ref[...]
           + jax.nn.sigmoid(gc_ref[...]) * yc_ref[...])
    o_ref[...] = x_ref[...] + jnp.dot(mix.astype(BF16), w_ref[...], preferred_element_type=F32)


def _mix_layer(x, y_a, y_b, y_c, p_g, w_out, tm=512):
    M, D = x.shape
    tm = _pick_tile(M, tm)
    tile = pl.BlockSpec((tm, D), lambda i: (i, 0))
    gate = lambda c: pl.BlockSpec((tm, D), lambda i: (i, c))
    return pl.pallas_call(
        _mix_kernel,
        grid=(M // tm,),
        in_specs=[tile, tile, tile, gate(0), gate(1), gate(2),
                  pl.BlockSpec((D, D), lambda i: (0, 0)), tile],
        out_specs=tile,
        out_shape=jax.ShapeDtypeStruct((M, D), F32),
        compiler_params=_cparams("parallel"),
        name="mix",
    )(y_a, y_b, y_c, p_g, p_g, p_g, w_out.astype(BF16), x)


def _xattn_kernel(x_ref, g_ref, wq_ref, k_ref, v_ref, wo_ref, o_ref):
    x = x_ref[0]
    h = x * lax.rsqrt(jnp.mean(x * x, axis=-1, keepdims=True) + EPS) * g_ref[...]
    q = jnp.dot(h.astype(BF16), wq_ref[...], preferred_element_type=F32)
    k = k_ref[0]
    v = v_ref[0]
    outs = []
    for hd in range(XA_HEADS):
        cs = slice(hd * XA_HEAD, (hd + 1) * XA_HEAD)
        logits = _dot_nt(q[:, cs], k[:, cs]) * XA_HEAD ** -0.5
        p = jnp.exp(logits - jnp.max(logits, axis=-1, keepdims=True))
        outs.append(_dot(p, v[:, cs]) / jnp.sum(p, axis=-1, keepdims=True))
    o = jnp.concatenate(outs, axis=-1)
    o_ref[0] = x + jnp.dot(o.astype(BF16), wo_ref[...], preferred_element_type=F32)


def _xattn_layer(x, mem, mem_norm, norm_g, w_q, w_kv, w_o, tq=512):
    B, T, D = x.shape
    Mm = mem.shape[1]
    kv = _mm(mem.reshape(B * Mm, D), w_kv, norm_g=mem_norm).reshape(B, Mm, 2 * D)
    tq = _pick_tile(T, tq)
    tile = pl.BlockSpec((1, tq, D), lambda b, i: (b, i, 0))
    wspec = pl.BlockSpec((D, D), lambda b, i: (0, 0))
    return pl.pallas_call(
        _xattn_kernel,
        grid=(B, T // tq),
        in_specs=[tile, pl.BlockSpec((1, D), lambda b, i: (0, 0)), wspec,
                  pl.BlockSpec((1, Mm, D), lambda b, i: (b, 0, 0)),
                  pl.BlockSpec((1, Mm, D), lambda b, i: (b, 0, 1)), wspec],
        out_specs=tile,
        out_shape=jax.ShapeDtypeStruct((B, T, D), F32),
        compiler_params=_cparams("parallel", "parallel"),
        name="xattn",
    )(x, norm_g.reshape(1, D), w_q.astype(BF16), kv, kv, w_o.astype(BF16))


def _ffn_kernel(*refs, n_experts):
    if n_experts:
        x_ref, g_ref, gates_ref, wg_ref, wu_ref, wd_ref, o_ref, h_scr, acc_scr = refs
        e, j = pl.program_id(1), pl.program_id(2)
        first = jnp.logical_and(e == 0, j == 0)
        last = jnp.logical_and(e == n_experts - 1, j == pl.num_programs(2) - 1)
        wg, wu, wd = wg_ref[0], wu_ref[0], wd_ref[0]
    else:
        x_ref, g_ref, wg_ref, wu_ref, wd_ref, o_ref, h_scr, acc_scr = refs
        j = pl.program_id(1)
        first = j == 0
        last = j == pl.num_programs(1) - 1
        wg, wu, wd = wg_ref[...], wu_ref[...], wd_ref[...]

    @pl.when(first)
    def _():
        x = x_ref[...]
        h = x * lax.rsqrt(jnp.mean(x * x, axis=-1, keepdims=True) + EPS) * g_ref[...]
        h_scr[...] = h.astype(BF16)
        acc_scr[...] = jnp.zeros_like(acc_scr)

    h = h_scr[...]
    gate = jnp.dot(h, wg, preferred_element_type=F32)
    up = jnp.dot(h, wu, preferred_element_type=F32)
    act = gate * jax.nn.sigmoid(gate) * up
    if n_experts:
        lane = lax.broadcasted_iota(jnp.int32, gates_ref.shape, 1)
        act = act * jnp.sum(jnp.where(lane == e, gates_ref[...], 0.0), axis=-1, keepdims=True)
    acc_scr[...] += jnp.dot(act.astype(BF16), wd, preferred_element_type=F32)

    @pl.when(last)
    def _():
        o_ref[...] = x_ref[...] + acc_scr[...]


def _ffn_tiles(M, F, tm, tf):
    tm = _pick_tile(M, tm)
    tf = min(F, tf)
    while F % tf or tf % LANES:
        tf -= LANES
    return tm, tf


def _ffn_layer(x, norm_g, w_gate, w_up, w_down, tm=512, tf=1408):
    B, T, D = x.shape
    M = B * T
    F = w_gate.shape[-1]
    tm, tf = _ffn_tiles(M, F, tm, tf)
    tile = pl.BlockSpec((tm, D), lambda i, j: (i, 0))
    out = pl.pallas_call(
        functools.partial(_ffn_kernel, n_experts=0),
        grid=(M // tm, F // tf),
        in_specs=[tile, pl.BlockSpec((1, D), lambda i, j: (0, 0)),
                  pl.BlockSpec((D, tf), lambda i, j: (0, j)),
                  pl.BlockSpec((D, tf), lambda i, j: (0, j)),
                  pl.BlockSpec((tf, D), lambda i, j: (j, 0))],
        out_specs=tile,
        out_shape=jax.ShapeDtypeStruct((M, D), F32),
        scratch_shapes=[pltpu.VMEM((tm, D), BF16), pltpu.VMEM((tm, D), F32)],
        compiler_params=_cparams("parallel", "arbitrary"),
        name="ffn",
    )(x.reshape(M, D), norm_g.reshape(1, D), w_gate.astype(BF16), w_up.astype(BF16), w_down.astype(BF16))
    return out.reshape(B, T, D)


def _router_kernel(x_ref, g_ref, w_ref, o_ref):
    x = x_ref[...]
    h = x * lax.rsqrt(jnp.mean(x * x, axis=-1, keepdims=True) + EPS) * g_ref[...]
    logits = _dot_hi(h, w_ref[...])
    lane = lax.broadcasted_iota(jnp.int32, logits.shape, 1).astype(F32)
    logits = jnp.where(lane < N_EXPERTS, logits, -jnp.inf)
    gates = jnp.zeros_like(logits)
    top = []
    for _ in range(TOP_K):
        m = jnp.max(logits, axis=-1, keepdims=True)
        idx = jnp.min(jnp.where(logits == m, lane, float(LANES)), axis=-1, keepdims=True)
        top.append((m, idx))
        logits = jnp.where(lane == idx, -jnp.inf, logits)
    m0 = top[0][0]
    ex = [jnp.exp(m - m0) for m, _ in top]
    denom = sum(ex)
    for (m, idx), e in zip(top, ex):
        gates = gates + jnp.where(lane == idx, e / denom, 0.0)
    o_ref[...] = gates


def _moe_layer(x, norm_g, router, w_gate, w_up, w_down, tm=512, tf=1792):
    B, T, D = x.shape
    M = B * T
    E, _, F = w_gate.shape
    tm, tf = _ffn_tiles(M, F, tm, tf)
    x2 = x.reshape(M, D)
    g2 = norm_g.reshape(1, D)
    gates = pl.pallas_call(
        _router_kernel,
        grid=(M // tm,),
        in_specs=[pl.BlockSpec((tm, D), lambda i: (i, 0)), pl.BlockSpec((1, D), lambda i: (0, 0)),
                  pl.BlockSpec((D, LANES), lambda i: (0, 0))],
        out_specs=pl.BlockSpec((tm, LANES), lambda i: (i, 0)),
        out_shape=jax.ShapeDtypeStruct((M, LANES), F32),
        compiler_params=_cparams("parallel"),
        name="router",
    )(x2, g2, jnp.pad(router, ((0, 0), (0, LANES - E))))
    tile = pl.BlockSpec((tm, D), lambda i, e, j: (i, 0))
    out = pl.pallas_call(
        functools.partial(_ffn_kernel, n_experts=E),
        grid=(M // tm, E, F // tf),
        in_specs=[tile, pl.BlockSpec((1, D), lambda i, e, j: (0, 0)),
                  pl.BlockSpec((tm, LANES), lambda i, e, j: (i, 0)),
                  pl.BlockSpec((1, D, tf), lambda i, e, j: (e, 0, j)),
                  pl.BlockSpec((1, D, tf), lambda i, e, j: (e, 0, j)),
                  pl.BlockSpec((1, tf, D), lambda i, e, j: (e, j, 0))],
        out_specs=tile,
        out_shape=jax.ShapeDtypeStruct((M, D), F32),
        scratch_shapes=[pltpu.VMEM((tm, D), BF16), pltpu.VMEM((tm, D), F32)],
        compiler_params=_cparams("parallel", "arbitrary", "arbitrary"),
        name="moe",
    )(x2, g2, gates, w_gate.astype(BF16), w_up.astype(BF16), w_down.astype(BF16))
    return out.reshape(B, T, D)


def _final_norm_kernel(x_ref, g_ref, o_ref):
    x = x_ref[...]
    o_ref[...] = x * lax.rsqrt(jnp.mean(x * x, axis=-1, keepdims=True) + EPS) * g_ref[...]


def _final_norm(x, g, tm=1024):
    M, D = x.shape
    tm = _pick_tile(M, tm)
    return pl.pallas_call(
        _final_norm_kernel,
        grid=(M // tm,),
        in_specs=[pl.BlockSpec((tm, D), lambda i: (i, 0)), pl.BlockSpec((1, D), lambda i: (0, 0))],
        out_specs=pl.BlockSpec((tm, D), lambda i: (i, 0)),
        out_shape=jax.ShapeDtypeStruct((M, D), F32),
        compiler_params=_cparams("parallel"),
        name="final_norm",
    )(x, g.reshape(1, D))


def kernel(x, mem, norm_mix, w_in, rw_mu, rw_w0, rw_w2, rw_a0, rw_a2, rw_k_k, rw_k_a, rw_r_k,
           rw_gn_g, rw_gn_b, rw_v0, rw_v1, rw_v2, gdn_conv, gdn_A_log, gdn_dt_bias, gdn_norm_g,
           mla_q_norm, mla_w_uq, mla_kv_norm, mla_w_uk, mla_w_uv, idx_w_q, idx_k_g, idx_k_b,
           w_mix_out, mem_norm, norm_xattn, xa_w_q, xa_w_kv, xa_w_o, norm_ffn, ffn_w_gate,
           ffn_w_up, ffn_w_down, moe_router, moe_w_gate, moe_w_up, moe_w_down, final_norm):
    B, T, D = x.shape
    M = B * T
    depth = w_in.shape[0]
    v_first = None
    col0 = (0, A_IN, A_IN + B_IN, A_IN + B_IN + C_IN)
    widths = (A_IN, B_IN, C_IN, G_IN)
    for l in range(depth):
        x2 = x.reshape(M, D)
        p_a, p_b, p_c, p_g = (
            _mm(x2, w_in[l][:, c:c + w], norm_g=norm_mix[l]) for c, w in zip(col0, widths))
        vres = None if l == 0 else (rw_v0[l - 1], rw_v1[l - 1], rw_v2[l - 1])
        y_a, v_first = _rwkv_branch(p_a.reshape(B, T, -1), v_first, rw_mu[l], rw_w0[l], rw_w2[l],
                                    rw_a0[l], rw_a2[l], rw_k_k[l], rw_k_a[l], rw_r_k[l],
                                    rw_gn_g[l], rw_gn_b[l], vres)
        y_b = _gdn_branch(p_b.reshape(B, T, -1), gdn_conv[l], gdn_A_log[l], gdn_dt_bias[l],
                          gdn_norm_g[l])
        y_c = _dsa_branch(p_c.reshape(B, T, -1), mla_q_norm[l], mla_w_uq[l], mla_kv_norm[l],
                          mla_w_uk[l], mla_w_uv[l], idx_w_q[l], idx_k_g[l], idx_k_b[l])
        x = _mix_layer(x2, y_a.reshape(M, D), y_b.reshape(M, D), y_c.reshape(M, D), p_g,
                       w_mix_out[l]).reshape(B, T, D)
        x = _xattn_layer(x, mem, mem_norm, norm_xattn[l], xa_w_q[l], xa_w_kv[l], xa_w_o[l])
        i = l // 2
        if l % 2 == 0:
            x = _ffn_layer(x, norm_ffn[l], ffn_w_gate[i], ffn_w_up[i], ffn_w_down[i])
        else:
            x = _moe_layer(x, norm_ffn[l], moe_router[i], moe_w_gate[i], moe_w_up[i], moe_w_down[i])
    return _final_norm(x.reshape(M, D), final_norm).reshape(B, T, D)
```

```python
import functools
import math

import jax
import jax.numpy as jnp
from jax import lax
from jax.experimental import pallas as pl
from jax.experimental.pallas import tpu as pltpu

F32 = jnp.float32
BF16 = jnp.bfloat16
HIGHEST = lax.Precision.HIGHEST

D_MODEL = 1024
EPS = 1e-6
LANES = 128
VMEM_LIMIT = 48 * 1024 * 1024

RW_HEAD = 64
RW_HEADS = D_MODEL // RW_HEAD
RW_DECAY_LORA = 64
RW_AAA_LORA = 64
RW_MV_LORA = 32
RW_GN_EPS = 64e-5
RW_CHUNK = 64

GDN_HEAD = 128
GDN_HEADS = D_MODEL // GDN_HEAD
GDN_CONV = 4
GDN_CHUNK = 64

MLA_HEADS = 8
MLA_HEAD = D_MODEL // MLA_HEADS
MLA_Q_RANK = 256
MLA_KV_RANK = 256
IDX_HEADS = 8
IDX_HEAD = 64
TOPK_MAX = 256

XA_HEADS = 4
XA_HEAD = D_MODEL // XA_HEADS

N_EXPERTS = 8
TOP_K = 2

A_SPLITS = (D_MODEL, D_MODEL, D_MODEL, RW_DECAY_LORA, RW_AAA_LORA)
B_SPLITS = (3 * D_MODEL, GDN_HEADS, GDN_HEADS)
C_SPLITS = (MLA_Q_RANK, MLA_KV_RANK, IDX_HEAD, IDX_HEADS)
A_IN = sum(A_SPLITS)
B_IN = sum(B_SPLITS)
C_IN = sum(C_SPLITS)
G_IN = 3 * D_MODEL

NEG_BIG = -1e30


def _cparams(*sem):
    return pltpu.CompilerParams(dimension_semantics=sem, vmem_limit_bytes=VMEM_LIMIT)


def _round_up(n, m):
    return (n + m - 1) // m * m


def _pick_tile(n, pref):
    t = min(n, pref)
    while n % t:
        t -= 8
    return t


def _dot(a, b):
    return jnp.dot(a.astype(BF16), b.astype(BF16), preferred_element_type=F32)


def _dot_nt(a, b):
    return lax.dot_general(a.astype(BF16), b.astype(BF16), (((1,), (1,)), ((), ())),
                           preferred_element_type=F32)


def _dot_tn(a, b):
    return lax.dot_general(a.astype(BF16), b.astype(BF16), (((0,), (0,)), ((), ())),
                           preferred_element_type=F32)


def _dot_hi(a, b):
    return jnp.dot(a, b, preferred_element_type=F32, precision=HIGHEST)


def _split_bf16(a):
    hi = a.astype(BF16)
    return hi, (a - hi.astype(F32)).astype(BF16)


def _dot_split(ah, al, bh, bl):
    n = ah.shape[0]
    top = jnp.dot(jnp.concatenate([ah, al], axis=0), bh, preferred_element_type=F32)
    return top[:n] + top[n:] + jnp.dot(ah, bl, preferred_element_type=F32)


def _nilpotent_inverse(ms, eye, size):
    splits = [_split_bf16(m) for m in ms]
    xs = [eye + m for m in ms]
    ps = [_dot_split(h, l, h, l) for h, l in splits]
    span = 2
    while span < size:
        splits = [_split_bf16(p) for p in ps]
        xsplits = [_split_bf16(x) for x in xs]
        xs = [x + _dot_split(xh, xl, h, l) for x, (xh, xl), (h, l) in zip(xs, xsplits, splits)]
        span *= 2
        if span < size:
            ps = [_dot_split(h, l, h, l) for h, l in splits]
    return xs


def _cumsum_rows(tril_bf16, x):
    hi = x.astype(BF16)
    r1 = x - hi.astype(F32)
    mid = r1.astype(BF16)
    lo = (r1 - mid.astype(F32)).astype(BF16)
    n = x.shape[1]
    parts = jnp.dot(tril_bf16, jnp.concatenate([hi, mid, lo], axis=1), preferred_element_type=F32)
    return parts[:, :n] + (parts[:, n:2 * n] + parts[:, 2 * n:])


def _mm_kernel(*refs, has_norm, has_res):
    it = iter(refs)
    x_ref = next(it)
    w_ref = next(it)
    g_ref = next(it) if has_norm else None
    r_ref = next(it) if has_res else None
    o_ref = next(it)
    xn_ref = next(it)

    @pl.when(pl.program_id(1) == 0)
    def _():
        x = x_ref[...].astype(F32)
        if has_norm:
            x = x * lax.rsqrt(jnp.mean(x * x, axis=-1, keepdims=True) + EPS) * g_ref[...]
        xn_ref[...] = x.astype(BF16)

    acc = jnp.dot(xn_ref[...], w_ref[...], preferred_element_type=F32)
    if has_res:
        acc = acc + r_ref[...]
    o_ref[...] = acc.astype(o_ref.dtype)


def _mm(x, w, norm_g=None, residual=None, tm=1024, tn=1024, keep_pad=False):
    M, K = x.shape
    N = w.shape[1]
    Np = _round_up(N, LANES)
    wb = w.astype(BF16)
    if Np != N:
        wb = jnp.pad(wb, ((0, 0), (0, Np - N)))
    tm = _pick_tile(M, tm)
    tn = min(Np, tn)
    while Np % tn:
        tn -= LANES
    args = [x, wb]
    in_specs = [pl.BlockSpec((tm, K), lambda i, j: (i, 0)),
                pl.BlockSpec((K, tn), lambda i, j: (0, j))]
    if norm_g is not None:
        args.append(norm_g.reshape(1, K).astype(F32))
        in_specs.append(pl.BlockSpec((1, K), lambda i, j: (0, 0)))
    if residual is not None:
        assert Np == N
        args.append(residual)
        in_specs.append(pl.BlockSpec((tm, tn), lambda i, j: (i, j)))
    out = pl.pallas_call(
        functools.partial(_mm_kernel, has_norm=norm_g is not None, has_res=residual is not None),
        grid=(M // tm, Np // tn),
        in_specs=in_specs,
        out_specs=pl.BlockSpec((tm, tn), lambda i, j: (i, j)),
        out_shape=jax.ShapeDtypeStruct((M, Np), F32),
        scratch_shapes=[pltpu.VMEM((tm, K), BF16)],
        compiler_params=_cparams("parallel", "arbitrary"),
        name="mm",
    )(*args)
    return out if (Np == N or keep_pad) else out[:, :N]


def _rwkv_kernel(r_ref, lw_ref, k_ref, v_ref, kk_ref, a_ref, rk_ref, gg_ref, gb_ref, o_ref, s_ref,
                 *, chunk, heads_per_block):
    C, HB, N = chunk, heads_per_block, RW_HEAD

    @pl.when(pl.program_id(2) == 0)
    def _():
        s_ref[...] = jnp.zeros_like(s_ref)

    NP = HB // 2
    W = 2 * N
    n_chunks = r_ref.shape[1] // C
    row = lax.broadcasted_iota(jnp.int32, (C, C), 0)
    col = lax.broadcasted_iota(jnp.int32, (C, C), 1)
    incl = row >= col
    strict = row > col
    strict4 = jnp.concatenate([strict, strict, incl, incl], axis=0)
    tril_b = incl.astype(BF16)
    eye = (row == col).astype(F32)
    lane = lax.broadcasted_iota(jnp.int32, (1, W), 1)
    in_a = lane < N
    mask_a = in_a.astype(F32)
    mask_b = 1.0 - mask_a
    srow = lax.broadcasted_iota(jnp.int32, (W, W), 0)
    scol = lax.broadcasted_iota(jnp.int32, (W, W), 1)
    block_diag = ((srow < N) == (scol < N)).astype(F32)

    def halves(stacked):
        return jnp.where(in_a, stacked[:C], stacked[C:])

    def head_sum(x):
        sa = jnp.sum(x * mask_a, axis=-1, keepdims=True)
        sb = jnp.sum(x * mask_b, axis=-1, keepdims=True)
        return jnp.where(in_a, sa, sb)

    incl2 = jnp.concatenate([incl, incl], axis=0)

    def chunk_body(c, carry):
        sl = pl.ds(pl.multiple_of(c * C, C), C)
        P = range(NP)
        cols = [slice(p * W, (p + 1) * W) for p in P]
        lw = [lw_ref[0, sl, cs] for cs in cols]
        cum = [_cumsum_rows(tril_b, x) for x in lw]
        g = [jnp.exp(x) for x in cum]
        g_prev = [jnp.exp(x - y) for x, y in zip(cum, lw)]
        g_inv = [jnp.exp(-x) for x in cum]
        g_end = [x[C - 1:C, :] for x in g]
        kk = [kk_ref[0, sl, cs] for cs in cols]
        a_bar = [-x * y for x, y in zip(kk, g_prev)]
        b_til = [x * a_ref[0, sl, cs] * y for x, cs, y in zip(kk, cols, g_inv)]
        k_til = [k_ref[0, sl, cs] * y for cs, y in zip(cols, g_inv)]
        r_bar = [r_ref[0, sl, cs] * y for cs, y in zip(cols, g)]
        lhs = [jnp.concatenate([x, y], axis=0) for x, y in zip(a_bar, r_bar)]
        rhs = [jnp.concatenate([x, y], axis=0) for x, y in zip(b_til, k_til)]
        pair = [_dot_nt(jnp.concatenate([x * mask_a, x * mask_b], axis=0), y)
                for x, y in zip(lhs, rhs)]
        l_ab = [jnp.where(strict, pr[base:base + C, :C], 0.0) for pr in pair for base in (0, 2 * C)]
        t_inv = _nilpotent_inverse(l_ab, eye, C)
        on_v = [jnp.where(strict4, jnp.concatenate(
            [pr[:C, C:], pr[2 * C:3 * C, C:], pr[C:2 * C, C:], pr[3 * C:, C:]], axis=0), 0.0)
            for pr in pair]
        v = [v_ref[0, sl, cs] for cs in cols]
        from_v = [_dot(x, y) for x, y in zip(on_v, v)]
        s = [s_ref[p] for p in P]
        from_state = [_dot_nt(x, y) for x, y in zip(lhs, s)]
        u = [halves(_dot(jnp.concatenate([t_inv[2 * p], t_inv[2 * p + 1]], axis=0),
                         from_state[p][:C] + halves(from_v[p][:2 * C]))) for p in P]
        a_rb = [jnp.where(incl2, jnp.concatenate([pr[C:2 * C, :C], pr[3 * C:, :C]], axis=0), 0.0)
                for pr in pair]
        from_u = [_dot(x, y) for x, y in zip(a_rb, u)]
        upd = [_dot_tn(jnp.concatenate([v[p], u[p]], axis=0),
                       jnp.concatenate([k_til[p] * g_end[p], b_til[p] * g_end[p]], axis=0))
               for p in P]
        for p in P:
            s_ref[p] = s[p] * g_end[p] + upd[p] * block_diag
        for p in P:
            cs = cols[p]
            o = from_state[p][C:] + halves(from_v[p][2 * C:]) + halves(from_u[p])
            mean = head_sum(o) * (1.0 / N)
            cen = o - mean
            var = head_sum(cen * cen) * (1.0 / N)
            o = cen * lax.rsqrt(var + RW_GN_EPS) * gg_ref[:, cs] + gb_ref[:, cs]
            bonus = head_sum(r_ref[0, sl, cs] * k_ref[0, sl, cs] * rk_ref[:, cs]) * v[p]
            o_ref[0, sl, cs] = o + bonus
        return carry

    lax.fori_loop(0, n_chunks, chunk_body, 0)


def _rwkv_recurrence(r, lw, k, v, kk, a, r_k, gn_g, gn_b, tb=256, heads_per_block=16):
    B, T, D = r.shape
    HB = heads_per_block
    W = HB * RW_HEAD
    tb = _pick_tile(T, tb)
    seq = pl.BlockSpec((1, tb, W), lambda b, h, t: (b, t, h))
    vec = pl.BlockSpec((1, W), lambda b, h, t: (0, h))
    return pl.pallas_call(
        functools.partial(_rwkv_kernel, chunk=min(RW_CHUNK, tb), heads_per_block=HB),
        grid=(B, D // W, T // tb),
        in_specs=[seq] * 6 + [vec] * 3,
        out_specs=seq,
        out_shape=jax.ShapeDtypeStruct((B, T, D), F32),
        scratch_shapes=[pltpu.VMEM((HB // 2, 2 * RW_HEAD, 2 * RW_HEAD), F32)],
        compiler_params=_cparams("parallel", "parallel", "arbitrary"),
        name="rwkv7",
    )(r, lw, k, v, kk, a, r_k.reshape(1, D), gn_g.reshape(1, D), gn_b.reshape(1, D))


def _gdn_kernel(q_ref, k_ref, v_ref, beta_ref, gcol_ref, grow_ref, ng_ref, o_ref, s_ref, *, chunk):
    C = chunk

    @pl.when(pl.program_id(1) == 0)
    def _():
        s_ref[...] = jnp.zeros_like(s_ref)

    n_chunks = q_ref.shape[1] // C
    row = lax.broadcasted_iota(jnp.int32, (C, C), 0)
    col = lax.broadcasted_iota(jnp.int32, (C, C), 1)
    incl = row >= col
    strict = row > col
    eye = (row == col).astype(F32)
    ng = ng_ref[...]

    def chunk_body(c, carry):
        sl = pl.ds(pl.multiple_of(c * C, C), C)
        beta_all = beta_ref[0, sl, :]
        gcol_all = gcol_ref[0, sl, :]
        grow_all = grow_ref[0, 0, c]
        H = range(GDN_HEADS)
        cols = [slice(h * GDN_HEAD, (h + 1) * GDN_HEAD) for h in H]
        beta = [beta_all[:, h:h + 1] for h in H]
        gcol = [gcol_all[:, h:h + 1] for h in H]
        k = [k_ref[0, sl, cs] for cs in cols]
        kb = [x * y for x, y in zip(k, beta)]
        decay = [jnp.exp(jnp.where(incl, gcol[h] - grow_all[h:h + 1, :], -jnp.inf)) for h in H]
        pair = [_dot_nt(jnp.concatenate([kb[h], q_ref[0, sl, cols[h]]], axis=0), k[h])
                for h in H]
        t_inv = _nilpotent_inverse(
            [-jnp.where(strict, pair[h][:C] * decay[h], 0.0) for h in H], eye, C)
        eg = [jnp.exp(x) for x in gcol]
        sol = [_dot(t_inv[h], jnp.concatenate([v_ref[0, sl, cols[h]] * beta[h], kb[h] * eg[h]],
                                              axis=-1)) for h in H]
        s = [s_ref[h] for h in H]
        from_state = [_dot(jnp.concatenate([sol[h][:, GDN_HEAD:], q_ref[0, sl, cols[h]] * eg[h]],
                                           axis=0), s[h]) for h in H]
        v_new = [sol[h][:, :GDN_HEAD] - from_state[h][:C] for h in H]
        intra = [_dot(pair[h][C:] * decay[h], v_new[h]) for h in H]
        g_end = [x[C - 1:C, :] for x in gcol]
        upd = [_dot_tn(k[h] * jnp.exp(g_end[h] - gcol[h]), v_new[h]) for h in H]
        for h in H:
            s_ref[h] = s[h] * jnp.exp(g_end[h]) + upd[h]
        for h in H:
            o = from_state[h][C:] + intra[h]
            o = o * lax.rsqrt(jnp.mean(o * o, axis=-1, keepdims=True) + EPS) * ng
            o_ref[0, sl, cols[h]] = o
        return carry

    lax.fori_loop(0, n_chunks, chunk_body, 0)


def _gdn_recurrence(q, k, v, beta, gcum, norm_g, tb=256):
    B, T, D = q.shape
    H, Dh = GDN_HEADS, GDN_HEAD
    tb = _pick_tile(T, tb)
    C = min(GDN_CHUNK, tb)
    grow = jnp.transpose(gcum.reshape(B, T // tb, tb // C, C, H), (0, 1, 2, 4, 3))
    seq = pl.BlockSpec((1, tb, D), lambda b, t: (b, t, 0))
    colspec = pl.BlockSpec((1, tb, H), lambda b, t: (b, t, 0))
    rowspec = pl.BlockSpec((1, 1, tb // C, H, C), lambda b, t: (b, t, 0, 0, 0))
    return pl.pallas_call(
        functools.partial(_gdn_kernel, chunk=C),
        grid=(B, T // tb),
        in_specs=[seq, seq, seq, colspec, colspec, rowspec,
                  pl.BlockSpec((1, Dh), lambda b, t: (0, 0))],
        out_specs=seq,
        out_shape=jax.ShapeDtypeStruct((B, T, D), F32),
        scratch_shapes=[pltpu.VMEM((H, Dh, Dh), F32)],
        compiler_params=_cparams("parallel", "arbitrary"),
        name="gdn",
    )(q, k, v, beta, gcum, grow, norm_g.reshape(1, Dh))


def _split(t, sizes):
    offs = []
    acc = 0
    for s in sizes[:-1]:
        acc += s
        offs.append(acc)
    return jnp.split(t, offs, axis=-1)


def _prev_rows(p, tm, rows):
    B, T, W = p.shape
    tail = p.reshape(B, T // tm, tm, W)[:, :-1, tm - rows:, :]
    return jnp.pad(tail, ((0, 0), (1, 0), (0, 0), (0, 0)))


def _group_sumsq(x, ones_bd):
    hi, lo = _split_bf16(x * x)
    return (jnp.dot(hi, ones_bd, preferred_element_type=F32)
            + jnp.dot(lo, ones_bd, preferred_element_type=F32))


def _shift_rows(x, halo, s):
    rolled = pltpu.roll(x, s, 0)
    row = lax.broadcasted_iota(jnp.int32, (8, x.shape[1]), 0)
    top = jnp.where(row < s, pltpu.roll(halo, s, 0), rolled[:8])
    return jnp.concatenate([top, rolled[8:]], axis=0)


def _rwkv_prep_kernel(*refs, has_vres):
    D = D_MODEL
    if has_vres:
        (pa_ref, prev_ref, mu_ref, w0_ref, w2_ref, a0_ref, a2_ref, kk_ref, ka_ref,
         vf_ref, v0_ref, v1_ref, v2_ref, r_o, lw_o, k_o, v_o, kkn_o, a_o) = refs
    else:
        (pa_ref, prev_ref, mu_ref, w0_ref, w2_ref, a0_ref, a2_ref, kk_ref, ka_ref,
         r_o, lw_o, k_o, v_o, kkn_o, a_o) = refs
    x = pa_ref[0]
    x = x + (_shift_rows(x, prev_ref[0, 0], 1) - x) * mu_ref[...]
    r = x[:, :D]
    k = x[:, D:2 * D]
    v = x[:, 2 * D:3 * D]
    xw = x[:, 3 * D:3 * D + RW_DECAY_LORA]
    xa = x[:, 3 * D + RW_DECAY_LORA:]
    lw_o[0] = -math.exp(-0.5) * jax.nn.sigmoid(w0_ref[...] + _dot(jnp.tanh(xw), w2_ref[...]))
    a = jax.nn.sigmoid(a0_ref[...] + _dot(xa, a2_ref[...]))
    if has_vres:
        gate = jax.nn.sigmoid(v0_ref[...] + _dot(_dot(v, v1_ref[...]), v2_ref[...]))
        v = v + (vf_ref[0] - v) * gate
    r_o[0] = r
    v_o[0] = v
    a_o[0] = a
    k_o[0] = k * (1.0 + (a - 1.0) * ka_ref[...])
    lane_r = lax.broadcasted_iota(jnp.int32, (LANES, LANES), 0) // RW_HEAD
    lane_c = lax.broadcasted_iota(jnp.int32, (LANES, LANES), 1) // RW_HEAD
    ones_bd = (lane_r == lane_c).astype(BF16)
    for j in range(D // LANES):
        cs = slice(j * LANES, (j + 1) * LANES)
        kx = k[:, cs] * kk_ref[:, cs]
        kkn_o[0, :, cs] = kx * lax.rsqrt(_group_sumsq(kx, ones_bd) + EPS)


def _rwkv_branch(p_a, v_first, mu, w0, w2, a0, a2, k_k, k_a, r_k, gn_g, gn_b, vres, tm=256):
    B, T, _ = p_a.shape
    D = D_MODEL
    tm = _pick_tile(T, tm)
    row = lambda n: pl.BlockSpec((1, n), lambda b, i: (0, 0))
    mat = lambda a, b_: pl.BlockSpec((a, b_), lambda b, i: (0, 0))
    seq = pl.BlockSpec((1, tm, D), lambda b, i: (b, i, 0))
    args = [p_a, _prev_rows(p_a, tm, 8), mu.reshape(1, A_IN), w0.reshape(1, D), w2.astype(BF16),
            a0.reshape(1, D), a2.astype(BF16), k_k.reshape(1, D), k_a.reshape(1, D)]
    in_specs = [pl.BlockSpec((1, tm, A_IN), lambda b, i: (b, i, 0)),
                pl.BlockSpec((1, 1, 8, A_IN), lambda b, i: (b, i, 0, 0)),
                row(A_IN), row(D), mat(RW_DECAY_LORA, D), row(D), mat(RW_AAA_LORA, D), row(D), row(D)]
    if vres is not None:
        v0, v1, v2 = vres
        pad = LANES - RW_MV_LORA
        args += [v_first, v0.reshape(1, D), jnp.pad(v1, ((0, 0), (0, pad))).astype(BF16),
                 jnp.pad(v2, ((0, pad), (0, 0))).astype(BF16)]
        in_specs += [seq, row(D), mat(D, LANES), mat(LANES, D)]
    r, lw, k, v, kk, a = pl.pallas_call(
        functools.partial(_rwkv_prep_kernel, has_vres=vres is not None),
        grid=(B, T // tm),
        in_specs=in_specs,
        out_specs=[seq] * 6,
        out_shape=[jax.ShapeDtypeStruct((B, T, D), F32)] * 6,
        compiler_params=_cparams("parallel", "parallel"),
        name="rwkv_prep",
    )(*args)
    if vres is None:
        v_first = v
    return _rwkv_recurrence(r, lw, k, v, kk, a, r_k, gn_g, gn_b), v_first


def _gdn_prep_kernel(pb_ref, halo_ref, cw_ref, q_o, k_o, v_o):
    D = D_MODEL
    x = pb_ref[0]
    halo = halo_ref[0, 0]
    acc = x * cw_ref[GDN_CONV - 1:GDN_CONV, :]
    for s in range(1, GDN_CONV):
        acc = acc + _shift_rows(x, halo, s) * cw_ref[GDN_CONV - 1 - s:GDN_CONV - s, :]
    y = acc * jax.nn.sigmoid(acc)
    v_o[0] = y[:, 2 * D:]
    ones = jnp.ones((LANES, LANES), BF16)
    for j in range(D // GDN_HEAD):
        cs = slice(j * GDN_HEAD, (j + 1) * GDN_HEAD)
        q = y[:, cs]
        k = y[:, D + j * GDN_HEAD:D + (j + 1) * GDN_HEAD]
        q_o[0, :, cs] = q * (lax.rsqrt(_group_sumsq(q, ones) + EPS) * GDN_HEAD ** -0.5)
        k_o[0, :, cs] = k * lax.rsqrt(_group_sumsq(k, ones) + EPS)


def _gdn_branch(p_b, conv_w, A_log, dt_bias, norm_g, tm=256):
    B, T, _ = p_b.shape
    D, H, C = D_MODEL, GDN_HEADS, GDN_CHUNK
    tm = _pick_tile(T, tm)
    seq = pl.BlockSpec((1, tm, D), lambda b, i: (b, i, 0))
    q, k, v = pl.pallas_call(
        _gdn_prep_kernel,
        grid=(B, T // tm),
        in_specs=[pl.BlockSpec((1, tm, 3 * D), lambda b, i: (b, i, 0)),
                  pl.BlockSpec((1, 1, 8, 3 * D), lambda b, i: (b, i, 0, 0)),
                  pl.BlockSpec((GDN_CONV, 3 * D), lambda b, i: (0, 0))],
        out_specs=[seq] * 3,
        out_shape=[jax.ShapeDtypeStruct((B, T, D), F32)] * 3,
        compiler_params=_cparams("parallel", "parallel"),
        name="gdn_prep",
    )(p_b, _prev_rows(p_b[..., :3 * D], tm, 8), conv_w)
    a_in = p_b[..., 3 * D:3 * D + H]
    b_in = p_b[..., 3 * D + H:3 * D + 2 * H]
    beta = jax.nn.sigmoid(b_in)
    g = -jnp.exp(A_log) * jax.nn.softplus(a_in + dt_bias)
    Cc = min(C, T)
    gcum = jnp.cumsum(g.reshape(B, T // Cc, Cc, H), axis=2).reshape(B, T, H)
    return _gdn_recurrence(q, k, v, beta, gcum, norm_g)


def _dsa_kernel(cq_ref, wi_ref, kit_ref, ckv_ref, ckvt_ref, wqi_ref, wuq_ref, wuk_ref, wuv_ref, o_ref,
                key_scr, m_scr, l_scr, acc_scr, qlat_scr, *, n_sel, tq, tk, pos_bits):
    H = IDX_HEADS
    q0 = pl.program_id(1) * tq
    nk = (q0 + tq + tk - 1) // tk
    fold = tk // LANES
    int_min = jnp.int32(-2 ** 31)

    cq = cq_ref[0].astype(BF16)
    q_idx = _dot(cq, wqi_ref[...])
    qi_rows = jnp.concatenate([q_idx[:, h * IDX_HEAD:(h + 1) * IDX_HEAD] for h in range(H)],
                              axis=0).astype(BF16)
    wi = wi_ref[0]
    wi_rows = jnp.concatenate([wi[:, h:h + 1] for h in range(H)], axis=0)
    q_pos = q0 + lax.broadcasted_iota(jnp.int32, (tq, tk), 0)
    lane_pos = lax.broadcasted_iota(jnp.int32, (tq, tk), 1)

    def score_tile(kt, carry):
        ki_t = kit_ref[0, kt]
        z = jnp.maximum(jnp.dot(qi_rows, ki_t, preferred_element_type=F32), 0.0) * wi_rows
        s = jnp.sum(z.reshape(H, tq, tk), axis=0)
        s = jnp.where(s == 0.0, 0.0, s)
        s = jnp.where(kt * tk + lane_pos <= q_pos, s, -jnp.inf)
        bits = pltpu.bitcast(s, jnp.int32)
        key_scr[kt] = bits ^ ((bits >> 31) & jnp.int32(0x7FFFFFFF))
        return carry

    lax.fori_loop(0, nk, score_tile, 0)

    def count(pred):
        def body(kt, acc):
            hit = jnp.where(pred(key_scr[kt], kt * tk + lane_pos), 1.0, 0.0)
            part = hit[:, :LANES]
            for f in range(1, fold):
                part = part + hit[:, f * LANES:(f + 1) * LANES]
            return acc + part
        acc = lax.fori_loop(0, nk, body, jnp.zeros((tq, LANES), F32))
        return jnp.sum(acc, axis=-1, keepdims=True)

    want = jnp.float32(n_sel)
    thr = jnp.where(count(lambda key, pos: key >= 0) >= want, jnp.int32(0), int_min)

    def thr_bit(i, thr):
        cand = thr | jnp.left_shift(jnp.int32(1), 30 - i)
        return jnp.where(count(lambda key, pos: key >= cand) >= want, cand, thr)

    thr = lax.fori_loop(0, 31, thr_bit, thr)

    row_pos = q0 + lax.broadcasted_iota(jnp.int32, (tq, 1), 0)
    tied = jnp.logical_and(count(lambda key, pos: key >= thr) > want, row_pos >= n_sel)
    any_tied = jnp.max(jnp.where(tied, 1.0, 0.0)) > 0.0

    def index_cut():
        need = want - count(lambda key, pos: key > thr)

        def pos_bit(i, last):
            cand = last + jnp.left_shift(jnp.int32(1), pos_bits - 1 - i)
            below = count(lambda key, pos: jnp.logical_and(key == thr, pos < cand))
            return jnp.where(below < need, cand, last)

        return lax.fori_loop(0, pos_bits, pos_bit, jnp.zeros((tq, 1), jnp.int32))

    last = lax.cond(any_tied, index_cut, lambda: jnp.full((tq, 1), 2 ** pos_bits, jnp.int32))

    q = _dot(cq, wuq_ref[...])
    q_lat = jnp.concatenate(
        [_dot_nt(q[:, h * MLA_HEAD:(h + 1) * MLA_HEAD], wuk_ref[h]) for h in range(MLA_HEADS)],
        axis=0) * MLA_HEAD ** -0.5
    qlat_scr[...] = q_lat.astype(BF16)
    m_scr[...] = jnp.full(m_scr.shape, NEG_BIG, F32)
    l_scr[...] = jnp.zeros(l_scr.shape, F32)
    acc_scr[...] = jnp.zeros(acc_scr.shape, F32)
    HG = 2
    groups = [slice(g * HG, (g + 1) * HG) for g in range(MLA_HEADS // HG)]

    def attend_tile(kt, carry):
        ckv = ckv_ref[0, pl.ds(pl.multiple_of(kt * tk, tk), tk), :]
        ckv_t = ckvt_ref[0, kt]
        key = key_scr[kt]
        pos = kt * tk + lane_pos
        sel = jnp.logical_or(key > thr, jnp.logical_and(key == thr, pos <= last))
        sel = jnp.logical_and(sel, pos <= q_pos)
        bias = jnp.where(sel, 0.0, NEG_BIG)
        logits = [jnp.dot(qlat_scr[pl.ds(g.start * tq, HG * tq), :], ckv_t,
                          preferred_element_type=F32) for g in groups]
        for g, lg in zip(groups, logits):
            lg = lg.reshape(HG, tq, tk) + bias[None]
            m_old = m_scr[g]
            m_new = jnp.maximum(m_old, jnp.max(lg, axis=-1, keepdims=True))
            p = jnp.exp(lg - m_new)
            alpha = jnp.exp(m_old - m_new)
            l_scr[g] = alpha * l_scr[g] + jnp.sum(p, axis=-1, keepdims=True)
            pv = _dot(p.reshape(HG * tq, tk), ckv).reshape(HG, tq, MLA_KV_RANK)
            acc_scr[g] = alpha * acc_scr[g] + pv
            m_scr[g] = m_new
        return carry

    lax.fori_loop(0, nk, attend_tile, 0)
    o_lat = acc_scr[...] / l_scr[...]
    o_ref[0] = jnp.concatenate([_dot(o_lat[h], wuv_ref[h]) for h in range(MLA_HEADS)], axis=-1)


def _dsa_attention(c_q, w_i, k_i, c_kv, w_qi, w_uq, w_uk, w_uv, tq=128, tk=512):
    B, T, _ = c_q.shape
    tq = _pick_tile(T, tq)
    tk = _pick_tile(T, tk)
    n_sel = min(TOPK_MAX, T // 4)
    assert tk % LANES == 0 and tk >= n_sel and tk % tq == 0
    H = MLA_HEADS
    full = lambda *shape: pl.BlockSpec(shape, lambda b, i: (0,) * len(shape))
    k_it = jnp.transpose(k_i.reshape(B, T // tk, tk, IDX_HEAD), (0, 1, 3, 2))
    c_kvt = jnp.transpose(c_kv.reshape(B, T // tk, tk, MLA_KV_RANK), (0, 1, 3, 2))
    return pl.pallas_call(
        functools.partial(_dsa_kernel, n_sel=n_sel, tq=tq, tk=tk, pos_bits=max(1, (T - 1).bit_length())),
        grid=(B, T // tq),
        in_specs=[pl.BlockSpec((1, tq, MLA_Q_RANK), lambda b, i: (b, i, 0)),
                  pl.BlockSpec((1, tq, IDX_HEADS), lambda b, i: (b, i, 0)),
                  pl.BlockSpec((1, T // tk, IDX_HEAD, tk), lambda b, i: (b, 0, 0, 0)),
                  pl.BlockSpec((1, T, MLA_KV_RANK), lambda b, i: (b, 0, 0)),
                  pl.BlockSpec((1, T // tk, MLA_KV_RANK, tk), lambda b, i: (b, 0, 0, 0)),
                  full(MLA_Q_RANK, IDX_HEADS * IDX_HEAD),
                  full(MLA_Q_RANK, H * MLA_HEAD),
                  full(H, MLA_KV_RANK, MLA_HEAD),
                  full(H, MLA_KV_RANK, MLA_HEAD)],
        out_specs=pl.BlockSpec((1, tq, H * MLA_HEAD), lambda b, i: (b, i, 0)),
        out_shape=jax.ShapeDtypeStruct((B, T, H * MLA_HEAD), F32),
        scratch_shapes=[pltpu.VMEM((T // tk, tq, tk), jnp.int32),
                        pltpu.VMEM((H, tq, 1), F32),
                        pltpu.VMEM((H, tq, 1), F32),
                        pltpu.VMEM((H, tq, MLA_KV_RANK), F32),
                        pltpu.VMEM((H * tq, MLA_KV_RANK), BF16)],
        compiler_params=_cparams("parallel", "arbitrary"),
        name="dsa",
    )(c_q, w_i, k_it, c_kv, c_kvt, w_qi.astype(BF16), w_uq.astype(BF16),
      jnp.transpose(w_uk, (1, 0, 2)).astype(BF16), jnp.transpose(w_uv, (1, 0, 2)).astype(BF16))


def _rms(x, g):
    return x * lax.rsqrt(jnp.mean(x * x, axis=-1, keepdims=True) + EPS) * g


def _dsa_prep_kernel(pc_ref, qn_ref, kvn_ref, kig_ref, kib_ref, cq_o, ckv_o, ki_o, wi_o):
    x = pc_ref[...]
    cq_o[...] = _rms(x[:, :MLA_Q_RANK], qn_ref[...]).astype(BF16)
    ckv_o[...] = _rms(x[:, MLA_Q_RANK:MLA_Q_RANK + MLA_KV_RANK], kvn_ref[...]).astype(BF16)
    off = MLA_Q_RANK + MLA_KV_RANK
    ki = x[:, off:off + IDX_HEAD]
    mu = jnp.mean(ki, axis=-1, keepdims=True)
    var = jnp.mean(jnp.square(ki - mu), axis=-1, keepdims=True)
    ki_o[...] = ((ki - mu) * lax.rsqrt(var + EPS) * kig_ref[...] + kib_ref[...]).astype(BF16)
    wi_o[...] = x[:, off + IDX_HEAD:off + IDX_HEAD + IDX_HEADS] * (IDX_HEADS ** -0.5 * IDX_HEAD ** -0.5)


def _dsa_branch(p_c, q_norm, w_uq, kv_norm, w_uk, w_uv, w_qi, ki_g, ki_b, tm=512):
    B, T, W = p_c.shape
    M = B * T
    tm = _pick_tile(M, tm)
    row = lambda n: pl.BlockSpec((1, n), lambda i: (0, 0))
    out = lambda n: pl.BlockSpec((tm, n), lambda i: (i, 0))
    c_q, c_kv, k_i, w_i = pl.pallas_call(
        _dsa_prep_kernel,
        grid=(M // tm,),
        in_specs=[pl.BlockSpec((tm, W), lambda i: (i, 0)), row(MLA_Q_RANK), row(MLA_KV_RANK),
                  row(IDX_HEAD), row(IDX_HEAD)],
        out_specs=[out(MLA_Q_RANK), out(MLA_KV_RANK), out(IDX_HEAD), out(IDX_HEADS)],
        out_shape=[jax.ShapeDtypeStruct((M, MLA_Q_RANK), BF16),
                   jax.ShapeDtypeStruct((M, MLA_KV_RANK), BF16),
                   jax.ShapeDtypeStruct((M, IDX_HEAD), BF16),
                   jax.ShapeDtypeStruct((M, IDX_HEADS), F32)],
        compiler_params=_cparams("parallel"),
        name="dsa_prep",
    )(p_c.reshape(M, W), q_norm.reshape(1, -1), kv_norm.reshape(1, -1), ki_g.reshape(1, -1),
      ki_b.reshape(1, -1))
    return _dsa_attention(c_q.reshape(B, T, -1), w_i.reshape(B, T, -1), k_i.reshape(B, T, -1),
                          c_kv.reshape(B, T, -1), w_qi, w_uq, w_uk, w_uv)


def _mix_kernel(ya_ref, yb_ref, yc_ref, ga_ref, gb_ref, gc_ref, w_ref, x_ref, o_ref):
    mix = (jax.nn.sigmoid(ga_ref[...]) * ya_ref[...] + jax.nn.sigmoid(gb_ref[...]) * yb_ref[...]
           + jax.nn.sigmoid(gc_ref[...]) * yc_ref[...])
    o_ref[...] = x_ref[...] + jnp.dot(mix.astype(BF16), w_ref[...], preferred_element_type=F32)


def _mix_layer(x, y_a, y_b, y_c, p_g, w_out, tm=512):
    M, D = x.shape
    tm = _pick_tile(M, tm)
    tile = pl.BlockSpec((tm, D), lambda i: (i, 0))
    gate = lambda c: pl.BlockSpec((tm, D), lambda i: (i, c))
    return pl.pallas_call(
        _mix_kernel,
        grid=(M // tm,),
        in_specs=[tile, tile, tile, gate(0), gate(1), gate(2),
                  pl.BlockSpec((D, D), lambda i: (0, 0)), tile],
        out_specs=tile,
        out_shape=jax.ShapeDtypeStruct((M, D), F32),
        compiler_params=_cparams("parallel"),
        name="mix",
    )(y_a, y_b, y_c, p_g, p_g, p_g, w_out.astype(BF16), x)


def _xattn_kernel(x_ref, g_ref, wq_ref, k_ref, v_ref, wo_ref, o_ref):
    x = x_ref[0]
    h = x * lax.rsqrt(jnp.mean(x * x, axis=-1, keepdims=True) + EPS) * g_ref[...]
    q = jnp.dot(h.astype(BF16), wq_ref[...], preferred_element_type=F32)
    k = k_ref[0]
    v = v_ref[0]
    outs = []
    for hd in range(XA_HEADS):
        cs = slice(hd * XA_HEAD, (hd + 1) * XA_HEAD)
        logits = _dot_nt(q[:, cs], k[:, cs]) * XA_HEAD ** -0.5
        p = jnp.exp(logits - jnp.max(logits, axis=-1, keepdims=True))
        outs.append(_dot(p, v[:, cs]) / jnp.sum(p, axis=-1, keepdims=True))
    o = jnp.concatenate(outs, axis=-1)
    o_ref[0] = x + jnp.dot(o.astype(BF16), wo_ref[...], preferred_element_type=F32)


def _xattn_layer(x, mem, mem_norm, norm_g, w_q, w_kv, w_o, tq=512):
    B, T, D = x.shape
    Mm = mem.shape[1]
    kv = _mm(mem.reshape(B * Mm, D), w_kv, norm_g=mem_norm).reshape(B, Mm, 2 * D)
    tq = _pick_tile(T, tq)
    tile = pl.BlockSpec((1, tq, D), lambda b, i: (b, i, 0))
    wspec = pl.BlockSpec((D, D), lambda b, i: (0, 0))
    return pl.pallas_call(
        _xattn_kernel,
        grid=(B, T // tq),
        in_specs=[tile, pl.BlockSpec((1, D), lambda b, i: (0, 0)), wspec,
                  pl.BlockSpec((1, Mm, D), lambda b, i: (b, 0, 0)),
                  pl.BlockSpec((1, Mm, D), lambda b, i: (b, 0, 1)), wspec],
        out_specs=tile,
        out_shape=jax.ShapeDtypeStruct((B, T, D), F32),
        compiler_params=_cparams("parallel", "parallel"),
        name="xattn",
    )(x, norm_g.reshape(1, D), w_q.astype(BF16), kv, kv, w_o.astype(BF16))


def _ffn_kernel(*refs, n_experts):
    if n_experts:
        x_ref, g_ref, gates_ref, wg_ref, wu_ref, wd_ref, o_ref, h_scr, acc_scr = refs
        e, j = pl.program_id(1), pl.program_id(2)
        first = jnp.logical_and(e == 0, j == 0)
        last = jnp.logical_and(e == n_experts - 1, j == pl.num_programs(2) - 1)
        wg, wu, wd = wg_ref[0], wu_ref[0], wd_ref[0]
    else:
        x_ref, g_ref, wg_ref, wu_ref, wd_ref, o_ref, h_scr, acc_scr = refs
        j = pl.program_id(1)
        first = j == 0
        last = j == pl.num_programs(1) - 1
        wg, wu, wd = wg_ref[...], wu_ref[...], wd_ref[...]

    @pl.when(first)
    def _():
        x = x_ref[...]
        h = x * lax.rsqrt(jnp.mean(x * x, axis=-1, keepdims=True) + EPS) * g_ref[...]
        h_scr[...] = h.astype(BF16)
        acc_scr[...] = jnp.zeros_like(acc_scr)

    h = h_scr[...]
    gate = jnp.dot(h, wg, preferred_element_type=F32)
    up = jnp.dot(h, wu, preferred_element_type=F32)
    act = gate * jax.nn.sigmoid(gate) * up
    if n_experts:
        lane = lax.broadcasted_iota(jnp.int32, gates_ref.shape, 1)
        act = act * jnp.sum(jnp.where(lane == e, gates_ref[...], 0.0), axis=-1, keepdims=True)
    acc_scr[...] += jnp.dot(act.astype(BF16), wd, preferred_element_type=F32)

    @pl.when(last)
    def _():
        o_ref[...] = x_ref[...] + acc_scr[...]


def _ffn_tiles(M, F, tm, tf):
    tm = _pick_tile(M, tm)
    tf = min(F, tf)
    while F % tf or tf % LANES:
        tf -= LANES
    return tm, tf


def _ffn_layer(x, norm_g, w_gate, w_up, w_down, tm=512, tf=1408):
    B, T, D = x.shape
    M = B * T
    F = w_gate.shape[-1]
    tm, tf = _ffn_tiles(M, F, tm, tf)
    tile = pl.BlockSpec((tm, D), lambda i, j: (i, 0))
    out = pl.pallas_call(
        functools.partial(_ffn_kernel, n_experts=0),
        grid=(M // tm, F // tf),
        in_specs=[tile, pl.BlockSpec((1, D), lambda i, j: (0, 0)),
                  pl.BlockSpec((D, tf), lambda i, j: (0, j)),
                  pl.BlockSpec((D, tf), lambda i, j: (0, j)),
                  pl.BlockSpec((tf, D), lambda i, j: (j, 0))],
        out_specs=tile,
        out_shape=jax.ShapeDtypeStruct((M, D), F32),
        scratch_shapes=[pltpu.VMEM((tm, D), BF16), pltpu.VMEM((tm, D), F32)],
        compiler_params=_cparams("parallel", "arbitrary"),
        name="ffn",
    )(x.reshape(M, D), norm_g.reshape(1, D), w_gate.astype(BF16), w_up.astype(BF16), w_down.astype(BF16))
    return out.reshape(B, T, D)


def _router_kernel(x_ref, g_ref, w_ref, o_ref):
    x = x_ref[...]
    h = x * lax.rsqrt(jnp.mean(x * x, axis=-1, keepdims=True) + EPS) * g_ref[...]
    logits = _dot_hi(h, w_ref[...])
    lane = lax.broadcasted_iota(jnp.int32, logits.shape, 1).astype(F32)
    logits = jnp.where(lane < N_EXPERTS, logits, -jnp.inf)
    gates = jnp.zeros_like(logits)
    top = []
    for _ in range(TOP_K):
        m = jnp.max(logits, axis=-1, keepdims=True)
        idx = jnp.min(jnp.where(logits == m, lane, float(LANES)), axis=-1, keepdims=True)
        top.append((m, idx))
        logits = jnp.where(lane == idx, -jnp.inf, logits)
    m0 = top[0][0]
    ex = [jnp.exp(m - m0) for m, _ in top]
    denom = sum(ex)
    for (m, idx), e in zip(top, ex):
        gates = gates + jnp.where(lane == idx, e / denom, 0.0)
    o_ref[...] = gates


def _moe_layer(x, norm_g, router, w_gate, w_up, w_down, tm=512, tf=1792):
    B, T, D = x.shape
    M = B * T
    E, _, F = w_gate.shape
    tm, tf = _ffn_tiles(M, F, tm, tf)
    x2 = x.reshape(M, D)
    g2 = norm_g.reshape(1, D)
    gates = pl.pallas_call(
        _router_kernel,
        grid=(M // tm,),
        in_specs=[pl.BlockSpec((tm, D), lambda i: (i, 0)), pl.BlockSpec((1, D), lambda i: (0, 0)),
                  pl.BlockSpec((D, LANES), lambda i: (0, 0))],
        out_specs=pl.BlockSpec((tm, LANES), lambda i: (i, 0)),
        out_shape=jax.ShapeDtypeStruct((M, LANES), F32),
        compiler_params=_cparams("parallel"),
        name="router",
    )(x2, g2, jnp.pad(router, ((0, 0), (0, LANES - E))))
    tile = pl.BlockSpec((tm, D), lambda i, e, j: (i, 0))
    out = pl.pallas_call(
        functools.partial(_ffn_kernel, n_experts=E),
        grid=(M // tm, E, F // tf),
        in_specs=[tile, pl.BlockSpec((1, D), lambda i, e, j: (0, 0)),
                  pl.BlockSpec((tm, LANES), lambda i, e, j: (i, 0)),
                  pl.BlockSpec((1, D, tf), lambda i, e, j: (e, 0, j)),
                  pl.BlockSpec((1, D, tf), lambda i, e, j: (e, 0, j)),
                  pl.BlockSpec((1, tf, D), lambda i, e, j: (e, j, 0))],
        out_specs=tile,
        out_shape=jax.ShapeDtypeStruct((M, D), F32),
        scratch_shapes=[pltpu.VMEM((tm, D), BF16), pltpu.VMEM((tm, D), F32)],
        compiler_params=_cparams("parallel", "arbitrary", "arbitrary"),
        name="moe",
    )(x2, g2, gates, w_gate.astype(BF16), w_up.astype(BF16), w_down.astype(BF16))
    return out.reshape(B, T, D)


def _final_norm_kernel(x_ref, g_ref, o_ref):
    x = x_ref[...]
    o_ref[...] = x * lax.rsqrt(jnp.mean(x * x, axis=-1, keepdims=True) + EPS) * g_ref[...]


def _final_norm(x, g, tm=1024):
    M, D = x.shape
    tm = _pick_tile(M, tm)
    return pl.pallas_call(
        _final_norm_kernel,
        grid=(M // tm,),
        in_specs=[pl.BlockSpec((tm, D), lambda i: (i, 0)), pl.BlockSpec((1, D), lambda i: (0, 0))],
        out_specs=pl.BlockSpec((tm, D), lambda i: (i, 0)),
        out_shape=jax.ShapeDtypeStruct((M, D), F32),
        compiler_params=_cparams("parallel"),
        name="final_norm",
    )(x, g.reshape(1, D))


def kernel(x, mem, norm_mix, w_in, rw_mu, rw_w0, rw_w2, rw_a0, rw_a2, rw_k_k, rw_k_a, rw_r_k,
           rw_gn_g, rw_gn_b, rw_v0, rw_v1, rw_v2, gdn_conv, gdn_A_log, gdn_dt_bias, gdn_norm_g,
           mla_q_norm, mla_w_uq, mla_kv_norm, mla_w_uk, mla_w_uv, idx_w_q, idx_k_g, idx_k_b,
           w_mix_out, mem_norm, norm_xattn, xa_w_q, xa_w_kv, xa_w_o, norm_ffn, ffn_w_gate,
           ffn_w_up, ffn_w_down, moe_router, moe_w_gate, moe_w_up, moe_w_down, final_norm):
    B, T, D = x.shape
    M = B * T
    depth = w_in.shape[0]
    v_first = None
    col0 = (0, A_IN, A_IN + B_IN, A_IN + B_IN + C_IN)
    widths = (A_IN, B_IN, C_IN, G_IN)
    for l in range(depth):
        x2 = x.reshape(M, D)
        p_a, p_b, p_c, p_g = (
            _mm(x2, w_in[l][:, c:c + w], norm_g=norm_mix[l], keep_pad=True)
            for c, w in zip(col0, widths))
        vres = None if l == 0 else (rw_v0[l - 1], rw_v1[l - 1], rw_v2[l - 1])
        y_a, v_first = _rwkv_branch(p_a.reshape(B, T, -1), v_first, rw_mu[l], rw_w0[l], rw_w2[l],
                                    rw_a0[l], rw_a2[l], rw_k_k[l], rw_k_a[l], rw_r_k[l],
                                    rw_gn_g[l], rw_gn_b[l], vres)
        y_b = _gdn_branch(p_b.reshape(B, T, -1), gdn_conv[l], gdn_A_log[l], gdn_dt_bias[l],
                          gdn_norm_g[l])
        y_c = _dsa_branch(p_c.reshape(B, T, -1), mla_q_norm[l], mla_w_uq[l], mla_kv_norm[l],
                          mla_w_uk[l], mla_w_uv[l], idx_w_q[l], idx_k_g[l], idx_k_b[l])
        x = _mix_layer(x2, y_a.reshape(M, D), y_b.reshape(M, D), y_c.reshape(M, D), p_g,
                       w_mix_out[l]).reshape(B, T, D)
        x = _xattn_layer(x, mem, mem_norm, norm_xattn[l], xa_w_q[l], xa_w_kv[l], xa_w_o[l])
        i = l // 2
        if l % 2 == 0:
            x = _ffn_layer(x, norm_ffn[l], ffn_w_gate[i], ffn_w_up[i], ffn_w_down[i])
        else:
            x = _moe_layer(x, norm_ffn[l], moe_router[i], moe_w_gate[i], moe_w_up[i], moe_w_down[i])
    return _final_norm(x.reshape(M, D), final_norm).reshape(B, T, D)
```

```python
import functools
import math

import jax
import jax.numpy as jnp
from jax import lax
from jax.experimental import pallas as pl
from jax.experimental.pallas import tpu as pltpu

F32 = jnp.float32
BF16 = jnp.bfloat16
HIGHEST = lax.Precision.HIGHEST

D_MODEL = 1024
EPS = 1e-6
LANES = 128
VMEM_LIMIT = 48 * 1024 * 1024

RW_HEAD = 64
RW_HEADS = D_MODEL // RW_HEAD
RW_DECAY_LORA = 64
RW_AAA_LORA = 64
RW_MV_LORA = 32
RW_GN_EPS = 64e-5
RW_CHUNK = 64

GDN_HEAD = 128
GDN_HEADS = D_MODEL // GDN_HEAD
GDN_CONV = 4
GDN_CHUNK = 64

MLA_HEADS = 8
MLA_HEAD = D_MODEL // MLA_HEADS
MLA_Q_RANK = 256
MLA_KV_RANK = 256
IDX_HEADS = 8
IDX_HEAD = 64
TOPK_MAX = 256

XA_HEADS = 4
XA_HEAD = D_MODEL // XA_HEADS

N_EXPERTS = 8
TOP_K = 2

A_SPLITS = (D_MODEL, D_MODEL, D_MODEL, RW_DECAY_LORA, RW_AAA_LORA)
B_SPLITS = (3 * D_MODEL, GDN_HEADS, GDN_HEADS)
C_SPLITS = (MLA_Q_RANK, MLA_KV_RANK, IDX_HEAD, IDX_HEADS)
A_IN = sum(A_SPLITS)
B_IN = sum(B_SPLITS)
C_IN = sum(C_SPLITS)
G_IN = 3 * D_MODEL

NEG_BIG = -1e30


def _cparams(*sem):
    return pltpu.CompilerParams(dimension_semantics=sem, vmem_limit_bytes=VMEM_LIMIT)


def _round_up(n, m):
    return (n + m - 1) // m * m


def _pick_tile(n, pref):
    t = min(n, pref)
    while n % t:
        t -= 8
    return t


def _dot(a, b):
    return jnp.dot(a.astype(BF16), b.astype(BF16), preferred_element_type=F32)


def _dot_nt(a, b):
    return lax.dot_general(a.astype(BF16), b.astype(BF16), (((1,), (1,)), ((), ())),
                           preferred_element_type=F32)


def _dot_tn(a, b):
    return lax.dot_general(a.astype(BF16), b.astype(BF16), (((0,), (0,)), ((), ())),
                           preferred_element_type=F32)


def _dot_hi(a, b):
    return jnp.dot(a, b, preferred_element_type=F32, precision=HIGHEST)


def _split_bf16(a):
    hi = a.astype(BF16)
    return hi, (a - hi.astype(F32)).astype(BF16)


def _dot_split(ah, al, bh, bl):
    n = ah.shape[0]
    top = jnp.dot(jnp.concatenate([ah, al], axis=0), bh, preferred_element_type=F32)
    return top[:n] + top[n:] + jnp.dot(ah, bl, preferred_element_type=F32)


def _nilpotent_inverse(ms, eye, size):
    splits = [_split_bf16(m) for m in ms]
    xs = [eye + m for m in ms]
    ps = [_dot_split(h, l, h, l) for h, l in splits]
    span = 2
    while span < size:
        splits = [_split_bf16(p) for p in ps]
        xsplits = [_split_bf16(x) for x in xs]
        xs = [x + _dot_split(xh, xl, h, l) for x, (xh, xl), (h, l) in zip(xs, xsplits, splits)]
        span *= 2
        if span < size:
            ps = [_dot_split(h, l, h, l) for h, l in splits]
    return xs


def _cumsum_rows(tril_bf16, x):
    hi = x.astype(BF16)
    r1 = x - hi.astype(F32)
    mid = r1.astype(BF16)
    lo = (r1 - mid.astype(F32)).astype(BF16)
    n = x.shape[1]
    parts = jnp.dot(tril_bf16, jnp.concatenate([hi, mid, lo], axis=1), preferred_element_type=F32)
    return parts[:, :n] + (parts[:, n:2 * n] + parts[:, 2 * n:])


def _mm_kernel(*refs, has_norm, has_res):
    it = iter(refs)
    x_ref = next(it)
    w_ref = next(it)
    g_ref = next(it) if has_norm else None
    r_ref = next(it) if has_res else None
    o_ref = next(it)
    xn_ref = next(it)

    @pl.when(pl.program_id(1) == 0)
    def _():
        x = x_ref[...].astype(F32)
        if has_norm:
            x = x * lax.rsqrt(jnp.mean(x * x, axis=-1, keepdims=True) + EPS) * g_ref[...]
        xn_ref[...] = x.astype(BF16)

    acc = jnp.dot(xn_ref[...], w_ref[...], preferred_element_type=F32)
    if has_res:
        acc = acc + r_ref[...]
    o_ref[...] = acc.astype(o_ref.dtype)


def _mm(x, w, norm_g=None, residual=None, tm=1024, tn=1024, keep_pad=False):
    M, K = x.shape
    N = w.shape[1]
    Np = _round_up(N, LANES)
    wb = w.astype(BF16)
    if Np != N:
        wb = jnp.pad(wb, ((0, 0), (0, Np - N)))
    tm = _pick_tile(M, tm)
    tn = min(Np, tn)
    while Np % tn:
        tn -= LANES
    args = [x, wb]
    in_specs = [pl.BlockSpec((tm, K), lambda i, j: (i, 0)),
                pl.BlockSpec((K, tn), lambda i, j: (0, j))]
    if norm_g is not None:
        args.append(norm_g.reshape(1, K).astype(F32))
        in_specs.append(pl.BlockSpec((1, K), lambda i, j: (0, 0)))
    if residual is not None:
        assert Np == N
        args.append(residual)
        in_specs.append(pl.BlockSpec((tm, tn), lambda i, j: (i, j)))
    out = pl.pallas_call(
        functools.partial(_mm_kernel, has_norm=norm_g is not None, has_res=residual is not None),
        grid=(M // tm, Np // tn),
        in_specs=in_specs,
        out_specs=pl.BlockSpec((tm, tn), lambda i, j: (i, j)),
        out_shape=jax.ShapeDtypeStruct((M, Np), F32),
        scratch_shapes=[pltpu.VMEM((tm, K), BF16)],
        compiler_params=_cparams("parallel", "arbitrary"),
        name="mm",
    )(*args)
    return out if (Np == N or keep_pad) else out[:, :N]


def _rwkv_kernel(r_ref, lw_ref, k_ref, v_ref, kk_ref, a_ref, rk_ref, gg_ref, gb_ref, o_ref, s_ref,
                 *, chunk, heads_per_block):
    C, HB, N = chunk, heads_per_block, RW_HEAD

    @pl.when(pl.program_id(2) == 0)
    def _():
        s_ref[...] = jnp.zeros_like(s_ref)

    NP = HB // 2
    W = 2 * N
    n_chunks = r_ref.shape[1] // C
    row = lax.broadcasted_iota(jnp.int32, (C, C), 0)
    col = lax.broadcasted_iota(jnp.int32, (C, C), 1)
    incl = row >= col
    strict = row > col
    strict4 = jnp.concatenate([strict, strict, incl, incl], axis=0)
    tril_b = incl.astype(BF16)
    eye = (row == col).astype(F32)
    lane = lax.broadcasted_iota(jnp.int32, (1, W), 1)
    in_a = lane < N
    mask_a = in_a.astype(F32)
    mask_b = 1.0 - mask_a
    srow = lax.broadcasted_iota(jnp.int32, (W, W), 0)
    scol = lax.broadcasted_iota(jnp.int32, (W, W), 1)
    block_diag = ((srow < N) == (scol < N)).astype(F32)

    def halves(stacked):
        return jnp.where(in_a, stacked[:C], stacked[C:])

    def head_sum(x):
        sa = jnp.sum(x * mask_a, axis=-1, keepdims=True)
        sb = jnp.sum(x * mask_b, axis=-1, keepdims=True)
        return jnp.where(in_a, sa, sb)

    incl2 = jnp.concatenate([incl, incl], axis=0)

    def chunk_body(c, carry):
        sl = pl.ds(pl.multiple_of(c * C, C), C)
        P = range(NP)
        cols = [slice(p * W, (p + 1) * W) for p in P]
        lw = [lw_ref[0, sl, cs] for cs in cols]
        cum = [_cumsum_rows(tril_b, x) for x in lw]
        g = [jnp.exp(x) for x in cum]
        g_prev = [jnp.exp(x - y) for x, y in zip(cum, lw)]
        g_inv = [jnp.exp(-x) for x in cum]
        g_end = [x[C - 1:C, :] for x in g]
        kk = [kk_ref[0, sl, cs] for cs in cols]
        a_bar = [-x * y for x, y in zip(kk, g_prev)]
        b_til = [x * a_ref[0, sl, cs] * y for x, cs, y in zip(kk, cols, g_inv)]
        k_til = [k_ref[0, sl, cs] * y for cs, y in zip(cols, g_inv)]
        r_bar = [r_ref[0, sl, cs] * y for cs, y in zip(cols, g)]
        lhs = [jnp.concatenate([x, y], axis=0) for x, y in zip(a_bar, r_bar)]
        rhs = [jnp.concatenate([x, y], axis=0) for x, y in zip(b_til, k_til)]
        pair = [_dot_nt(jnp.concatenate([x * mask_a, x * mask_b], axis=0), y)
                for x, y in zip(lhs, rhs)]
        l_ab = [jnp.where(strict, pr[base:base + C, :C], 0.0) for pr in pair for base in (0, 2 * C)]
        t_inv = _nilpotent_inverse(l_ab, eye, C)
        on_v = [jnp.where(strict4, jnp.concatenate(
            [pr[:C, C:], pr[2 * C:3 * C, C:], pr[C:2 * C, C:], pr[3 * C:, C:]], axis=0), 0.0)
            for pr in pair]
        v = [v_ref[0, sl, cs] for cs in cols]
        from_v = [_dot(x, y) for x, y in zip(on_v, v)]
        s = [s_ref[p] for p in P]
        from_state = [_dot_nt(x, y) for x, y in zip(lhs, s)]
        u = [halves(_dot(jnp.concatenate([t_inv[2 * p], t_inv[2 * p + 1]], axis=0),
                         from_state[p][:C] + halves(from_v[p][:2 * C]))) for p in P]
        a_rb = [jnp.where(incl2, jnp.concatenate([pr[C:2 * C, :C], pr[3 * C:, :C]], axis=0), 0.0)
                for pr in pair]
        from_u = [_dot(x, y) for x, y in zip(a_rb, u)]
        upd = [_dot_tn(jnp.concatenate([v[p], u[p]], axis=0),
                       jnp.concatenate([k_til[p] * g_end[p], b_til[p] * g_end[p]], axis=0))
               for p in P]
        for p in P:
            s_ref[p] = s[p] * g_end[p] + upd[p] * block_diag
        for p in P:
            cs = cols[p]
            o = from_state[p][C:] + halves(from_v[p][2 * C:]) + halves(from_u[p])
            mean = head_sum(o) * (1.0 / N)
            cen = o - mean
            var = head_sum(cen * cen) * (1.0 / N)
            o = cen * lax.rsqrt(var + RW_GN_EPS) * gg_ref[:, cs] + gb_ref[:, cs]
            bonus = head_sum(r_ref[0, sl, cs] * k_ref[0, sl, cs] * rk_ref[:, cs]) * v[p]
            o_ref[0, sl, cs] = o + bonus
        return carry

    lax.fori_loop(0, n_chunks, chunk_body, 0)


def _rwkv_recurrence(r, lw, k, v, kk, a, r_k, gn_g, gn_b, tb=256, heads_per_block=16):
    B, T, D = r.shape
    HB = heads_per_block
    W = HB * RW_HEAD
    tb = _pick_tile(T, tb)
    seq = pl.BlockSpec((1, tb, W), lambda b, h, t: (b, t, h))
    vec = pl.BlockSpec((1, W), lambda b, h, t: (0, h))
    return pl.pallas_call(
        functools.partial(_rwkv_kernel, chunk=min(RW_CHUNK, tb), heads_per_block=HB),
        grid=(B, D // W, T // tb),
        in_specs=[seq] * 6 + [vec] * 3,
        out_specs=seq,
        out_shape=jax.ShapeDtypeStruct((B, T, D), F32),
        scratch_shapes=[pltpu.VMEM((HB // 2, 2 * RW_HEAD, 2 * RW_HEAD), F32)],
        compiler_params=_cparams("parallel", "parallel", "arbitrary"),
        name="rwkv7",
    )(r, lw, k, v, kk, a, r_k.reshape(1, D), gn_g.reshape(1, D), gn_b.reshape(1, D))


def _gdn_kernel(q_ref, k_ref, v_ref, beta_ref, gcol_ref, grow_ref, ng_ref, o_ref, s_ref, *, chunk):
    C = chunk

    @pl.when(pl.program_id(1) == 0)
    def _():
        s_ref[...] = jnp.zeros_like(s_ref)

    n_chunks = q_ref.shape[1] // C
    row = lax.broadcasted_iota(jnp.int32, (C, C), 0)
    col = lax.broadcasted_iota(jnp.int32, (C, C), 1)
    incl = row >= col
    strict = row > col
    eye = (row == col).astype(F32)
    ng = ng_ref[...]

    def chunk_body(c, carry):
        sl = pl.ds(pl.multiple_of(c * C, C), C)
        beta_all = beta_ref[0, sl, :]
        gcol_all = gcol_ref[0, sl, :]
        grow_all = grow_ref[0, 0, c]
        H = range(GDN_HEADS)
        cols = [slice(h * GDN_HEAD, (h + 1) * GDN_HEAD) for h in H]
        beta = [beta_all[:, h:h + 1] for h in H]
        gcol = [gcol_all[:, h:h + 1] for h in H]
        k = [k_ref[0, sl, cs] for cs in cols]
        kb = [x * y for x, y in zip(k, beta)]
        decay = [jnp.exp(jnp.where(incl, gcol[h] - grow_all[h:h + 1, :], -jnp.inf)) for h in H]
        pair = [_dot_nt(jnp.concatenate([kb[h], q_ref[0, sl, cols[h]]], axis=0), k[h])
                for h in H]
        t_inv = _nilpotent_inverse(
            [-jnp.where(strict, pair[h][:C] * decay[h], 0.0) for h in H], eye, C)
        eg = [jnp.exp(x) for x in gcol]
        sol = [_dot(t_inv[h], jnp.concatenate([v_ref[0, sl, cols[h]] * beta[h], kb[h] * eg[h]],
                                              axis=-1)) for h in H]
        s = [s_ref[h] for h in H]
        from_state = [_dot(jnp.concatenate([sol[h][:, GDN_HEAD:], q_ref[0, sl, cols[h]] * eg[h]],
                                           axis=0), s[h]) for h in H]
        v_new = [sol[h][:, :GDN_HEAD] - from_state[h][:C] for h in H]
        intra = [_dot(pair[h][C:] * decay[h], v_new[h]) for h in H]
        g_end = [x[C - 1:C, :] for x in gcol]
        upd = [_dot_tn(k[h] * jnp.exp(g_end[h] - gcol[h]), v_new[h]) for h in H]
        for h in H:
            s_ref[h] = s[h] * jnp.exp(g_end[h]) + upd[h]
        for h in H:
            o = from_state[h][C:] + intra[h]
            o = o * lax.rsqrt(jnp.mean(o * o, axis=-1, keepdims=True) + EPS) * ng
            o_ref[0, sl, cols[h]] = o
        return carry

    lax.fori_loop(0, n_chunks, chunk_body, 0)


def _gdn_recurrence(q, k, v, beta, gcum, norm_g, tb=256):
    B, T, D = q.shape
    H, Dh = GDN_HEADS, GDN_HEAD
    tb = _pick_tile(T, tb)
    C = min(GDN_CHUNK, tb)
    grow = jnp.transpose(gcum.reshape(B, T // tb, tb // C, C, H), (0, 1, 2, 4, 3))
    seq = pl.BlockSpec((1, tb, D), lambda b, t: (b, t, 0))
    colspec = pl.BlockSpec((1, tb, H), lambda b, t: (b, t, 0))
    rowspec = pl.BlockSpec((1, 1, tb // C, H, C), lambda b, t: (b, t, 0, 0, 0))
    return pl.pallas_call(
        functools.partial(_gdn_kernel, chunk=C),
        grid=(B, T // tb),
        in_specs=[seq, seq, seq, colspec, colspec, rowspec,
                  pl.BlockSpec((1, Dh), lambda b, t: (0, 0))],
        out_specs=seq,
        out_shape=jax.ShapeDtypeStruct((B, T, D), F32),
        scratch_shapes=[pltpu.VMEM((H, Dh, Dh), F32)],
        compiler_params=_cparams("parallel", "arbitrary"),
        name="gdn",
    )(q, k, v, beta, gcum, grow, norm_g.reshape(1, Dh))


def _split(t, sizes):
    offs = []
    acc = 0
    for s in sizes[:-1]:
        acc += s
        offs.append(acc)
    return jnp.split(t, offs, axis=-1)


def _prev_rows(p, tm, rows, width):
    B, T, W = p.shape
    tail = p.reshape(B, T // tm, tm, W)[:, :-1, tm - rows:, :width]
    return jnp.pad(tail, ((0, 0), (1, 0), (0, 0), (0, 0)))


def _group_sumsq(x, ones_bd):
    hi, lo = _split_bf16(x * x)
    return (jnp.dot(hi, ones_bd, preferred_element_type=F32)
            + jnp.dot(lo, ones_bd, preferred_element_type=F32))


def _shift_rows(x, halo, s):
    rolled = pltpu.roll(x, s, 0)
    row = lax.broadcasted_iota(jnp.int32, (8, x.shape[1]), 0)
    top = jnp.where(row < s, pltpu.roll(halo, s, 0), rolled[:8])
    return jnp.concatenate([top, rolled[8:]], axis=0)


def _rwkv_prep_kernel(*refs, has_vres):
    D = D_MODEL
    if has_vres:
        (pa_ref, prev_ref, mu_ref, w0_ref, w2_ref, a0_ref, a2_ref, kk_ref, ka_ref,
         vf_ref, v0_ref, v1_ref, v2_ref, r_o, lw_o, k_o, v_o, kkn_o, a_o) = refs
    else:
        (pa_ref, prev_ref, mu_ref, w0_ref, w2_ref, a0_ref, a2_ref, kk_ref, ka_ref,
         r_o, lw_o, k_o, v_o, kkn_o, a_o) = refs
    x = pa_ref[0]
    x = x + (_shift_rows(x, prev_ref[0, 0], 1) - x) * mu_ref[...]
    r = x[:, :D]
    k = x[:, D:2 * D]
    v = x[:, 2 * D:3 * D]
    xw = x[:, 3 * D:3 * D + RW_DECAY_LORA]
    xa = x[:, 3 * D + RW_DECAY_LORA:]
    lw_o[0] = -math.exp(-0.5) * jax.nn.sigmoid(w0_ref[...] + _dot(jnp.tanh(xw), w2_ref[...]))
    a = jax.nn.sigmoid(a0_ref[...] + _dot(xa, a2_ref[...]))
    if has_vres:
        gate = jax.nn.sigmoid(v0_ref[...] + _dot(_dot(v, v1_ref[...]), v2_ref[...]))
        v = v + (vf_ref[0] - v) * gate
    r_o[0] = r
    v_o[0] = v
    a_o[0] = a
    k_o[0] = k * (1.0 + (a - 1.0) * ka_ref[...])
    lane_r = lax.broadcasted_iota(jnp.int32, (LANES, LANES), 0) // RW_HEAD
    lane_c = lax.broadcasted_iota(jnp.int32, (LANES, LANES), 1) // RW_HEAD
    ones_bd = (lane_r == lane_c).astype(BF16)
    for j in range(D // LANES):
        cs = slice(j * LANES, (j + 1) * LANES)
        kx = k[:, cs] * kk_ref[:, cs]
        kkn_o[0, :, cs] = kx * lax.rsqrt(_group_sumsq(kx, ones_bd) + EPS)


def _rwkv_branch(p_a, v_first, mu, w0, w2, a0, a2, k_k, k_a, r_k, gn_g, gn_b, vres, tm=256):
    B, T, _ = p_a.shape
    D = D_MODEL
    tm = _pick_tile(T, tm)
    row = lambda n: pl.BlockSpec((1, n), lambda b, i: (0, 0))
    mat = lambda a, b_: pl.BlockSpec((a, b_), lambda b, i: (0, 0))
    seq = pl.BlockSpec((1, tm, D), lambda b, i: (b, i, 0))
    args = [p_a, _prev_rows(p_a, tm, 8, A_IN), mu.reshape(1, A_IN), w0.reshape(1, D), w2.astype(BF16),
            a0.reshape(1, D), a2.astype(BF16), k_k.reshape(1, D), k_a.reshape(1, D)]
    in_specs = [pl.BlockSpec((1, tm, A_IN), lambda b, i: (b, i, 0)),
                pl.BlockSpec((1, 1, 8, A_IN), lambda b, i: (b, i, 0, 0)),
                row(A_IN), row(D), mat(RW_DECAY_LORA, D), row(D), mat(RW_AAA_LORA, D), row(D), row(D)]
    if vres is not None:
        v0, v1, v2 = vres
        pad = LANES - RW_MV_LORA
        args += [v_first, v0.reshape(1, D), jnp.pad(v1, ((0, 0), (0, pad))).astype(BF16),
                 jnp.pad(v2, ((0, pad), (0, 0))).astype(BF16)]
        in_specs += [seq, row(D), mat(D, LANES), mat(LANES, D)]
    r, lw, k, v, kk, a = pl.pallas_call(
        functools.partial(_rwkv_prep_kernel, has_vres=vres is not None),
        grid=(B, T // tm),
        in_specs=in_specs,
        out_specs=[seq] * 6,
        out_shape=[jax.ShapeDtypeStruct((B, T, D), F32)] * 6,
        compiler_params=_cparams("parallel", "parallel"),
        name="rwkv_prep",
    )(*args)
    if vres is None:
        v_first = v
    return _rwkv_recurrence(r, lw, k, v, kk, a, r_k, gn_g, gn_b), v_first


def _gdn_prep_kernel(pb_ref, halo_ref, cw_ref, q_o, k_o, v_o):
    D = D_MODEL
    x = pb_ref[0]
    halo = halo_ref[0, 0]
    acc = x * cw_ref[GDN_CONV - 1:GDN_CONV, :]
    for s in range(1, GDN_CONV):
        acc = acc + _shift_rows(x, halo, s) * cw_ref[GDN_CONV - 1 - s:GDN_CONV - s, :]
    y = acc * jax.nn.sigmoid(acc)
    v_o[0] = y[:, 2 * D:]
    ones = jnp.ones((LANES, LANES), BF16)
    for j in range(D // GDN_HEAD):
        cs = slice(j * GDN_HEAD, (j + 1) * GDN_HEAD)
        q = y[:, cs]
        k = y[:, D + j * GDN_HEAD:D + (j + 1) * GDN_HEAD]
        q_o[0, :, cs] = q * (lax.rsqrt(_group_sumsq(q, ones) + EPS) * GDN_HEAD ** -0.5)
        k_o[0, :, cs] = k * lax.rsqrt(_group_sumsq(k, ones) + EPS)


def _gdn_branch(p_b, conv_w, A_log, dt_bias, norm_g, tm=256):
    B, T, _ = p_b.shape
    D, H, C = D_MODEL, GDN_HEADS, GDN_CHUNK
    tm = _pick_tile(T, tm)
    seq = pl.BlockSpec((1, tm, D), lambda b, i: (b, i, 0))
    q, k, v = pl.pallas_call(
        _gdn_prep_kernel,
        grid=(B, T // tm),
        in_specs=[pl.BlockSpec((1, tm, 3 * D), lambda b, i: (b, i, 0)),
                  pl.BlockSpec((1, 1, 8, 3 * D), lambda b, i: (b, i, 0, 0)),
                  pl.BlockSpec((GDN_CONV, 3 * D), lambda b, i: (0, 0))],
        out_specs=[seq] * 3,
        out_shape=[jax.ShapeDtypeStruct((B, T, D), F32)] * 3,
        compiler_params=_cparams("parallel", "parallel"),
        name="gdn_prep",
    )(p_b, _prev_rows(p_b, tm, 8, 3 * D), conv_w)
    a_in = p_b[..., 3 * D:3 * D + H]
    b_in = p_b[..., 3 * D + H:3 * D + 2 * H]
    beta = jax.nn.sigmoid(b_in)
    g = -jnp.exp(A_log) * jax.nn.softplus(a_in + dt_bias)
    Cc = min(C, T)
    gcum = jnp.cumsum(g.reshape(B, T // Cc, Cc, H), axis=2).reshape(B, T, H)
    return _gdn_recurrence(q, k, v, beta, gcum, norm_g)


def _dsa_t_kernel(cqt_ref, wit_ref, ki_ref, ckv_ref, ckvt_ref, wqit_ref, wuqt_ref, wuk_ref, wuv_ref,
                  o_ref, key_scr, acc_scr, qlat_scr, *, n_sel, tq, tk, pos_bits):
    H = IDX_HEADS
    q0 = pl.program_id(1) * tq
    nk = (q0 + tq + tk - 1) // tk
    int_min = jnp.int32(-2 ** 31)

    cqt = cqt_ref[0]
    qit = jnp.dot(wqit_ref[...], cqt, preferred_element_type=F32)
    qi_cat = jnp.concatenate([qit[h * IDX_HEAD:(h + 1) * IDX_HEAD] for h in range(H)],
                             axis=1).astype(BF16)
    wit = wit_ref[0]
    wi_cat = jnp.concatenate([wit[h:h + 1] for h in range(H)], axis=1)
    key_off = lax.broadcasted_iota(jnp.int32, (tk, tq), 0)
    q_pos = q0 + lax.broadcasted_iota(jnp.int32, (tk, tq), 1)

    def score_tile(kt, carry):
        ki = ki_ref[0, pl.ds(pl.multiple_of(kt * tk, tk), tk), :]
        z = jnp.maximum(jnp.dot(ki, qi_cat, preferred_element_type=F32), 0.0) * wi_cat
        s = z[:, :tq]
        for h in range(1, H):
            s = s + z[:, h * tq:(h + 1) * tq]
        s = jnp.where(s == 0.0, 0.0, s)
        s = jnp.where(kt * tk + key_off <= q_pos, s, -jnp.inf)
        bits = pltpu.bitcast(s, jnp.int32)
        key_scr[kt] = bits ^ ((bits >> 31) & jnp.int32(0x7FFFFFFF))
        return carry

    lax.fori_loop(0, nk, score_tile, 0)

    def count(pred):
        ways = 4
        def body(kt, acc):
            hit = jnp.where(pred(key_scr[kt], kt * tk + key_off), 1.0, 0.0)
            return acc + jnp.sum(hit.reshape(tk // (8 * ways), ways * 8, tq), axis=0)
        acc = lax.fori_loop(0, nk, body, jnp.zeros((ways * 8, tq), F32))
        return jnp.sum(acc, axis=0, keepdims=True)

    want = jnp.float32(n_sel)
    thr = jnp.where(count(lambda key, pos: key >= 0) >= want, jnp.int32(0), int_min)

    def thr_bit(i, thr):
        cand = thr | jnp.left_shift(jnp.int32(1), 30 - i)
        return jnp.where(count(lambda key, pos: key >= cand) >= want, cand, thr)

    thr = lax.fori_loop(0, 31, thr_bit, thr)

    query_pos = q0 + lax.broadcasted_iota(jnp.int32, (1, tq), 1)
    tied = jnp.logical_and(count(lambda key, pos: key >= thr) > want, query_pos >= n_sel)
    any_tied = jnp.max(jnp.where(tied, 1.0, 0.0)) > 0.0

    def index_cut():
        need = want - count(lambda key, pos: key > thr)

        def pos_bit(i, last):
            cand = last + jnp.left_shift(jnp.int32(1), pos_bits - 1 - i)
            below = count(lambda key, pos: jnp.logical_and(key == thr, pos < cand))
            return jnp.where(below < need, cand, last)

        return lax.fori_loop(0, pos_bits, pos_bit, jnp.zeros((1, tq), jnp.int32))

    last = lax.cond(any_tied, index_cut, lambda: jnp.full((1, tq), 2 ** pos_bits, jnp.int32))

    qt = jnp.dot(wuqt_ref[...], cqt, preferred_element_type=F32)
    q_lat = jnp.concatenate(
        [_dot(wuk_ref[h], qt[h * MLA_HEAD:(h + 1) * MLA_HEAD]) for h in range(MLA_HEADS)],
        axis=1) * MLA_HEAD ** -0.5
    qlat_scr[...] = q_lat.astype(BF16)
    acc_scr[...] = jnp.zeros(acc_scr.shape, F32)
    HG = 2
    GW = HG * tq
    groups = [slice(g * GW, (g + 1) * GW) for g in range(MLA_HEADS // HG)]

    def attend_tile(kt, carry):
        m_old, l_old = carry
        ckv = ckv_ref[0, pl.ds(pl.multiple_of(kt * tk, tk), tk), :]
        ckv_t = ckvt_ref[0, kt]
        key = key_scr[kt]
        pos = kt * tk + key_off
        sel = jnp.logical_or(key > thr, jnp.logical_and(key == thr, pos <= last))
        sel = jnp.logical_and(sel, pos <= q_pos)
        bias = jnp.where(sel, 0.0, NEG_BIG)
        bias = jnp.concatenate([bias] * HG, axis=1)
        logits = [jnp.dot(ckv, qlat_scr[:, g], preferred_element_type=F32) for g in groups]
        m_out, l_out = [], []
        for g, lg in zip(groups, logits):
            lg = lg + bias
            m_new = jnp.maximum(m_old[:, g], jnp.max(lg, axis=0, keepdims=True))
            p = jnp.exp(lg - m_new)
            alpha = jnp.exp(m_old[:, g] - m_new)
            l_out.append(alpha * l_old[:, g] + jnp.sum(p, axis=0, keepdims=True))
            acc_scr[:, g] = alpha * acc_scr[:, g] + jnp.dot(
                ckv_t, p.astype(BF16), preferred_element_type=F32)
            m_out.append(m_new)
        return jnp.concatenate(m_out, axis=1), jnp.concatenate(l_out, axis=1)

    width = MLA_HEADS * tq
    _, l_fin = lax.fori_loop(0, nk, attend_tile,
                             (jnp.full((1, width), NEG_BIG, F32), jnp.zeros((1, width), F32)))
    o_lat = acc_scr[...] / l_fin
    o_ref[0] = jnp.concatenate(
        [_dot_tn(o_lat[:, h * tq:(h + 1) * tq], wuv_ref[h]) for h in range(MLA_HEADS)],
        axis=1)


def _dsa_attention_t(c_q, w_i, k_i, c_kv, w_qi, w_uq, w_uk, w_uv, tq=128, tk=512):
    B, T, _ = c_q.shape
    tq = _pick_tile(T, tq)
    tk = _pick_tile(T, tk)
    n_sel = min(TOPK_MAX, T // 4)
    assert tq % LANES == 0 and tk >= n_sel and tk % tq == 0
    H = MLA_HEADS
    full = lambda *shape: pl.BlockSpec(shape, lambda b, i: (0,) * len(shape))
    c_qt = jnp.transpose(c_q, (0, 2, 1))
    w_it = jnp.transpose(w_i, (0, 2, 1))
    c_kvt = jnp.transpose(c_kv.reshape(B, T // tk, tk, MLA_KV_RANK), (0, 1, 3, 2))
    return pl.pallas_call(
        functools.partial(_dsa_t_kernel, n_sel=n_sel, tq=tq, tk=tk,
                          pos_bits=max(1, (T - 1).bit_length())),
        grid=(B, T // tq),
        in_specs=[pl.BlockSpec((1, MLA_Q_RANK, tq), lambda b, i: (b, 0, i)),
                  pl.BlockSpec((1, IDX_HEADS, tq), lambda b, i: (b, 0, i)),
                  pl.BlockSpec((1, T, IDX_HEAD), lambda b, i: (b, 0, 0)),
                  pl.BlockSpec((1, T, MLA_KV_RANK), lambda b, i: (b, 0, 0)),
                  pl.BlockSpec((1, T // tk, MLA_KV_RANK, tk), lambda b, i: (b, 0, 0, 0)),
                  full(IDX_HEADS * IDX_HEAD, MLA_Q_RANK),
                  full(H * MLA_HEAD, MLA_Q_RANK),
                  full(H, MLA_KV_RANK, MLA_HEAD),
                  full(H, MLA_KV_RANK, MLA_HEAD)],
        out_specs=pl.BlockSpec((1, tq, H * MLA_HEAD), lambda b, i: (b, i, 0)),
        out_shape=jax.ShapeDtypeStruct((B, T, H * MLA_HEAD), F32),
        scratch_shapes=[pltpu.VMEM((T // tk, tk, tq), jnp.int32),
                        pltpu.VMEM((MLA_KV_RANK, H * tq), F32),
                        pltpu.VMEM((MLA_KV_RANK, H * tq), BF16)],
        compiler_params=_cparams("parallel", "arbitrary"),
        name="dsa",
    )(c_qt, w_it, k_i, c_kv, c_kvt, w_qi.T.astype(BF16), w_uq.T.astype(BF16),
      jnp.transpose(w_uk, (1, 0, 2)).astype(BF16), jnp.transpose(w_uv, (1, 0, 2)).astype(BF16))


def _rms(x, g):
    return x * lax.rsqrt(jnp.mean(x * x, axis=-1, keepdims=True) + EPS) * g


def _dsa_prep_kernel(pc_ref, qn_ref, kvn_ref, kig_ref, kib_ref, cq_o, ckv_o, ki_o, wi_o):
    x = pc_ref[...]
    cq_o[...] = _rms(x[:, :MLA_Q_RANK], qn_ref[...]).astype(BF16)
    ckv_o[...] = _rms(x[:, MLA_Q_RANK:MLA_Q_RANK + MLA_KV_RANK], kvn_ref[...]).astype(BF16)
    off = MLA_Q_RANK + MLA_KV_RANK
    ki = x[:, off:off + IDX_HEAD]
    mu = jnp.mean(ki, axis=-1, keepdims=True)
    var = jnp.mean(jnp.square(ki - mu), axis=-1, keepdims=True)
    ki_o[...] = ((ki - mu) * lax.rsqrt(var + EPS) * kig_ref[...] + kib_ref[...]).astype(BF16)
    wi_o[...] = x[:, off + IDX_HEAD:off + IDX_HEAD + IDX_HEADS] * (IDX_HEADS ** -0.5 * IDX_HEAD ** -0.5)


def _dsa_branch(p_c, q_norm, w_uq, kv_norm, w_uk, w_uv, w_qi, ki_g, ki_b, tm=512):
    B, T, W = p_c.shape
    M = B * T
    tm = _pick_tile(M, tm)
    row = lambda n: pl.BlockSpec((1, n), lambda i: (0, 0))
    out = lambda n: pl.BlockSpec((tm, n), lambda i: (i, 0))
    c_q, c_kv, k_i, w_i = pl.pallas_call(
        _dsa_prep_kernel,
        grid=(M // tm,),
        in_specs=[pl.BlockSpec((tm, W), lambda i: (i, 0)), row(MLA_Q_RANK), row(MLA_KV_RANK),
                  row(IDX_HEAD), row(IDX_HEAD)],
        out_specs=[out(MLA_Q_RANK), out(MLA_KV_RANK), out(IDX_HEAD), out(IDX_HEADS)],
        out_shape=[jax.ShapeDtypeStruct((M, MLA_Q_RANK), BF16),
                   jax.ShapeDtypeStruct((M, MLA_KV_RANK), BF16),
                   jax.ShapeDtypeStruct((M, IDX_HEAD), BF16),
                   jax.ShapeDtypeStruct((M, IDX_HEADS), F32)],
        compiler_params=_cparams("parallel"),
        name="dsa_prep",
    )(p_c.reshape(M, W), q_norm.reshape(1, -1), kv_norm.reshape(1, -1), ki_g.reshape(1, -1),
      ki_b.reshape(1, -1))
    return _dsa_attention_t(c_q.reshape(B, T, -1), w_i.reshape(B, T, -1), k_i.reshape(B, T, -1),
                            c_kv.reshape(B, T, -1), w_qi, w_uq, w_uk, w_uv)


def _mix_kernel(ya_ref, yb_ref, yc_ref, ga_ref, gb_ref, gc_ref, w_ref, x_ref, o_ref):
    mix = (jax.nn.sigmoid(ga_ref[...]) * ya_ref[...] + jax.nn.sigmoid(gb_ref[...]) * yb_ref[...]
           + jax.nn.sigmoid(gc_ref[...]) * yc_ref[...])
    o_ref[...] = x_ref[...] + jnp.dot(mix.astype(BF16), w_ref[...], preferred_element_type=F32)


def _mix_layer(x, y_a, y_b, y_c, p_g, w_out, tm=512):
    M, D = x.shape
    tm = _pick_tile(M, tm)
    tile = pl.BlockSpec((tm, D), lambda i: (i, 0))
    gate = lambda c: pl.BlockSpec((tm, D), lambda i: (i, c))
    return pl.pallas_call(
        _mix_kernel,
        grid=(M // tm,),
        in_specs=[tile, tile, tile, gate(0), gate(1), gate(2),
                  pl.BlockSpec((D, D), lambda i: (0, 0)), tile],
        out_specs=tile,
        out_shape=jax.ShapeDtypeStruct((M, D), F32),
        compiler_params=_cparams("parallel"),
        name="mix",
    )(y_a, y_b, y_c, p_g, p_g, p_g, w_out.astype(BF16), x)


def _xattn_kernel(x_ref, g_ref, wq_ref, k_ref, v_ref, wo_ref, o_ref):
    x = x_ref[0]
    h = x * lax.rsqrt(jnp.mean(x * x, axis=-1, keepdims=True) + EPS) * g_ref[...]
    q = jnp.dot(h.astype(BF16), wq_ref[...], preferred_element_type=F32)
    k = k_ref[0]
    v = v_ref[0]
    outs = []
    for hd in range(XA_HEADS):
        cs = slice(hd * XA_HEAD, (hd + 1) * XA_HEAD)
        logits = _dot_nt(q[:, cs], k[:, cs]) * XA_HEAD ** -0.5
        p = jnp.exp(logits - jnp.max(logits, axis=-1, keepdims=True))
        outs.append(_dot(p, v[:, cs]) / jnp.sum(p, axis=-1, keepdims=True))
    o = jnp.concatenate(outs, axis=-1)
    o_ref[0] = x + jnp.dot(o.astype(BF16), wo_ref[...], preferred_element_type=F32)


def _xattn_layer(x, mem, mem_norm, norm_g, w_q, w_kv, w_o, tq=512):
    B, T, D = x.shape
    Mm = mem.shape[1]
    kv = _mm(mem.reshape(B * Mm, D), w_kv, norm_g=mem_norm).reshape(B, Mm, 2 * D)
    tq = _pick_tile(T, tq)
    tile = pl.BlockSpec((1, tq, D), lambda b, i: (b, i, 0))
    wspec = pl.BlockSpec((D, D), lambda b, i: (0, 0))
    return pl.pallas_call(
        _xattn_kernel,
        grid=(B, T // tq),
        in_specs=[tile, pl.BlockSpec((1, D), lambda b, i: (0, 0)), wspec,
                  pl.BlockSpec((1, Mm, D), lambda b, i: (b, 0, 0)),
                  pl.BlockSpec((1, Mm, D), lambda b, i: (b, 0, 1)), wspec],
        out_specs=tile,
        out_shape=jax.ShapeDtypeStruct((B, T, D), F32),
        compiler_params=_cparams("parallel", "parallel"),
        name="xattn",
    )(x, norm_g.reshape(1, D), w_q.astype(BF16), kv, kv, w_o.astype(BF16))


def _ffn_kernel(x_ref, g_ref, wg_ref, wu_ref, wd_ref, o_ref, h_scr, acc_scr):
    j = pl.program_id(1)

    @pl.when(j == 0)
    def _():
        x = x_ref[...]
        h = x * lax.rsqrt(jnp.mean(x * x, axis=-1, keepdims=True) + EPS) * g_ref[...]
        h_scr[...] = h.astype(BF16)
        acc_scr[...] = jnp.zeros_like(acc_scr)

    h = h_scr[...]
    gate = jnp.dot(h, wg_ref[...], preferred_element_type=F32)
    up = jnp.dot(h, wu_ref[...], preferred_element_type=F32)
    act = gate * jax.nn.sigmoid(gate) * up
    acc_scr[...] += jnp.dot(act.astype(BF16), wd_ref[...], preferred_element_type=F32)

    @pl.when(j == pl.num_programs(1) - 1)
    def _():
        o_ref[...] = x_ref[...] + acc_scr[...]


def _ffn_tiles(M, F, tm, tf):
    tm = _pick_tile(M, tm)
    tf = min(F, tf)
    while F % tf or tf % LANES:
        tf -= LANES
    return tm, tf


def _ffn_layer(x, norm_g, w_gate, w_up, w_down, tm=512, tf=1408):
    B, T, D = x.shape
    M = B * T
    F = w_gate.shape[-1]
    tm, tf = _ffn_tiles(M, F, tm, tf)
    tile = pl.BlockSpec((tm, D), lambda i, j: (i, 0))
    out = pl.pallas_call(
        _ffn_kernel,
        grid=(M // tm, F // tf),
        in_specs=[tile, pl.BlockSpec((1, D), lambda i, j: (0, 0)),
                  pl.BlockSpec((D, tf), lambda i, j: (0, j)),
                  pl.BlockSpec((D, tf), lambda i, j: (0, j)),
                  pl.BlockSpec((tf, D), lambda i, j: (j, 0))],
        out_specs=tile,
        out_shape=jax.ShapeDtypeStruct((M, D), F32),
        scratch_shapes=[pltpu.VMEM((tm, D), BF16), pltpu.VMEM((tm, D), F32)],
        compiler_params=_cparams("parallel", "arbitrary"),
        name="ffn",
    )(x.reshape(M, D), norm_g.reshape(1, D), w_gate.astype(BF16), w_up.astype(BF16), w_down.astype(BF16))
    return out.reshape(B, T, D)


def _router_kernel(x_ref, g_ref, w_ref, h_o, route_o):
    x = x_ref[...]
    h = x * lax.rsqrt(jnp.mean(x * x, axis=-1, keepdims=True) + EPS) * g_ref[...]
    h_o[...] = h.astype(BF16)
    logits = _dot_hi(h, w_ref[...])
    lane = lax.broadcasted_iota(jnp.int32, logits.shape, 1).astype(F32)
    logits = jnp.where(lane < N_EXPERTS, logits, -jnp.inf)
    top = []
    for _ in range(TOP_K):
        m = jnp.max(logits, axis=-1, keepdims=True)
        idx = jnp.min(jnp.where(logits == m, lane, float(LANES)), axis=-1, keepdims=True)
        top.append((m, idx))
        logits = jnp.where(lane == idx, -jnp.inf, logits)
    m0 = top[0][0]
    ex = [jnp.exp(m - m0) for m, _ in top]
    denom = sum(ex)
    route = jnp.zeros_like(logits)
    for k, ((m, idx), e) in enumerate(zip(top, ex)):
        route = route + jnp.where(lane == k, idx, 0.0) + jnp.where(lane == TOP_K + k, e / denom, 0.0)
    route_o[...] = route


def _grouped_ffn_kernel(te_ref, nt_ref, x_ref, gate_ref, wg_ref, wu_ref, wd_ref, o_ref, acc_scr):
    i, j = pl.program_id(0), pl.program_id(1)

    @pl.when(j == 0)
    def _():
        acc_scr[...] = jnp.zeros_like(acc_scr)

    @pl.when(i < nt_ref[0])
    def _():
        h = x_ref[...]
        gate = jnp.dot(h, wg_ref[0], preferred_element_type=F32)
        up = jnp.dot(h, wu_ref[0], preferred_element_type=F32)
        act = gate * jax.nn.sigmoid(gate) * up
        acc_scr[...] += jnp.dot(act.astype(BF16), wd_ref[0], preferred_element_type=F32)

    @pl.when(j == pl.num_programs(1) - 1)
    def _():
        o_ref[...] = acc_scr[...] * gate_ref[...]


def _moe_layer(x, norm_g, router, w_gate, w_up, w_down, tm=512, tf=1792):
    B, T, D = x.shape
    M = B * T
    E, _, F = w_gate.shape
    tm, tf = _ffn_tiles(M, F, tm, tf)
    x2 = x.reshape(M, D)
    h, route = pl.pallas_call(
        _router_kernel,
        grid=(M // tm,),
        in_specs=[pl.BlockSpec((tm, D), lambda i: (i, 0)), pl.BlockSpec((1, D), lambda i: (0, 0)),
                  pl.BlockSpec((D, LANES), lambda i: (0, 0))],
        out_specs=[pl.BlockSpec((tm, D), lambda i: (i, 0)), pl.BlockSpec((tm, LANES), lambda i: (i, 0))],
        out_shape=[jax.ShapeDtypeStruct((M, D), BF16), jax.ShapeDtypeStruct((M, LANES), F32)],
        compiler_params=_cparams("parallel"),
        name="router",
    )(x2, norm_g.reshape(1, D), jnp.pad(router, ((0, 0), (0, LANES - E))))

    n_asg = TOP_K * M
    eid = route[:, :TOP_K].astype(jnp.int32).T.reshape(n_asg)
    prob = route[:, TOP_K:2 * TOP_K].T.reshape(n_asg)
    order = jnp.argsort(eid, stable=True)
    eid_s = eid[order]
    counts = jnp.sum(eid[None, :] == jnp.arange(E, dtype=jnp.int32)[:, None], axis=1).astype(jnp.int32)
    tiles_per = (counts + tm - 1) // tm
    tiles_end = jnp.cumsum(tiles_per)
    row0 = (tiles_end - tiles_per) * tm
    first = jnp.cumsum(counts) - counts
    rows = row0[eid_s] + jnp.arange(n_asg, dtype=jnp.int32) - first[eid_s]
    n_rows = n_asg + E * tm
    n_tiles = n_rows // tm
    src = jnp.zeros((n_rows,), jnp.int32).at[rows].set((order % M).astype(jnp.int32))
    row_gate = jnp.zeros((n_rows,), F32).at[rows].set(prob[order])
    row_of = jnp.zeros((n_asg,), jnp.int32).at[order].set(rows)
    tile_expert = jnp.minimum(
        jnp.searchsorted(tiles_end, jnp.arange(n_tiles, dtype=jnp.int32), side="right"),
        E - 1).astype(jnp.int32)
    used_tiles = tiles_end[-1:].astype(jnp.int32)

    ys = pl.pallas_call(
        _grouped_ffn_kernel,
        grid_spec=pltpu.PrefetchScalarGridSpec(
            num_scalar_prefetch=2,
            grid=(n_tiles, F // tf),
            in_specs=[pl.BlockSpec((tm, D), lambda i, j, te, nt: (i, 0)),
                      pl.BlockSpec((tm, 1), lambda i, j, te, nt: (i, 0)),
                      pl.BlockSpec((1, D, tf), lambda i, j, te, nt: (te[i], 0, j)),
                      pl.BlockSpec((1, D, tf), lambda i, j, te, nt: (te[i], 0, j)),
                      pl.BlockSpec((1, tf, D), lambda i, j, te, nt: (te[i], j, 0))],
            out_specs=pl.BlockSpec((tm, D), lambda i, j, te, nt: (i, 0)),
            scratch_shapes=[pltpu.VMEM((tm, D), F32)]),
        out_shape=jax.ShapeDtypeStruct((n_rows, D), F32),
        compiler_params=_cparams("parallel", "arbitrary"),
        name="moe",
    )(tile_expert, used_tiles, h[src], row_gate[:, None], w_gate.astype(BF16), w_up.astype(BF16),
      w_down.astype(BF16))
    out = x2
    for k in range(TOP_K):
        out = out + ys[row_of[k * M:(k + 1) * M]]
    return out.reshape(B, T, D)


def _final_norm_kernel(x_ref, g_ref, o_ref):
    x = x_ref[...]
    o_ref[...] = x * lax.rsqrt(jnp.mean(x * x, axis=-1, keepdims=True) + EPS) * g_ref[...]


def _final_norm(x, g, tm=1024):
    M, D = x.shape
    tm = _pick_tile(M, tm)
    return pl.pallas_call(
        _final_norm_kernel,
        grid=(M // tm,),
        in_specs=[pl.BlockSpec((tm, D), lambda i: (i, 0)), pl.BlockSpec((1, D), lambda i: (0, 0))],
        out_specs=pl.BlockSpec((tm, D), lambda i: (i, 0)),
        out_shape=jax.ShapeDtypeStruct((M, D), F32),
        compiler_params=_cparams("parallel"),
        name="final_norm",
    )(x, g.reshape(1, D))


def kernel(x, mem, norm_mix, w_in, rw_mu, rw_w0, rw_w2, rw_a0, rw_a2, rw_k_k, rw_k_a, rw_r_k,
           rw_gn_g, rw_gn_b, rw_v0, rw_v1, rw_v2, gdn_conv, gdn_A_log, gdn_dt_bias, gdn_norm_g,
           mla_q_norm, mla_w_uq, mla_kv_norm, mla_w_uk, mla_w_uv, idx_w_q, idx_k_g, idx_k_b,
           w_mix_out, mem_norm, norm_xattn, xa_w_q, xa_w_kv, xa_w_o, norm_ffn, ffn_w_gate,
           ffn_w_up, ffn_w_down, moe_router, moe_w_gate, moe_w_up, moe_w_down, final_norm):
    B, T, D = x.shape
    M = B * T
    depth = w_in.shape[0]
    v_first = None
    col0 = (0, A_IN, A_IN + B_IN, A_IN + B_IN + C_IN)
    widths = (A_IN, B_IN, C_IN, G_IN)
    for l in range(depth):
        x2 = x.reshape(M, D)
        p_a, p_b, p_c, p_g = (
            _mm(x2, w_in[l][:, c:c + w], norm_g=norm_mix[l], keep_pad=True)
            for c, w in zip(col0, widths))
        vres = None if l == 0 else (rw_v0[l - 1], rw_v1[l - 1], rw_v2[l - 1])
        y_a, v_first = _rwkv_branch(p_a.reshape(B, T, -1), v_first, rw_mu[l], rw_w0[l], rw_w2[l],
                                    rw_a0[l], rw_a2[l], rw_k_k[l], rw_k_a[l], rw_r_k[l],
                                    rw_gn_g[l], rw_gn_b[l], vres)
        y_b = _gdn_branch(p_b.reshape(B, T, -1), gdn_conv[l], gdn_A_log[l], gdn_dt_bias[l],
                          gdn_norm_g[l])
        y_c = _dsa_branch(p_c.reshape(B, T, -1), mla_q_norm[l], mla_w_uq[l], mla_kv_norm[l],
                          mla_w_uk[l], mla_w_uv[l], idx_w_q[l], idx_k_g[l], idx_k_b[l])
        x = _mix_layer(x2, y_a.reshape(M, D), y_b.reshape(M, D), y_c.reshape(M, D), p_g,
                       w_mix_out[l]).reshape(B, T, D)
        x = _xattn_layer(x, mem, mem_norm, norm_xattn[l], xa_w_q[l], xa_w_kv[l], xa_w_o[l])
        i = l // 2
        if l % 2 == 0:
            x = _ffn_layer(x, norm_ffn[l], ffn_w_gate[i], ffn_w_up[i], ffn_w_down[i])
        else:
            x = _moe_layer(x, norm_ffn[l], moe_router[i], moe_w_gate[i], moe_w_up[i], moe_w_down[i])
    return _final_norm(x.reshape(M, D), final_norm).reshape(B, T, D)
```

```python
import functools
import math

import jax
import jax.numpy as jnp
from jax import lax
from jax.experimental import pallas as pl
from jax.experimental.pallas import tpu as pltpu

F32 = jnp.float32
BF16 = jnp.bfloat16
HIGHEST = lax.Precision.HIGHEST

D_MODEL = 1024
EPS = 1e-6
LANES = 128
VMEM_LIMIT = 48 * 1024 * 1024

RW_HEAD = 64
RW_HEADS = D_MODEL // RW_HEAD
RW_DECAY_LORA = 64
RW_AAA_LORA = 64
RW_MV_LORA = 32
RW_GN_EPS = 64e-5
RW_CHUNK = 64

GDN_HEAD = 128
GDN_HEADS = D_MODEL // GDN_HEAD
GDN_CONV = 4
GDN_CHUNK = 64

MLA_HEADS = 8
MLA_HEAD = D_MODEL // MLA_HEADS
MLA_Q_RANK = 256
MLA_KV_RANK = 256
IDX_HEADS = 8
IDX_HEAD = 64
TOPK_MAX = 256

XA_HEADS = 4
XA_HEAD = D_MODEL // XA_HEADS

N_EXPERTS = 8
TOP_K = 2

A_SPLITS = (D_MODEL, D_MODEL, D_MODEL, RW_DECAY_LORA, RW_AAA_LORA)
B_SPLITS = (3 * D_MODEL, GDN_HEADS, GDN_HEADS)
C_SPLITS = (MLA_Q_RANK, MLA_KV_RANK, IDX_HEAD, IDX_HEADS)
A_IN = sum(A_SPLITS)
B_IN = sum(B_SPLITS)
C_IN = sum(C_SPLITS)
G_IN = 3 * D_MODEL

NEG_BIG = -1e30


def _cparams(*sem):
    return pltpu.CompilerParams(dimension_semantics=sem, vmem_limit_bytes=VMEM_LIMIT)


def _round_up(n, m):
    return (n + m - 1) // m * m


def _pick_tile(n, pref):
    t = min(n, pref)
    while n % t:
        t -= 8
    return t


def _dot(a, b):
    return jnp.dot(a.astype(BF16), b.astype(BF16), preferred_element_type=F32)


def _dot_nt(a, b):
    return lax.dot_general(a.astype(BF16), b.astype(BF16), (((1,), (1,)), ((), ())),
                           preferred_element_type=F32)


def _dot_tn(a, b):
    return lax.dot_general(a.astype(BF16), b.astype(BF16), (((0,), (0,)), ((), ())),
                           preferred_element_type=F32)


def _dot_hi(a, b):
    return jnp.dot(a, b, preferred_element_type=F32, precision=HIGHEST)


def _split_bf16(a):
    hi = a.astype(BF16)
    return hi, (a - hi.astype(F32)).astype(BF16)


def _dot_split(ah, al, bh, bl):
    n = ah.shape[0]
    top = jnp.dot(jnp.concatenate([ah, al], axis=0), bh, preferred_element_type=F32)
    return top[:n] + top[n:] + jnp.dot(ah, bl, preferred_element_type=F32)


def _nilpotent_inverse(ms, eye, size):
    splits = [_split_bf16(m) for m in ms]
    xs = [eye + m for m in ms]
    ps = [_dot_split(h, l, h, l) for h, l in splits]
    span = 2
    while span < size:
        splits = [_split_bf16(p) for p in ps]
        xsplits = [_split_bf16(x) for x in xs]
        xs = [x + _dot_split(xh, xl, h, l) for x, (xh, xl), (h, l) in zip(xs, xsplits, splits)]
        span *= 2
        if span < size:
            ps = [_dot_split(h, l, h, l) for h, l in splits]
    return xs


def _cumsum_rows(tril_bf16, x):
    hi = x.astype(BF16)
    r1 = x - hi.astype(F32)
    mid = r1.astype(BF16)
    lo = (r1 - mid.astype(F32)).astype(BF16)
    n = x.shape[1]
    parts = jnp.dot(tril_bf16, jnp.concatenate([hi, mid, lo], axis=1), preferred_element_type=F32)
    return parts[:, :n] + (parts[:, n:2 * n] + parts[:, 2 * n:])


def _mm_kernel(*refs, has_norm, has_res):
    it = iter(refs)
    x_ref = next(it)
    w_ref = next(it)
    g_ref = next(it) if has_norm else None
    r_ref = next(it) if has_res else None
    o_ref = next(it)
    xn_ref = next(it)

    @pl.when(pl.program_id(1) == 0)
    def _():
        x = x_ref[...].astype(F32)
        if has_norm:
            x = x * lax.rsqrt(jnp.mean(x * x, axis=-1, keepdims=True) + EPS) * g_ref[...]
        xn_ref[...] = x.astype(BF16)

    acc = jnp.dot(xn_ref[...], w_ref[...], preferred_element_type=F32)
    if has_res:
        acc = acc + r_ref[...]
    o_ref[...] = acc.astype(o_ref.dtype)


def _mm(x, w, norm_g=None, residual=None, tm=1024, tn=1024, keep_pad=False, out_dtype=F32):
    M, K = x.shape
    N = w.shape[1]
    Np = _round_up(N, LANES)
    wb = w.astype(BF16)
    if Np != N:
        wb = jnp.pad(wb, ((0, 0), (0, Np - N)))
    tm = _pick_tile(M, tm)
    tn = min(Np, tn)
    while Np % tn:
        tn -= LANES
    args = [x, wb]
    in_specs = [pl.BlockSpec((tm, K), lambda i, j: (i, 0)),
                pl.BlockSpec((K, tn), lambda i, j: (0, j))]
    if norm_g is not None:
        args.append(norm_g.reshape(1, K).astype(F32))
        in_specs.append(pl.BlockSpec((1, K), lambda i, j: (0, 0)))
    if residual is not None:
        assert Np == N
        args.append(residual)
        in_specs.append(pl.BlockSpec((tm, tn), lambda i, j: (i, j)))
    out = pl.pallas_call(
        functools.partial(_mm_kernel, has_norm=norm_g is not None, has_res=residual is not None),
        grid=(M // tm, Np // tn),
        in_specs=in_specs,
        out_specs=pl.BlockSpec((tm, tn), lambda i, j: (i, j)),
        out_shape=jax.ShapeDtypeStruct((M, Np), out_dtype),
        scratch_shapes=[pltpu.VMEM((tm, K), BF16)],
        compiler_params=_cparams("parallel", "arbitrary"),
        name="mm",
    )(*args)
    return out if (Np == N or keep_pad) else out[:, :N]


def _rwkv_kernel(r_ref, lw_ref, k_ref, v_ref, kk_ref, a_ref, rk_ref, gg_ref, gb_ref, o_ref, s_ref,
                 *, chunk, heads_per_block):
    C, HB, N = chunk, heads_per_block, RW_HEAD

    @pl.when(pl.program_id(2) == 0)
    def _():
        s_ref[...] = jnp.zeros_like(s_ref)

    NP = HB // 2
    W = 2 * N
    n_chunks = r_ref.shape[1] // C
    row = lax.broadcasted_iota(jnp.int32, (C, C), 0)
    col = lax.broadcasted_iota(jnp.int32, (C, C), 1)
    incl = row >= col
    strict = row > col
    strict4 = jnp.concatenate([strict, strict, incl, incl], axis=0)
    tril_b = incl.astype(BF16)
    eye = (row == col).astype(F32)
    lane = lax.broadcasted_iota(jnp.int32, (1, W), 1)
    in_a = lane < N
    mask_a = in_a.astype(F32)
    mask_b = 1.0 - mask_a
    srow = lax.broadcasted_iota(jnp.int32, (W, W), 0)
    scol = lax.broadcasted_iota(jnp.int32, (W, W), 1)
    block_diag = ((srow < N) == (scol < N)).astype(F32)

    def halves(stacked):
        return jnp.where(in_a, stacked[:C], stacked[C:])

    def head_sum(x):
        sa = jnp.sum(x * mask_a, axis=-1, keepdims=True)
        sb = jnp.sum(x * mask_b, axis=-1, keepdims=True)
        return jnp.where(in_a, sa, sb)

    incl2 = jnp.concatenate([incl, incl], axis=0)

    def chunk_body(c, carry):
        sl = pl.ds(pl.multiple_of(c * C, C), C)
        P = range(NP)
        cols = [slice(p * W, (p + 1) * W) for p in P]
        lw = [lw_ref[0, sl, cs] for cs in cols]
        cum = [_cumsum_rows(tril_b, x) for x in lw]
        g = [jnp.exp(x) for x in cum]
        g_prev = [jnp.exp(x - y) for x, y in zip(cum, lw)]
        g_inv = [jnp.exp(-x) for x in cum]
        g_end = [x[C - 1:C, :] for x in g]
        kk = [kk_ref[0, sl, cs] for cs in cols]
        a_bar = [-x * y for x, y in zip(kk, g_prev)]
        b_til = [x * a_ref[0, sl, cs] * y for x, cs, y in zip(kk, cols, g_inv)]
        k_til = [k_ref[0, sl, cs].astype(F32) * y for cs, y in zip(cols, g_inv)]
        r_bar = [r_ref[0, sl, cs].astype(F32) * y for cs, y in zip(cols, g)]
        lhs = [jnp.concatenate([x, y], axis=0) for x, y in zip(a_bar, r_bar)]
        rhs = [jnp.concatenate([x, y], axis=0) for x, y in zip(b_til, k_til)]
        pair = [_dot_nt(jnp.concatenate([x * mask_a, x * mask_b], axis=0), y)
                for x, y in zip(lhs, rhs)]
        l_ab = [jnp.where(strict, pr[base:base + C, :C], 0.0) for pr in pair for base in (0, 2 * C)]
        t_inv = _nilpotent_inverse(l_ab, eye, C)
        on_v = [jnp.where(strict4, jnp.concatenate(
            [pr[:C, C:], pr[2 * C:3 * C, C:], pr[C:2 * C, C:], pr[3 * C:, C:]], axis=0), 0.0)
            for pr in pair]
        v = [v_ref[0, sl, cs] for cs in cols]
        from_v = [_dot(x, y) for x, y in zip(on_v, v)]
        s = [s_ref[p] for p in P]
        from_state = [_dot_nt(x, y) for x, y in zip(lhs, s)]
        u = [halves(_dot(jnp.concatenate([t_inv[2 * p], t_inv[2 * p + 1]], axis=0),
                         from_state[p][:C] + halves(from_v[p][:2 * C]))) for p in P]
        a_rb = [jnp.where(incl2, jnp.concatenate([pr[C:2 * C, :C], pr[3 * C:, :C]], axis=0), 0.0)
                for pr in pair]
        from_u = [_dot(x, y) for x, y in zip(a_rb, u)]
        upd = [_dot_tn(jnp.concatenate([v[p], u[p]], axis=0),
                       jnp.concatenate([k_til[p] * g_end[p], b_til[p] * g_end[p]], axis=0))
               for p in P]
        for p in P:
            s_ref[p] = s[p] * g_end[p] + upd[p] * block_diag
        for p in P:
            cs = cols[p]
            o = from_state[p][C:] + halves(from_v[p][2 * C:]) + halves(from_u[p])
            mean = head_sum(o) * (1.0 / N)
            cen = o - mean
            var = head_sum(cen * cen) * (1.0 / N)
            o = cen * lax.rsqrt(var + RW_GN_EPS) * gg_ref[:, cs] + gb_ref[:, cs]
            bonus = head_sum(r_ref[0, sl, cs].astype(F32) * k_ref[0, sl, cs].astype(F32)
                             * rk_ref[:, cs]) * v[p]
            o_ref[0, sl, cs] = o + bonus
        return carry

    lax.fori_loop(0, n_chunks, chunk_body, 0)


def _rwkv_recurrence(r, lw, k, v, kk, a, r_k, gn_g, gn_b, tb=256, heads_per_block=16):
    B, T, D = r.shape
    HB = heads_per_block
    W = HB * RW_HEAD
    tb = _pick_tile(T, tb)
    seq = pl.BlockSpec((1, tb, W), lambda b, h, t: (b, t, h))
    vec = pl.BlockSpec((1, W), lambda b, h, t: (0, h))
    return pl.pallas_call(
        functools.partial(_rwkv_kernel, chunk=min(RW_CHUNK, tb), heads_per_block=HB),
        grid=(B, D // W, T // tb),
        in_specs=[seq] * 6 + [vec] * 3,
        out_specs=seq,
        out_shape=jax.ShapeDtypeStruct((B, T, D), F32),
        scratch_shapes=[pltpu.VMEM((HB // 2, 2 * RW_HEAD, 2 * RW_HEAD), F32)],
        compiler_params=_cparams("parallel", "parallel", "arbitrary"),
        name="rwkv7",
    )(r, lw, k, v, kk, a, r_k.reshape(1, D), gn_g.reshape(1, D), gn_b.reshape(1, D))


def _gdn_kernel(q_ref, k_ref, v_ref, beta_ref, gcol_ref, grow_ref, ng_ref, o_ref, s_ref, *, chunk):
    C = chunk

    @pl.when(pl.program_id(1) == 0)
    def _():
        s_ref[...] = jnp.zeros_like(s_ref)

    n_chunks = q_ref.shape[1] // C
    row = lax.broadcasted_iota(jnp.int32, (C, C), 0)
    col = lax.broadcasted_iota(jnp.int32, (C, C), 1)
    incl = row >= col
    strict = row > col
    eye = (row == col).astype(F32)
    ng = ng_ref[...]

    def chunk_body(c, carry):
        sl = pl.ds(pl.multiple_of(c * C, C), C)
        beta_all = beta_ref[0, sl, :]
        gcol_all = gcol_ref[0, sl, :]
        grow_all = grow_ref[0, 0, c]
        H = range(GDN_HEADS)
        cols = [slice(h * GDN_HEAD, (h + 1) * GDN_HEAD) for h in H]
        beta = [beta_all[:, h:h + 1] for h in H]
        gcol = [gcol_all[:, h:h + 1] for h in H]
        k = [k_ref[0, sl, cs].astype(F32) for cs in cols]
        q = [q_ref[0, sl, cs].astype(F32) for cs in cols]
        kb = [x * y for x, y in zip(k, beta)]
        decay = [jnp.exp(jnp.where(incl, gcol[h] - grow_all[h:h + 1, :], -jnp.inf)) for h in H]
        pair = [_dot_nt(jnp.concatenate([kb[h], q[h]], axis=0), k[h]) for h in H]
        t_inv = _nilpotent_inverse(
            [-jnp.where(strict, pair[h][:C] * decay[h], 0.0) for h in H], eye, C)
        eg = [jnp.exp(x) for x in gcol]
        sol = [_dot(t_inv[h], jnp.concatenate(
            [v_ref[0, sl, cols[h]].astype(F32) * beta[h], kb[h] * eg[h]], axis=-1))
            for h in H]
        s = [s_ref[h] for h in H]
        from_state = [_dot(jnp.concatenate([sol[h][:, GDN_HEAD:], q[h] * eg[h]], axis=0), s[h])
                      for h in H]
        v_new = [sol[h][:, :GDN_HEAD] - from_state[h][:C] for h in H]
        intra = [_dot(pair[h][C:] * decay[h], v_new[h]) for h in H]
        g_end = [x[C - 1:C, :] for x in gcol]
        upd = [_dot_tn(k[h] * jnp.exp(g_end[h] - gcol[h]), v_new[h]) for h in H]
        for h in H:
            s_ref[h] = s[h] * jnp.exp(g_end[h]) + upd[h]
        for h in H:
            o = from_state[h][C:] + intra[h]
            o = o * lax.rsqrt(jnp.mean(o * o, axis=-1, keepdims=True) + EPS) * ng
            o_ref[0, sl, cols[h]] = o
        return carry

    lax.fori_loop(0, n_chunks, chunk_body, 0)


def _gdn_recurrence(q, k, v, beta, gcum, norm_g, tb=256):
    B, T, D = q.shape
    H, Dh = GDN_HEADS, GDN_HEAD
    tb = _pick_tile(T, tb)
    C = min(GDN_CHUNK, tb)
    grow = jnp.transpose(gcum.reshape(B, T // tb, tb // C, C, H), (0, 1, 2, 4, 3))
    seq = pl.BlockSpec((1, tb, D), lambda b, t: (b, t, 0))
    colspec = pl.BlockSpec((1, tb, H), lambda b, t: (b, t, 0))
    rowspec = pl.BlockSpec((1, 1, tb // C, H, C), lambda b, t: (b, t, 0, 0, 0))
    return pl.pallas_call(
        functools.partial(_gdn_kernel, chunk=C),
        grid=(B, T // tb),
        in_specs=[seq, seq, seq, colspec, colspec, rowspec,
                  pl.BlockSpec((1, Dh), lambda b, t: (0, 0))],
        out_specs=seq,
        out_shape=jax.ShapeDtypeStruct((B, T, D), F32),
        scratch_shapes=[pltpu.VMEM((H, Dh, Dh), F32)],
        compiler_params=_cparams("parallel", "arbitrary"),
        name="gdn",
    )(q, k, v, beta, gcum, grow, norm_g.reshape(1, Dh))


def _split(t, sizes):
    offs = []
    acc = 0
    for s in sizes[:-1]:
        acc += s
        offs.append(acc)
    return jnp.split(t, offs, axis=-1)


def _prev_rows(p, tm, rows, width):
    B, T, W = p.shape
    tail = p.reshape(B, T // tm, tm, W)[:, :-1, tm - rows:, :width].astype(F32)
    return jnp.pad(tail, ((0, 0), (1, 0), (0, 0), (0, 0)))


def _group_sumsq(x, ones_bd):
    hi, lo = _split_bf16(x * x)
    return (jnp.dot(hi, ones_bd, preferred_element_type=F32)
            + jnp.dot(lo, ones_bd, preferred_element_type=F32))


def _shift_rows(x, halo, s):
    rolled = pltpu.roll(x, s, 0)
    row = lax.broadcasted_iota(jnp.int32, (8, x.shape[1]), 0)
    top = jnp.where(row < s, pltpu.roll(halo, s, 0), rolled[:8])
    return jnp.concatenate([top, rolled[8:]], axis=0)


def _rwkv_prep_kernel(*refs, has_vres):
    D = D_MODEL
    if has_vres:
        (pa_ref, prev_ref, mu_ref, w0_ref, w2_ref, a0_ref, a2_ref, kk_ref, ka_ref,
         vf_ref, v0_ref, v1_ref, v2_ref, r_o, lw_o, k_o, v_o, kkn_o, a_o) = refs
    else:
        (pa_ref, prev_ref, mu_ref, w0_ref, w2_ref, a0_ref, a2_ref, kk_ref, ka_ref,
         r_o, lw_o, k_o, v_o, kkn_o, a_o) = refs
    x = pa_ref[0].astype(F32)
    x = x + (_shift_rows(x, prev_ref[0, 0], 1) - x) * mu_ref[...]
    r = x[:, :D]
    k = x[:, D:2 * D]
    v = x[:, 2 * D:3 * D]
    xw = x[:, 3 * D:3 * D + RW_DECAY_LORA]
    xa = x[:, 3 * D + RW_DECAY_LORA:]
    lw_o[0] = -math.exp(-0.5) * jax.nn.sigmoid(w0_ref[...] + _dot(jnp.tanh(xw), w2_ref[...]))
    a = jax.nn.sigmoid(a0_ref[...] + _dot(xa, a2_ref[...]))
    if has_vres:
        gate = jax.nn.sigmoid(v0_ref[...] + _dot(_dot(v, v1_ref[...]), v2_ref[...]))
        v = v + (vf_ref[0] - v) * gate
    r_o[0] = r.astype(r_o.dtype)
    v_o[0] = v
    a_o[0] = a
    k_o[0] = (k * (1.0 + (a - 1.0) * ka_ref[...])).astype(k_o.dtype)
    lane_r = lax.broadcasted_iota(jnp.int32, (LANES, LANES), 0) // RW_HEAD
    lane_c = lax.broadcasted_iota(jnp.int32, (LANES, LANES), 1) // RW_HEAD
    ones_bd = (lane_r == lane_c).astype(BF16)
    for j in range(D // LANES):
        cs = slice(j * LANES, (j + 1) * LANES)
        kx = k[:, cs] * kk_ref[:, cs]
        kkn_o[0, :, cs] = kx * lax.rsqrt(_group_sumsq(kx, ones_bd) + EPS)


def _rwkv_branch(p_a, v_first, mu, w0, w2, a0, a2, k_k, k_a, r_k, gn_g, gn_b, vres, tm=256):
    B, T, _ = p_a.shape
    D = D_MODEL
    tm = _pick_tile(T, tm)
    row = lambda n: pl.BlockSpec((1, n), lambda b, i: (0, 0))
    mat = lambda a, b_: pl.BlockSpec((a, b_), lambda b, i: (0, 0))
    seq = pl.BlockSpec((1, tm, D), lambda b, i: (b, i, 0))
    args = [p_a, _prev_rows(p_a, tm, 8, A_IN), mu.reshape(1, A_IN), w0.reshape(1, D), w2.astype(BF16),
            a0.reshape(1, D), a2.astype(BF16), k_k.reshape(1, D), k_a.reshape(1, D)]
    in_specs = [pl.BlockSpec((1, tm, A_IN), lambda b, i: (b, i, 0)),
                pl.BlockSpec((1, 1, 8, A_IN), lambda b, i: (b, i, 0, 0)),
                row(A_IN), row(D), mat(RW_DECAY_LORA, D), row(D), mat(RW_AAA_LORA, D), row(D), row(D)]
    if vres is not None:
        v0, v1, v2 = vres
        pad = LANES - RW_MV_LORA
        args += [v_first, v0.reshape(1, D), jnp.pad(v1, ((0, 0), (0, pad))).astype(BF16),
                 jnp.pad(v2, ((0, pad), (0, 0))).astype(BF16)]
        in_specs += [seq, row(D), mat(D, LANES), mat(LANES, D)]
    r, lw, k, v, kk, a = pl.pallas_call(
        functools.partial(_rwkv_prep_kernel, has_vres=vres is not None),
        grid=(B, T // tm),
        in_specs=in_specs,
        out_specs=[seq] * 6,
        out_shape=[jax.ShapeDtypeStruct((B, T, D), dt) for dt in (BF16, F32, BF16, F32, F32, F32)],
        compiler_params=_cparams("parallel", "parallel"),
        name="rwkv_prep",
    )(*args)
    if vres is None:
        v_first = v
    return _rwkv_recurrence(r, lw, k, v, kk, a, r_k, gn_g, gn_b), v_first


def _gdn_prep_kernel(pb_ref, halo_ref, cw_ref, q_o, k_o, v_o):
    D = D_MODEL
    x = pb_ref[0].astype(F32)
    halo = halo_ref[0, 0]
    acc = x * cw_ref[GDN_CONV - 1:GDN_CONV, :]
    for s in range(1, GDN_CONV):
        acc = acc + _shift_rows(x, halo, s) * cw_ref[GDN_CONV - 1 - s:GDN_CONV - s, :]
    y = acc * jax.nn.sigmoid(acc)
    v_o[0] = y[:, 2 * D:].astype(BF16)
    ones = jnp.ones((LANES, LANES), BF16)
    for j in range(D // GDN_HEAD):
        cs = slice(j * GDN_HEAD, (j + 1) * GDN_HEAD)
        q = y[:, cs]
        k = y[:, D + j * GDN_HEAD:D + (j + 1) * GDN_HEAD]
        q_o[0, :, cs] = (q * (lax.rsqrt(_group_sumsq(q, ones) + EPS) * GDN_HEAD ** -0.5)).astype(BF16)
        k_o[0, :, cs] = (k * lax.rsqrt(_group_sumsq(k, ones) + EPS)).astype(BF16)


def _gdn_branch(p_b, conv_w, A_log, dt_bias, norm_g, tm=256):
    B, T, _ = p_b.shape
    D, H, C = D_MODEL, GDN_HEADS, GDN_CHUNK
    tm = _pick_tile(T, tm)
    seq = pl.BlockSpec((1, tm, D), lambda b, i: (b, i, 0))
    q, k, v = pl.pallas_call(
        _gdn_prep_kernel,
        grid=(B, T // tm),
        in_specs=[pl.BlockSpec((1, tm, 3 * D), lambda b, i: (b, i, 0)),
                  pl.BlockSpec((1, 1, 8, 3 * D), lambda b, i: (b, i, 0, 0)),
                  pl.BlockSpec((GDN_CONV, 3 * D), lambda b, i: (0, 0))],
        out_specs=[seq] * 3,
        out_shape=[jax.ShapeDtypeStruct((B, T, D), BF16)] * 3,
        compiler_params=_cparams("parallel", "parallel"),
        name="gdn_prep",
    )(p_b, _prev_rows(p_b, tm, 8, 3 * D), conv_w)
    a_in = p_b[..., 3 * D:3 * D + H].astype(F32)
    b_in = p_b[..., 3 * D + H:3 * D + 2 * H].astype(F32)
    beta = jax.nn.sigmoid(b_in)
    g = -jnp.exp(A_log) * jax.nn.softplus(a_in + dt_bias)
    Cc = min(C, T)
    gcum = jnp.cumsum(g.reshape(B, T // Cc, Cc, H), axis=2).reshape(B, T, H)
    return _gdn_recurrence(q, k, v, beta, gcum, norm_g)


def _dsa_t_kernel(cqt_ref, wit_ref, ki_ref, ckv_ref, ckvt_ref, wqit_ref, wuqt_ref, wuk_ref, wuv_ref,
                  o_ref, key_scr, acc_scr, qlat_scr, *, n_sel, tq, tk, pos_bits):
    H = IDX_HEADS
    q0 = pl.program_id(1) * tq
    nk = (q0 + tq + tk - 1) // tk
    int_min = jnp.int32(-2 ** 31)

    cqt = cqt_ref[0]
    qit = jnp.dot(wqit_ref[...], cqt, preferred_element_type=F32)
    qi_cat = jnp.concatenate([qit[h * IDX_HEAD:(h + 1) * IDX_HEAD] for h in range(H)],
                             axis=1).astype(BF16)
    wit = wit_ref[0]
    wi_cat = jnp.concatenate([wit[h:h + 1] for h in range(H)], axis=1)
    key_off = lax.broadcasted_iota(jnp.int32, (tk, tq), 0)
    q_pos = q0 + lax.broadcasted_iota(jnp.int32, (tk, tq), 1)

    def score_tile(kt, carry):
        ki = ki_ref[0, pl.ds(pl.multiple_of(kt * tk, tk), tk), :]
        z = jnp.maximum(jnp.dot(ki, qi_cat, preferred_element_type=F32), 0.0) * wi_cat
        s = z[:, :tq]
        for h in range(1, H):
            s = s + z[:, h * tq:(h + 1) * tq]
        s = jnp.where(s == 0.0, 0.0, s)
        s = jnp.where(kt * tk + key_off <= q_pos, s, -jnp.inf)
        bits = pltpu.bitcast(s, jnp.int32)
        key_scr[kt] = bits ^ ((bits >> 31) & jnp.int32(0x7FFFFFFF))
        return carry

    lax.fori_loop(0, nk, score_tile, 0)

    def count(pred):
        ways = 4
        def body(kt, acc):
            hit = jnp.where(pred(key_scr[kt], kt * tk + key_off), 1.0, 0.0)
            return acc + jnp.sum(hit.reshape(tk // (8 * ways), ways * 8, tq), axis=0)
        acc = lax.fori_loop(0, nk, body, jnp.zeros((ways * 8, tq), F32))
        return jnp.sum(acc, axis=0, keepdims=True)

    want = jnp.float32(n_sel)
    thr = jnp.where(count(lambda key, pos: key >= 0) >= want, jnp.int32(0), int_min)

    def thr_bit(i, thr):
        cand = thr | jnp.left_shift(jnp.int32(1), 30 - i)
        return jnp.where(count(lambda key, pos: key >= cand) >= want, cand, thr)

    thr = lax.fori_loop(0, 31, thr_bit, thr)

    query_pos = q0 + lax.broadcasted_iota(jnp.int32, (1, tq), 1)
    tied = jnp.logical_and(count(lambda key, pos: key >= thr) > want, query_pos >= n_sel)
    any_tied = jnp.max(jnp.where(tied, 1.0, 0.0)) > 0.0

    def index_cut():
        need = want - count(lambda key, pos: key > thr)

        def pos_bit(i, last):
            cand = last + jnp.left_shift(jnp.int32(1), pos_bits - 1 - i)
            below = count(lambda key, pos: jnp.logical_and(key == thr, pos < cand))
            return jnp.where(below < need, cand, last)

        return lax.fori_loop(0, pos_bits, pos_bit, jnp.zeros((1, tq), jnp.int32))

    last = lax.cond(any_tied, index_cut, lambda: jnp.full((1, tq), 2 ** pos_bits, jnp.int32))

    qt = jnp.dot(wuqt_ref[...], cqt, preferred_element_type=F32)
    q_lat = jnp.concatenate(
        [_dot(wuk_ref[h], qt[h * MLA_HEAD:(h + 1) * MLA_HEAD]) for h in range(MLA_HEADS)],
        axis=1) * MLA_HEAD ** -0.5
    qlat_scr[...] = q_lat.astype(BF16)
    acc_scr[...] = jnp.zeros(acc_scr.shape, F32)
    HG = 2
    GW = HG * tq
    groups = [slice(g * GW, (g + 1) * GW) for g in range(MLA_HEADS // HG)]

    def attend_tile(kt, carry):
        m_old, l_old = carry
        ckv = ckv_ref[0, pl.ds(pl.multiple_of(kt * tk, tk), tk), :]
        ckv_t = ckvt_ref[0, kt]
        key = key_scr[kt]
        pos = kt * tk + key_off
        sel = jnp.logical_or(key > thr, jnp.logical_and(key == thr, pos <= last))
        sel = jnp.logical_and(sel, pos <= q_pos)
        bias = jnp.where(sel, 0.0, NEG_BIG)
        bias = jnp.concatenate([bias] * HG, axis=1)
        logits = [jnp.dot(ckv, qlat_scr[:, g], preferred_element_type=F32) for g in groups]
        m_out, l_out = [], []
        for g, lg in zip(groups, logits):
            lg = lg + bias
            m_new = jnp.maximum(m_old[:, g], jnp.max(lg, axis=0, keepdims=True))
            p = jnp.exp(lg - m_new)
            alpha = jnp.exp(m_old[:, g] - m_new)
            l_out.append(alpha * l_old[:, g] + jnp.sum(p, axis=0, keepdims=True))
            acc_scr[:, g] = alpha * acc_scr[:, g] + jnp.dot(
                ckv_t, p.astype(BF16), preferred_element_type=F32)
            m_out.append(m_new)
        return jnp.concatenate(m_out, axis=1), jnp.concatenate(l_out, axis=1)

    width = MLA_HEADS * tq
    _, l_fin = lax.fori_loop(0, nk, attend_tile,
                             (jnp.full((1, width), NEG_BIG, F32), jnp.zeros((1, width), F32)))
    o_lat = acc_scr[...] / l_fin
    o_ref[0] = jnp.concatenate(
        [_dot_tn(o_lat[:, h * tq:(h + 1) * tq], wuv_ref[h]) for h in range(MLA_HEADS)],
        axis=1)


def _dsa_attention_t(c_q, w_i, k_i, c_kv, w_qi, w_uq, w_uk, w_uv, tq=128, tk=512):
    B, T, _ = c_q.shape
    tq = _pick_tile(T, tq)
    tk = _pick_tile(T, tk)
    n_sel = min(TOPK_MAX, T // 4)
    assert tq % LANES == 0 and tk >= n_sel and tk % tq == 0
    H = MLA_HEADS
    full = lambda *shape: pl.BlockSpec(shape, lambda b, i: (0,) * len(shape))
    c_qt = jnp.transpose(c_q, (0, 2, 1))
    w_it = jnp.transpose(w_i, (0, 2, 1))
    c_kvt = jnp.transpose(c_kv.reshape(B, T // tk, tk, MLA_KV_RANK), (0, 1, 3, 2))
    return pl.pallas_call(
        functools.partial(_dsa_t_kernel, n_sel=n_sel, tq=tq, tk=tk,
                          pos_bits=max(1, (T - 1).bit_length())),
        grid=(B, T // tq),
        in_specs=[pl.BlockSpec((1, MLA_Q_RANK, tq), lambda b, i: (b, 0, i)),
                  pl.BlockSpec((1, IDX_HEADS, tq), lambda b, i: (b, 0, i)),
                  pl.BlockSpec((1, T, IDX_HEAD), lambda b, i: (b, 0, 0)),
                  pl.BlockSpec((1, T, MLA_KV_RANK), lambda b, i: (b, 0, 0)),
                  pl.BlockSpec((1, T // tk, MLA_KV_RANK, tk), lambda b, i: (b, 0, 0, 0)),
                  full(IDX_HEADS * IDX_HEAD, MLA_Q_RANK),
                  full(H * MLA_HEAD, MLA_Q_RANK),
                  full(H, MLA_KV_RANK, MLA_HEAD),
                  full(H, MLA_KV_RANK, MLA_HEAD)],
        out_specs=pl.BlockSpec((1, tq, H * MLA_HEAD), lambda b, i: (b, i, 0)),
        out_shape=jax.ShapeDtypeStruct((B, T, H * MLA_HEAD), F32),
        scratch_shapes=[pltpu.VMEM((T // tk, tk, tq), jnp.int32),
                        pltpu.VMEM((MLA_KV_RANK, H * tq), F32),
                        pltpu.VMEM((MLA_KV_RANK, H * tq), BF16)],
        compiler_params=_cparams("parallel", "arbitrary"),
        name="dsa",
    )(c_qt, w_it, k_i, c_kv, c_kvt, w_qi.T.astype(BF16), w_uq.T.astype(BF16),
      jnp.transpose(w_uk, (1, 0, 2)).astype(BF16), jnp.transpose(w_uv, (1, 0, 2)).astype(BF16))


def _rms(x, g):
    return x * lax.rsqrt(jnp.mean(x * x, axis=-1, keepdims=True) + EPS) * g


def _dsa_prep_kernel(pc_ref, qn_ref, kvn_ref, kig_ref, kib_ref, cq_o, ckv_o, ki_o, wi_o):
    x = pc_ref[...].astype(F32)
    cq_o[...] = _rms(x[:, :MLA_Q_RANK], qn_ref[...]).astype(BF16)
    ckv_o[...] = _rms(x[:, MLA_Q_RANK:MLA_Q_RANK + MLA_KV_RANK], kvn_ref[...]).astype(BF16)
    off = MLA_Q_RANK + MLA_KV_RANK
    ki = x[:, off:off + IDX_HEAD]
    mu = jnp.mean(ki, axis=-1, keepdims=True)
    var = jnp.mean(jnp.square(ki - mu), axis=-1, keepdims=True)
    ki_o[...] = ((ki - mu) * lax.rsqrt(var + EPS) * kig_ref[...] + kib_ref[...]).astype(BF16)
    wi_o[...] = x[:, off + IDX_HEAD:off + IDX_HEAD + IDX_HEADS] * (IDX_HEADS ** -0.5 * IDX_HEAD ** -0.5)


def _dsa_branch(p_c, q_norm, w_uq, kv_norm, w_uk, w_uv, w_qi, ki_g, ki_b, tm=512):
    B, T, W = p_c.shape
    M = B * T
    tm = _pick_tile(M, tm)
    row = lambda n: pl.BlockSpec((1, n), lambda i: (0, 0))
    out = lambda n: pl.BlockSpec((tm, n), lambda i: (i, 0))
    c_q, c_kv, k_i, w_i = pl.pallas_call(
        _dsa_prep_kernel,
        grid=(M // tm,),
        in_specs=[pl.BlockSpec((tm, W), lambda i: (i, 0)), row(MLA_Q_RANK), row(MLA_KV_RANK),
                  row(IDX_HEAD), row(IDX_HEAD)],
        out_specs=[out(MLA_Q_RANK), out(MLA_KV_RANK), out(IDX_HEAD), out(IDX_HEADS)],
        out_shape=[jax.ShapeDtypeStruct((M, MLA_Q_RANK), BF16),
                   jax.ShapeDtypeStruct((M, MLA_KV_RANK), BF16),
                   jax.ShapeDtypeStruct((M, IDX_HEAD), BF16),
                   jax.ShapeDtypeStruct((M, IDX_HEADS), F32)],
        compiler_params=_cparams("parallel"),
        name="dsa_prep",
    )(p_c.reshape(M, W), q_norm.reshape(1, -1), kv_norm.reshape(1, -1), ki_g.reshape(1, -1),
      ki_b.reshape(1, -1))
    return _dsa_attention_t(c_q.reshape(B, T, -1), w_i.reshape(B, T, -1), k_i.reshape(B, T, -1),
                            c_kv.reshape(B, T, -1), w_qi, w_uq, w_uk, w_uv)


def _mix_kernel(ya_ref, yb_ref, yc_ref, ga_ref, gb_ref, gc_ref, w_ref, x_ref, o_ref):
    gate = lambda ref: jax.nn.sigmoid(ref[...].astype(F32))
    mix = gate(ga_ref) * ya_ref[...] + gate(gb_ref) * yb_ref[...] + gate(gc_ref) * yc_ref[...]
    o_ref[...] = x_ref[...] + jnp.dot(mix.astype(BF16), w_ref[...], preferred_element_type=F32)


def _mix_layer(x, y_a, y_b, y_c, p_g, w_out, tm=512):
    M, D = x.shape
    tm = _pick_tile(M, tm)
    tile = pl.BlockSpec((tm, D), lambda i: (i, 0))
    gate = lambda c: pl.BlockSpec((tm, D), lambda i: (i, c))
    return pl.pallas_call(
        _mix_kernel,
        grid=(M // tm,),
        in_specs=[tile, tile, tile, gate(0), gate(1), gate(2),
                  pl.BlockSpec((D, D), lambda i: (0, 0)), tile],
        out_specs=tile,
        out_shape=jax.ShapeDtypeStruct((M, D), F32),
        compiler_params=_cparams("parallel"),
        name="mix",
    )(y_a, y_b, y_c, p_g, p_g, p_g, w_out.astype(BF16), x)


def _xattn_kernel(x_ref, g_ref, wq_ref, k_ref, v_ref, wo_ref, o_ref):
    x = x_ref[0]
    h = x * lax.rsqrt(jnp.mean(x * x, axis=-1, keepdims=True) + EPS) * g_ref[...]
    q = jnp.dot(h.astype(BF16), wq_ref[...], preferred_element_type=F32)
    k = k_ref[0]
    v = v_ref[0]
    outs = []
    for hd in range(XA_HEADS):
        cs = slice(hd * XA_HEAD, (hd + 1) * XA_HEAD)
        logits = _dot_nt(q[:, cs], k[:, cs]) * XA_HEAD ** -0.5
        p = jnp.exp(logits - jnp.max(logits, axis=-1, keepdims=True))
        outs.append(_dot(p, v[:, cs]) / jnp.sum(p, axis=-1, keepdims=True))
    o = jnp.concatenate(outs, axis=-1)
    o_ref[0] = x + jnp.dot(o.astype(BF16), wo_ref[...], preferred_element_type=F32)


def _xattn_layer(x, mem, mem_norm, norm_g, w_q, w_kv, w_o, tq=512):
    B, T, D = x.shape
    Mm = mem.shape[1]
    kv = _mm(mem.reshape(B * Mm, D), w_kv, norm_g=mem_norm).reshape(B, Mm, 2 * D)
    tq = _pick_tile(T, tq)
    tile = pl.BlockSpec((1, tq, D), lambda b, i: (b, i, 0))
    wspec = pl.BlockSpec((D, D), lambda b, i: (0, 0))
    return pl.pallas_call(
        _xattn_kernel,
        grid=(B, T // tq),
        in_specs=[tile, pl.BlockSpec((1, D), lambda b, i: (0, 0)), wspec,
                  pl.BlockSpec((1, Mm, D), lambda b, i: (b, 0, 0)),
                  pl.BlockSpec((1, Mm, D), lambda b, i: (b, 0, 1)), wspec],
        out_specs=tile,
        out_shape=jax.ShapeDtypeStruct((B, T, D), F32),
        compiler_params=_cparams("parallel", "parallel"),
        name="xattn",
    )(x, norm_g.reshape(1, D), w_q.astype(BF16), kv, kv, w_o.astype(BF16))


def _ffn_kernel(x_ref, g_ref, wg_ref, wu_ref, wd_ref, o_ref, h_scr, acc_scr):
    j = pl.program_id(1)

    @pl.when(j == 0)
    def _():
        x = x_ref[...]
        h = x * lax.rsqrt(jnp.mean(x * x, axis=-1, keepdims=True) + EPS) * g_ref[...]
        h_scr[...] = h.astype(BF16)
        acc_scr[...] = jnp.zeros_like(acc_scr)

    h = h_scr[...]
    gate = jnp.dot(h, wg_ref[...], preferred_element_type=F32)
    up = jnp.dot(h, wu_ref[...], preferred_element_type=F32)
    act = gate * jax.nn.sigmoid(gate) * up
    acc_scr[...] += jnp.dot(act.astype(BF16), wd_ref[...], preferred_element_type=F32)

    @pl.when(j == pl.num_programs(1) - 1)
    def _():
        o_ref[...] = x_ref[...] + acc_scr[...]


def _ffn_tiles(M, F, tm, tf):
    tm = _pick_tile(M, tm)
    tf = min(F, tf)
    while F % tf or tf % LANES:
        tf -= LANES
    return tm, tf


def _ffn_layer(x, norm_g, w_gate, w_up, w_down, tm=512, tf=1408):
    B, T, D = x.shape
    M = B * T
    F = w_gate.shape[-1]
    tm, tf = _ffn_tiles(M, F, tm, tf)
    tile = pl.BlockSpec((tm, D), lambda i, j: (i, 0))
    out = pl.pallas_call(
        _ffn_kernel,
        grid=(M // tm, F // tf),
        in_specs=[tile, pl.BlockSpec((1, D), lambda i, j: (0, 0)),
                  pl.BlockSpec((D, tf), lambda i, j: (0, j)),
                  pl.BlockSpec((D, tf), lambda i, j: (0, j)),
                  pl.BlockSpec((tf, D), lambda i, j: (j, 0))],
        out_specs=tile,
        out_shape=jax.ShapeDtypeStruct((M, D), F32),
        scratch_shapes=[pltpu.VMEM((tm, D), BF16), pltpu.VMEM((tm, D), F32)],
        compiler_params=_cparams("parallel", "arbitrary"),
        name="ffn",
    )(x.reshape(M, D), norm_g.reshape(1, D), w_gate.astype(BF16), w_up.astype(BF16), w_down.astype(BF16))
    return out.reshape(B, T, D)


def _router_kernel(x_ref, g_ref, w_ref, h_o, route_o):
    x = x_ref[...]
    h = x * lax.rsqrt(jnp.mean(x * x, axis=-1, keepdims=True) + EPS) * g_ref[...]
    h_o[...] = h.astype(BF16)
    logits = _dot_hi(h, w_ref[...])
    lane = lax.broadcasted_iota(jnp.int32, logits.shape, 1).astype(F32)
    logits = jnp.where(lane < N_EXPERTS, logits, -jnp.inf)
    top = []
    for _ in range(TOP_K):
        m = jnp.max(logits, axis=-1, keepdims=True)
        idx = jnp.min(jnp.where(logits == m, lane, float(LANES)), axis=-1, keepdims=True)
        top.append((m, idx))
        logits = jnp.where(lane == idx, -jnp.inf, logits)
    m0 = top[0][0]
    ex = [jnp.exp(m - m0) for m, _ in top]
    denom = sum(ex)
    route = jnp.zeros_like(logits)
    for k, ((m, idx), e) in enumerate(zip(top, ex)):
        route = route + jnp.where(lane == k, idx, 0.0) + jnp.where(lane == TOP_K + k, e / denom, 0.0)
    route_o[...] = route


def _grouped_ffn_kernel(te_ref, nt_ref, x_ref, gate_ref, wg_ref, wu_ref, wd_ref, o_ref, acc_scr):
    i, j = pl.program_id(0), pl.program_id(1)

    @pl.when(j == 0)
    def _():
        acc_scr[...] = jnp.zeros_like(acc_scr)

    @pl.when(i < nt_ref[0])
    def _():
        h = x_ref[...]
        gate = jnp.dot(h, wg_ref[0], preferred_element_type=F32)
        up = jnp.dot(h, wu_ref[0], preferred_element_type=F32)
        act = gate * jax.nn.sigmoid(gate) * up
        acc_scr[...] += jnp.dot(act.astype(BF16), wd_ref[0], preferred_element_type=F32)

    @pl.when(j == pl.num_programs(1) - 1)
    def _():
        o_ref[...] = acc_scr[...] * gate_ref[...]


def _moe_layer(x, norm_g, router, w_gate, w_up, w_down, tm=512, tf=1792):
    B, T, D = x.shape
    M = B * T
    E, _, F = w_gate.shape
    tm, tf = _ffn_tiles(M, F, tm, tf)
    x2 = x.reshape(M, D)
    h, route = pl.pallas_call(
        _router_kernel,
        grid=(M // tm,),
        in_specs=[pl.BlockSpec((tm, D), lambda i: (i, 0)), pl.BlockSpec((1, D), lambda i: (0, 0)),
                  pl.BlockSpec((D, LANES), lambda i: (0, 0))],
        out_specs=[pl.BlockSpec((tm, D), lambda i: (i, 0)), pl.BlockSpec((tm, LANES), lambda i: (i, 0))],
        out_shape=[jax.ShapeDtypeStruct((M, D), BF16), jax.ShapeDtypeStruct((M, LANES), F32)],
        compiler_params=_cparams("parallel"),
        name="router",
    )(x2, norm_g.reshape(1, D), jnp.pad(router, ((0, 0), (0, LANES - E))))

    n_asg = TOP_K * M
    eid = route[:, :TOP_K].astype(jnp.int32).T.reshape(n_asg)
    prob = route[:, TOP_K:2 * TOP_K].T.reshape(n_asg)
    order = jnp.argsort(eid, stable=True)
    eid_s = eid[order]
    counts = jnp.sum(eid[None, :] == jnp.arange(E, dtype=jnp.int32)[:, None], axis=1).astype(jnp.int32)
    tiles_per = (counts + tm - 1) // tm
    tiles_end = jnp.cumsum(tiles_per)
    row0 = (tiles_end - tiles_per) * tm
    first = jnp.cumsum(counts) - counts
    rows = row0[eid_s] + jnp.arange(n_asg, dtype=jnp.int32) - first[eid_s]
    row_of = rows[jnp.argsort(order)]
    n_rows = n_asg + E * tm
    n_tiles = n_rows // tm
    tile_expert = jnp.minimum(
        jnp.searchsorted(tiles_end, jnp.arange(n_tiles, dtype=jnp.int32), side="right"),
        E - 1).astype(jnp.int32)
    used_tiles = tiles_end[-1:].astype(jnp.int32)
    row_expert = jnp.repeat(tile_expert, tm)
    rank = jnp.arange(n_rows, dtype=jnp.int32) - row0[row_expert]
    live = rank < counts[row_expert]
    slot = jnp.clip(first[row_expert] + rank, 0, n_asg - 1)
    src = jnp.where(live, (order % M).astype(jnp.int32)[slot], 0)
    row_gate = jnp.where(live, prob[order][slot], 0.0)

    ys = pl.pallas_call(
        _grouped_ffn_kernel,
        grid_spec=pltpu.PrefetchScalarGridSpec(
            num_scalar_prefetch=2,
            grid=(n_tiles, F // tf),
            in_specs=[pl.BlockSpec((tm, D), lambda i, j, te, nt: (i, 0)),
                      pl.BlockSpec((tm, 1), lambda i, j, te, nt: (i, 0)),
                      pl.BlockSpec((1, D, tf), lambda i, j, te, nt: (te[i], 0, j)),
                      pl.BlockSpec((1, D, tf), lambda i, j, te, nt: (te[i], 0, j)),
                      pl.BlockSpec((1, tf, D), lambda i, j, te, nt: (te[i], j, 0))],
            out_specs=pl.BlockSpec((tm, D), lambda i, j, te, nt: (i, 0)),
            scratch_shapes=[pltpu.VMEM((tm, D), F32)]),
        out_shape=jax.ShapeDtypeStruct((n_rows, D), F32),
        compiler_params=_cparams("parallel", "arbitrary"),
        name="moe",
    )(tile_expert, used_tiles, h[src], row_gate[:, None], w_gate.astype(BF16), w_up.astype(BF16),
      w_down.astype(BF16))
    out = x2
    for k in range(TOP_K):
        out = out + ys[row_of[k * M:(k + 1) * M]]
    return out.reshape(B, T, D)


def _final_norm_kernel(x_ref, g_ref, o_ref):
    x = x_ref[...]
    o_ref[...] = x * lax.rsqrt(jnp.mean(x * x, axis=-1, keepdims=True) + EPS) * g_ref[...]


def _final_norm(x, g, tm=1024):
    M, D = x.shape
    tm = _pick_tile(M, tm)
    return pl.pallas_call(
        _final_norm_kernel,
        grid=(M // tm,),
        in_specs=[pl.BlockSpec((tm, D), lambda i: (i, 0)), pl.BlockSpec((1, D), lambda i: (0, 0))],
        out_specs=pl.BlockSpec((tm, D), lambda i: (i, 0)),
        out_shape=jax.ShapeDtypeStruct((M, D), F32),
        compiler_params=_cparams("parallel"),
        name="final_norm",
    )(x, g.reshape(1, D))


def kernel(x, mem, norm_mix, w_in, rw_mu, rw_w0, rw_w2, rw_a0, rw_a2, rw_k_k, rw_k_a, rw_r_k,
           rw_gn_g, rw_gn_b, rw_v0, rw_v1, rw_v2, gdn_conv, gdn_A_log, gdn_dt_bias, gdn_norm_g,
           mla_q_norm, mla_w_uq, mla_kv_norm, mla_w_uk, mla_w_uv, idx_w_q, idx_k_g, idx_k_b,
           w_mix_out, mem_norm, norm_xattn, xa_w_q, xa_w_kv, xa_w_o, norm_ffn, ffn_w_gate,
           ffn_w_up, ffn_w_down, moe_router, moe_w_gate, moe_w_up, moe_w_down, final_norm):
    B, T, D = x.shape
    M = B * T
    depth = w_in.shape[0]
    v_first = None
    col0 = (0, A_IN, A_IN + B_IN, A_IN + B_IN + C_IN)
    widths = (A_IN, B_IN, C_IN, G_IN)
    for l in range(depth):
        x2 = x.reshape(M, D)
        p_a, p_b, p_c, p_g = (
            _mm(x2, w_in[l][:, c:c + w], norm_g=norm_mix[l], keep_pad=True, out_dtype=BF16)
            for c, w in zip(col0, widths))
        vres = None if l == 0 else (rw_v0[l - 1], rw_v1[l - 1], rw_v2[l - 1])
        y_a, v_first = _rwkv_branch(p_a.reshape(B, T, -1), v_first, rw_mu[l], rw_w0[l], rw_w2[l],
                                    rw_a0[l], rw_a2[l], rw_k_k[l], rw_k_a[l], rw_r_k[l],
                                    rw_gn_g[l], rw_gn_b[l], vres)
        y_b = _gdn_branch(p_b.reshape(B, T, -1), gdn_conv[l], gdn_A_log[l], gdn_dt_bias[l],
                          gdn_norm_g[l])
        y_c = _dsa_branch(p_c.reshape(B, T, -1), mla_q_norm[l], mla_w_uq[l], mla_kv_norm[l],
                          mla_w_uk[l], mla_w_uv[l], idx_w_q[l], idx_k_g[l], idx_k_b[l])
        x = _mix_layer(x2, y_a.reshape(M, D), y_b.reshape(M, D), y_c.reshape(M, D), p_g,
                       w_mix_out[l]).reshape(B, T, D)
        x = _xattn_layer(x, mem, mem_norm, norm_xattn[l], xa_w_q[l], xa_w_kv[l], xa_w_o[l])
        i = l // 2
        if l % 2 == 0:
            x = _ffn_layer(x, norm_ffn[l], ffn_w_gate[i], ffn_w_up[i], ffn_w_down[i])
        else:
            x = _moe_layer(x, norm_ffn[l], moe_router[i], moe_w_gate[i], moe_w_up[i], moe_w_down[i])
    return _final_norm(x.reshape(M, D), final_norm).reshape(B, T, D)
```

```python
import functools
import math

import jax
import jax.numpy as jnp
from jax import lax
from jax.experimental import pallas as pl
from jax.experimental.pallas import tpu as pltpu

F32 = jnp.float32
BF16 = jnp.bfloat16
HIGHEST = lax.Precision.HIGHEST

D_MODEL = 1024
EPS = 1e-6
LANES = 128
VMEM_LIMIT = 48 * 1024 * 1024

RW_HEAD = 64
RW_HEADS = D_MODEL // RW_HEAD
RW_DECAY_LORA = 64
RW_AAA_LORA = 64
RW_MV_LORA = 32
RW_GN_EPS = 64e-5
RW_CHUNK = 64

GDN_HEAD = 128
GDN_HEADS = D_MODEL // GDN_HEAD
GDN_CONV = 4
GDN_CHUNK = 64

MLA_HEADS = 8
MLA_HEAD = D_MODEL // MLA_HEADS
MLA_Q_RANK = 256
MLA_KV_RANK = 256
IDX_HEADS = 8
IDX_HEAD = 64
TOPK_MAX = 256

XA_HEADS = 4
XA_HEAD = D_MODEL // XA_HEADS

N_EXPERTS = 8
TOP_K = 2

A_SPLITS = (D_MODEL, D_MODEL, D_MODEL, RW_DECAY_LORA, RW_AAA_LORA)
B_SPLITS = (3 * D_MODEL, GDN_HEADS, GDN_HEADS)
C_SPLITS = (MLA_Q_RANK, MLA_KV_RANK, IDX_HEAD, IDX_HEADS)
A_IN = sum(A_SPLITS)
B_IN = sum(B_SPLITS)
C_IN = sum(C_SPLITS)
G_IN = 3 * D_MODEL

NEG_BIG = -1e30


def _cparams(*sem):
    return pltpu.CompilerParams(dimension_semantics=sem, vmem_limit_bytes=VMEM_LIMIT)


def _round_up(n, m):
    return (n + m - 1) // m * m


def _pick_tile(n, pref):
    t = min(n, pref)
    while n % t:
        t -= 8
    return t


def _dot(a, b):
    return jnp.dot(a.astype(BF16), b.astype(BF16), preferred_element_type=F32)


def _dot_nt(a, b):
    return lax.dot_general(a.astype(BF16), b.astype(BF16), (((1,), (1,)), ((), ())),
                           preferred_element_type=F32)


def _dot_tn(a, b):
    return lax.dot_general(a.astype(BF16), b.astype(BF16), (((0,), (0,)), ((), ())),
                           preferred_element_type=F32)


def _dot_hi(a, b):
    return jnp.dot(a, b, preferred_element_type=F32, precision=HIGHEST)


def _split_bf16(a):
    hi = a.astype(BF16)
    return hi, (a - hi.astype(F32)).astype(BF16)


def _dot_split(ah, al, bh, bl):
    n = ah.shape[0]
    top = jnp.dot(jnp.concatenate([ah, al], axis=0), bh, preferred_element_type=F32)
    return top[:n] + top[n:] + jnp.dot(ah, bl, preferred_element_type=F32)


def _nilpotent_inverse(ms, eye, size):
    row = lax.broadcasted_iota(jnp.int32, (size, size), 0)
    col = lax.broadcasted_iota(jnp.int32, (size, size), 1)
    base = 4
    in_block = row // base == col // base
    ds = [jnp.where(in_block, m, 0.0) for m in ms]
    dsplit = [_split_bf16(d) for d in ds]
    d2 = [_dot_split(h, l, h, l) for h, l in dsplit]
    xs = [eye + d for d in ds]
    xs = [x + _dot_split(*_split_bf16(x), *_split_bf16(p)) for x, p in zip(xs, d2)]
    b = base
    while b < size:
        below = jnp.logical_and(row // (2 * b) == col // (2 * b), row // b != col // b)
        es = [_split_bf16(jnp.where(below, m, 0.0)) for m in ms]
        xsplit = [_split_bf16(x) for x in xs]
        ys = [_dot_split(eh, el, xh, xl) for (eh, el), (xh, xl) in zip(es, xsplit)]
        xs = [x + _dot_split(xh, xl, *_split_bf16(y)) for x, (xh, xl), y in zip(xs, xsplit, ys)]
        b *= 2
    return xs


def _cumsum_rows(tril_bf16, x):
    hi = x.astype(BF16)
    r1 = x - hi.astype(F32)
    mid = r1.astype(BF16)
    lo = (r1 - mid.astype(F32)).astype(BF16)
    n = x.shape[1]
    parts = jnp.dot(tril_bf16, jnp.concatenate([hi, mid, lo], axis=1), preferred_element_type=F32)
    return parts[:, :n] + (parts[:, n:2 * n] + parts[:, 2 * n:])


def _mm_kernel(*refs, has_norm, has_res):
    it = iter(refs)
    x_ref = next(it)
    w_ref = next(it)
    g_ref = next(it) if has_norm else None
    r_ref = next(it) if has_res else None
    o_ref = next(it)
    xn_ref = next(it)

    @pl.when(pl.program_id(1) == 0)
    def _():
        x = x_ref[...].astype(F32)
        if has_norm:
            x = x * lax.rsqrt(jnp.mean(x * x, axis=-1, keepdims=True) + EPS) * g_ref[...]
        xn_ref[...] = x.astype(BF16)

    acc = jnp.dot(xn_ref[...], w_ref[...], preferred_element_type=F32)
    if has_res:
        acc = acc + r_ref[...]
    o_ref[...] = acc.astype(o_ref.dtype)


def _mm(x, w, norm_g=None, residual=None, tm=1024, tn=1024, keep_pad=False, out_dtype=F32):
    M, K = x.shape
    N = w.shape[1]
    Np = _round_up(N, LANES)
    wb = w.astype(BF16)
    if Np != N:
        wb = jnp.pad(wb, ((0, 0), (0, Np - N)))
    tm = _pick_tile(M, tm)
    tn = min(Np, tn)
    while Np % tn:
        tn -= LANES
    args = [x, wb]
    in_specs = [pl.BlockSpec((tm, K), lambda i, j: (i, 0)),
                pl.BlockSpec((K, tn), lambda i, j: (0, j))]
    if norm_g is not None:
        args.append(norm_g.reshape(1, K).astype(F32))
        in_specs.append(pl.BlockSpec((1, K), lambda i, j: (0, 0)))
    if residual is not None:
        assert Np == N
        args.append(residual)
        in_specs.append(pl.BlockSpec((tm, tn), lambda i, j: (i, j)))
    out = pl.pallas_call(
        functools.partial(_mm_kernel, has_norm=norm_g is not None, has_res=residual is not None),
        grid=(M // tm, Np // tn),
        in_specs=in_specs,
        out_specs=pl.BlockSpec((tm, tn), lambda i, j: (i, j)),
        out_shape=jax.ShapeDtypeStruct((M, Np), out_dtype),
        scratch_shapes=[pltpu.VMEM((tm, K), BF16)],
        compiler_params=_cparams("parallel", "arbitrary"),
        name="mm",
    )(*args)
    return out if (Np == N or keep_pad) else out[:, :N]


def _rwkv_kernel(r_ref, lw_ref, k_ref, v_ref, kk_ref, a_ref, rk_ref, gg_ref, gb_ref, o_ref, s_ref,
                 *, chunk, heads_per_block):
    C, HB, N = chunk, heads_per_block, RW_HEAD

    @pl.when(pl.program_id(2) == 0)
    def _():
        s_ref[...] = jnp.zeros_like(s_ref)

    NP = HB // 2
    W = 2 * N
    n_chunks = r_ref.shape[1] // C
    row = lax.broadcasted_iota(jnp.int32, (C, C), 0)
    col = lax.broadcasted_iota(jnp.int32, (C, C), 1)
    incl = row >= col
    strict = row > col
    strict4 = jnp.concatenate([strict, strict, incl, incl], axis=0)
    tril_b = incl.astype(BF16)
    eye = (row == col).astype(F32)
    lane = lax.broadcasted_iota(jnp.int32, (1, W), 1)
    in_a = lane < N
    mask_a = in_a.astype(F32)
    mask_b = 1.0 - mask_a
    srow = lax.broadcasted_iota(jnp.int32, (W, W), 0)
    scol = lax.broadcasted_iota(jnp.int32, (W, W), 1)
    block_diag = ((srow < N) == (scol < N)).astype(F32)

    def halves(stacked):
        return jnp.where(in_a, stacked[:C], stacked[C:])

    def head_sum(x):
        sa = jnp.sum(x * mask_a, axis=-1, keepdims=True)
        sb = jnp.sum(x * mask_b, axis=-1, keepdims=True)
        return jnp.where(in_a, sa, sb)

    incl2 = jnp.concatenate([incl, incl], axis=0)

    def chunk_body(c, carry):
        sl = pl.ds(pl.multiple_of(c * C, C), C)
        P = range(NP)
        cols = [slice(p * W, (p + 1) * W) for p in P]
        lw = [lw_ref[0, sl, cs] for cs in cols]
        cum = [_cumsum_rows(tril_b, x) for x in lw]
        g = [jnp.exp(x) for x in cum]
        g_prev = [jnp.exp(x - y) for x, y in zip(cum, lw)]
        g_inv = [jnp.exp(-x) for x in cum]
        g_end = [x[C - 1:C, :] for x in g]
        kk = [kk_ref[0, sl, cs] for cs in cols]
        a_bar = [-x * y for x, y in zip(kk, g_prev)]
        b_til = [x * a_ref[0, sl, cs] * y for x, cs, y in zip(kk, cols, g_inv)]
        k_til = [k_ref[0, sl, cs].astype(F32) * y for cs, y in zip(cols, g_inv)]
        r_bar = [r_ref[0, sl, cs].astype(F32) * y for cs, y in zip(cols, g)]
        lhs = [jnp.concatenate([x, y], axis=0) for x, y in zip(a_bar, r_bar)]
        rhs = [jnp.concatenate([x, y], axis=0) for x, y in zip(b_til, k_til)]
        pair = [_dot_nt(jnp.concatenate([x * mask_a, x * mask_b], axis=0), y)
                for x, y in zip(lhs, rhs)]
        l_ab = [jnp.where(strict, pr[base:base + C, :C], 0.0) for pr in pair for base in (0, 2 * C)]
        t_inv = _nilpotent_inverse(l_ab, eye, C)
        on_v = [jnp.where(strict4, jnp.concatenate(
            [pr[:C, C:], pr[2 * C:3 * C, C:], pr[C:2 * C, C:], pr[3 * C:, C:]], axis=0), 0.0)
            for pr in pair]
        v = [v_ref[0, sl, cs] for cs in cols]
        from_v = [_dot(x, y) for x, y in zip(on_v, v)]
        s = [s_ref[p] for p in P]
        from_state = [_dot_nt(x, y) for x, y in zip(lhs, s)]
        u = [halves(_dot(jnp.concatenate([t_inv[2 * p], t_inv[2 * p + 1]], axis=0),
                         from_state[p][:C] + halves(from_v[p][:2 * C]))) for p in P]
        a_rb = [jnp.where(incl2, jnp.concatenate([pr[C:2 * C, :C], pr[3 * C:, :C]], axis=0), 0.0)
                for pr in pair]
        from_u = [_dot(x, y) for x, y in zip(a_rb, u)]
        upd = [_dot_tn(jnp.concatenate([v[p], u[p]], axis=0),
                       jnp.concatenate([k_til[p] * g_end[p], b_til[p] * g_end[p]], axis=0))
               for p in P]
        for p in P:
            s_ref[p] = s[p] * g_end[p] + upd[p] * block_diag
        for p in P:
            cs = cols[p]
            o = from_state[p][C:] + halves(from_v[p][2 * C:]) + halves(from_u[p])
            mean = head_sum(o) * (1.0 / N)
            cen = o - mean
            var = head_sum(cen * cen) * (1.0 / N)
            o = cen * lax.rsqrt(var + RW_GN_EPS) * gg_ref[:, cs] + gb_ref[:, cs]
            bonus = head_sum(r_ref[0, sl, cs].astype(F32) * k_ref[0, sl, cs].astype(F32)
                             * rk_ref[:, cs]) * v[p]
            o_ref[0, sl, cs] = o + bonus
        return carry

    lax.fori_loop(0, n_chunks, chunk_body, 0)


def _rwkv_recurrence(r, lw, k, v, kk, a, r_k, gn_g, gn_b, tb=256, heads_per_block=16):
    B, T, D = r.shape
    HB = heads_per_block
    W = HB * RW_HEAD
    tb = _pick_tile(T, tb)
    seq = pl.BlockSpec((1, tb, W), lambda b, h, t: (b, t, h))
    vec = pl.BlockSpec((1, W), lambda b, h, t: (0, h))
    return pl.pallas_call(
        functools.partial(_rwkv_kernel, chunk=min(RW_CHUNK, tb), heads_per_block=HB),
        grid=(B, D // W, T // tb),
        in_specs=[seq] * 6 + [vec] * 3,
        out_specs=seq,
        out_shape=jax.ShapeDtypeStruct((B, T, D), F32),
        scratch_shapes=[pltpu.VMEM((HB // 2, 2 * RW_HEAD, 2 * RW_HEAD), F32)],
        compiler_params=_cparams("parallel", "parallel", "arbitrary"),
        name="rwkv7",
    )(r, lw, k, v, kk, a, r_k.reshape(1, D), gn_g.reshape(1, D), gn_b.reshape(1, D))


def _gdn_kernel(q_ref, k_ref, v_ref, beta_ref, gcol_ref, grow_ref, ng_ref, o_ref, s_ref, *, chunk):
    C = chunk

    @pl.when(pl.program_id(1) == 0)
    def _():
        s_ref[...] = jnp.zeros_like(s_ref)

    n_chunks = q_ref.shape[1] // C
    row = lax.broadcasted_iota(jnp.int32, (C, C), 0)
    col = lax.broadcasted_iota(jnp.int32, (C, C), 1)
    incl = row >= col
    strict = row > col
    eye = (row == col).astype(F32)
    ng = ng_ref[...]

    def chunk_body(c, carry):
        sl = pl.ds(pl.multiple_of(c * C, C), C)
        beta_all = beta_ref[0, sl, :]
        gcol_all = gcol_ref[0, sl, :]
        grow_all = grow_ref[0, 0, c]
        H = range(GDN_HEADS)
        cols = [slice(h * GDN_HEAD, (h + 1) * GDN_HEAD) for h in H]
        beta = [beta_all[:, h:h + 1] for h in H]
        gcol = [gcol_all[:, h:h + 1] for h in H]
        k = [k_ref[0, sl, cs].astype(F32) for cs in cols]
        q = [q_ref[0, sl, cs].astype(F32) for cs in cols]
        kb = [x * y for x, y in zip(k, beta)]
        decay = [jnp.exp(jnp.where(incl, gcol[h] - grow_all[h:h + 1, :], -jnp.inf)) for h in H]
        pair = [_dot_nt(jnp.concatenate([kb[h], q[h]], axis=0), k[h]) for h in H]
        t_inv = _nilpotent_inverse(
            [-jnp.where(strict, pair[h][:C] * decay[h], 0.0) for h in H], eye, C)
        eg = [jnp.exp(x) for x in gcol]
        sol = [_dot(t_inv[h], jnp.concatenate(
            [v_ref[0, sl, cols[h]].astype(F32) * beta[h], kb[h] * eg[h]], axis=-1))
            for h in H]
        s = [s_ref[h] for h in H]
        from_state = [_dot(jnp.concatenate([sol[h][:, GDN_HEAD:], q[h] * eg[h]], axis=0), s[h])
                      for h in H]
        v_new = [sol[h][:, :GDN_HEAD] - from_state[h][:C] for h in H]
        intra = [_dot(pair[h][C:] * decay[h], v_new[h]) for h in H]
        g_end = [x[C - 1:C, :] for x in gcol]
        upd = [_dot_tn(k[h] * jnp.exp(g_end[h] - gcol[h]), v_new[h]) for h in H]
        for h in H:
            s_ref[h] = s[h] * jnp.exp(g_end[h]) + upd[h]
        for h in H:
            o = from_state[h][C:] + intra[h]
            o = o * lax.rsqrt(jnp.mean(o * o, axis=-1, keepdims=True) + EPS) * ng
            o_ref[0, sl, cols[h]] = o
        return carry

    lax.fori_loop(0, n_chunks, chunk_body, 0)


def _gdn_recurrence(q, k, v, beta, gcum, norm_g, tb=256):
    B, T, D = q.shape
    H, Dh = GDN_HEADS, GDN_HEAD
    tb = _pick_tile(T, tb)
    C = min(GDN_CHUNK, tb)
    grow = jnp.transpose(gcum.reshape(B, T // tb, tb // C, C, H), (0, 1, 2, 4, 3))
    seq = pl.BlockSpec((1, tb, D), lambda b, t: (b, t, 0))
    colspec = pl.BlockSpec((1, tb, H), lambda b, t: (b, t, 0))
    rowspec = pl.BlockSpec((1, 1, tb // C, H, C), lambda b, t: (b, t, 0, 0, 0))
    return pl.pallas_call(
        functools.partial(_gdn_kernel, chunk=C),
        grid=(B, T // tb),
        in_specs=[seq, seq, seq, colspec, colspec, rowspec,
                  pl.BlockSpec((1, Dh), lambda b, t: (0, 0))],
        out_specs=seq,
        out_shape=jax.ShapeDtypeStruct((B, T, D), F32),
        scratch_shapes=[pltpu.VMEM((H, Dh, Dh), F32)],
        compiler_params=_cparams("parallel", "arbitrary"),
        name="gdn",
    )(q, k, v, beta, gcum, grow, norm_g.reshape(1, Dh))


def _split(t, sizes):
    offs = []
    acc = 0
    for s in sizes[:-1]:
        acc += s
        offs.append(acc)
    return jnp.split(t, offs, axis=-1)


def _prev_rows(p, tm, rows, width):
    B, T, W = p.shape
    tail = p.reshape(B, T // tm, tm, W)[:, :-1, tm - rows:, :width].astype(F32)
    return jnp.pad(tail, ((0, 0), (1, 0), (0, 0), (0, 0)))


def _group_sumsq(x, ones_bd):
    hi, lo = _split_bf16(x * x)
    return (jnp.dot(hi, ones_bd, preferred_element_type=F32)
            + jnp.dot(lo, ones_bd, preferred_element_type=F32))


def _shift_rows(x, halo, s):
    rolled = pltpu.roll(x, s, 0)
    row = lax.broadcasted_iota(jnp.int32, (8, x.shape[1]), 0)
    top = jnp.where(row < s, pltpu.roll(halo, s, 0), rolled[:8])
    return jnp.concatenate([top, rolled[8:]], axis=0)


def _rwkv_prep_kernel(*refs, has_vres):
    D = D_MODEL
    if has_vres:
        (pa_ref, prev_ref, mu_ref, w0_ref, w2_ref, a0_ref, a2_ref, kk_ref, ka_ref,
         vf_ref, v0_ref, v1_ref, v2_ref, r_o, lw_o, k_o, v_o, kkn_o, a_o) = refs
    else:
        (pa_ref, prev_ref, mu_ref, w0_ref, w2_ref, a0_ref, a2_ref, kk_ref, ka_ref,
         r_o, lw_o, k_o, v_o, kkn_o, a_o) = refs
    x = pa_ref[0].astype(F32)
    x = x + (_shift_rows(x, prev_ref[0, 0], 1) - x) * mu_ref[...]
    r = x[:, :D]
    k = x[:, D:2 * D]
    v = x[:, 2 * D:3 * D]
    xw = x[:, 3 * D:3 * D + RW_DECAY_LORA]
    xa = x[:, 3 * D + RW_DECAY_LORA:]
    lw_o[0] = -math.exp(-0.5) * jax.nn.sigmoid(w0_ref[...] + _dot(jnp.tanh(xw), w2_ref[...]))
    a = jax.nn.sigmoid(a0_ref[...] + _dot(xa, a2_ref[...]))
    if has_vres:
        gate = jax.nn.sigmoid(v0_ref[...] + _dot(_dot(v, v1_ref[...]), v2_ref[...]))
        v = v + (vf_ref[0] - v) * gate
    r_o[0] = r.astype(r_o.dtype)
    v_o[0] = v
    a_o[0] = a
    k_o[0] = (k * (1.0 + (a - 1.0) * ka_ref[...])).astype(k_o.dtype)
    lane_r = lax.broadcasted_iota(jnp.int32, (LANES, LANES), 0) // RW_HEAD
    lane_c = lax.broadcasted_iota(jnp.int32, (LANES, LANES), 1) // RW_HEAD
    ones_bd = (lane_r == lane_c).astype(BF16)
    for j in range(D // LANES):
        cs = slice(j * LANES, (j + 1) * LANES)
        kx = k[:, cs] * kk_ref[:, cs]
        kkn_o[0, :, cs] = kx * lax.rsqrt(_group_sumsq(kx, ones_bd) + EPS)


def _rwkv_branch(p_a, v_first, mu, w0, w2, a0, a2, k_k, k_a, r_k, gn_g, gn_b, vres, tm=256):
    B, T, _ = p_a.shape
    D = D_MODEL
    tm = _pick_tile(T, tm)
    row = lambda n: pl.BlockSpec((1, n), lambda b, i: (0, 0))
    mat = lambda a, b_: pl.BlockSpec((a, b_), lambda b, i: (0, 0))
    seq = pl.BlockSpec((1, tm, D), lambda b, i: (b, i, 0))
    args = [p_a, _prev_rows(p_a, tm, 8, A_IN), mu.reshape(1, A_IN), w0.reshape(1, D), w2.astype(BF16),
            a0.reshape(1, D), a2.astype(BF16), k_k.reshape(1, D), k_a.reshape(1, D)]
    in_specs = [pl.BlockSpec((1, tm, A_IN), lambda b, i: (b, i, 0)),
                pl.BlockSpec((1, 1, 8, A_IN), lambda b, i: (b, i, 0, 0)),
                row(A_IN), row(D), mat(RW_DECAY_LORA, D), row(D), mat(RW_AAA_LORA, D), row(D), row(D)]
    if vres is not None:
        v0, v1, v2 = vres
        pad = LANES - RW_MV_LORA
        args += [v_first, v0.reshape(1, D), jnp.pad(v1, ((0, 0), (0, pad))).astype(BF16),
                 jnp.pad(v2, ((0, pad), (0, 0))).astype(BF16)]
        in_specs += [seq, row(D), mat(D, LANES), mat(LANES, D)]
    r, lw, k, v, kk, a = pl.pallas_call(
        functools.partial(_rwkv_prep_kernel, has_vres=vres is not None),
        grid=(B, T // tm),
        in_specs=in_specs,
        out_specs=[seq] * 6,
        out_shape=[jax.ShapeDtypeStruct((B, T, D), dt) for dt in (BF16, F32, BF16, F32, F32, F32)],
        compiler_params=_cparams("parallel", "parallel"),
        name="rwkv_prep",
    )(*args)
    if vres is None:
        v_first = v
    return _rwkv_recurrence(r, lw, k, v, kk, a, r_k, gn_g, gn_b), v_first


def _gdn_prep_kernel(pb_ref, halo_ref, cw_ref, q_o, k_o, v_o):
    D = D_MODEL
    x = pb_ref[0].astype(F32)
    halo = halo_ref[0, 0]
    acc = x * cw_ref[GDN_CONV - 1:GDN_CONV, :]
    for s in range(1, GDN_CONV):
        acc = acc + _shift_rows(x, halo, s) * cw_ref[GDN_CONV - 1 - s:GDN_CONV - s, :]
    y = acc * jax.nn.sigmoid(acc)
    v_o[0] = y[:, 2 * D:].astype(BF16)
    ones = jnp.ones((LANES, LANES), BF16)
    for j in range(D // GDN_HEAD):
        cs = slice(j * GDN_HEAD, (j + 1) * GDN_HEAD)
        q = y[:, cs]
        k = y[:, D + j * GDN_HEAD:D + (j + 1) * GDN_HEAD]
        q_o[0, :, cs] = (q * (lax.rsqrt(_group_sumsq(q, ones) + EPS) * GDN_HEAD ** -0.5)).astype(BF16)
        k_o[0, :, cs] = (k * lax.rsqrt(_group_sumsq(k, ones) + EPS)).astype(BF16)


def _gdn_branch(p_b, conv_w, A_log, dt_bias, norm_g, tm=256):
    B, T, _ = p_b.shape
    D, H, C = D_MODEL, GDN_HEADS, GDN_CHUNK
    tm = _pick_tile(T, tm)
    seq = pl.BlockSpec((1, tm, D), lambda b, i: (b, i, 0))
    q, k, v = pl.pallas_call(
        _gdn_prep_kernel,
        grid=(B, T // tm),
        in_specs=[pl.BlockSpec((1, tm, 3 * D), lambda b, i: (b, i, 0)),
                  pl.BlockSpec((1, 1, 8, 3 * D), lambda b, i: (b, i, 0, 0)),
                  pl.BlockSpec((GDN_CONV, 3 * D), lambda b, i: (0, 0))],
        out_specs=[seq] * 3,
        out_shape=[jax.ShapeDtypeStruct((B, T, D), BF16)] * 3,
        compiler_params=_cparams("parallel", "parallel"),
        name="gdn_prep",
    )(p_b, _prev_rows(p_b, tm, 8, 3 * D), conv_w)
    a_in = p_b[..., 3 * D:3 * D + H].astype(F32)
    b_in = p_b[..., 3 * D + H:3 * D + 2 * H].astype(F32)
    beta = jax.nn.sigmoid(b_in)
    g = -jnp.exp(A_log) * jax.nn.softplus(a_in + dt_bias)
    Cc = min(C, T)
    gcum = jnp.cumsum(g.reshape(B, T // Cc, Cc, H), axis=2).reshape(B, T, H)
    return _gdn_recurrence(q, k, v, beta, gcum, norm_g)


def _dsa_t_kernel(cqt_ref, wit_ref, ki_ref, ckv_ref, ckvt_ref, wqit_ref, wuqt_ref, wuk_ref, wuv_ref,
                  o_ref, key_scr, acc_scr, qlat_scr, *, n_sel, tq, tk, pos_bits):
    H = IDX_HEADS
    q0 = pl.program_id(1) * tq
    nk = (q0 + tq + tk - 1) // tk
    int_min = jnp.int32(-2 ** 31)

    cqt = cqt_ref[0]
    qit = jnp.dot(wqit_ref[...], cqt, preferred_element_type=F32)
    qi_cat = jnp.concatenate([qit[h * IDX_HEAD:(h + 1) * IDX_HEAD] for h in range(H)],
                             axis=1).astype(BF16)
    wit = wit_ref[0]
    wi_cat = jnp.concatenate([wit[h:h + 1] for h in range(H)], axis=1)
    key_off = lax.broadcasted_iota(jnp.int32, (tk, tq), 0)
    q_pos = q0 + lax.broadcasted_iota(jnp.int32, (tk, tq), 1)

    def score_tile(kt, carry):
        ki = ki_ref[0, pl.ds(pl.multiple_of(kt * tk, tk), tk), :]
        z = jnp.maximum(jnp.dot(ki, qi_cat, preferred_element_type=F32), 0.0) * wi_cat
        s = z[:, :tq]
        for h in range(1, H):
            s = s + z[:, h * tq:(h + 1) * tq]
        s = jnp.where(s == 0.0, 0.0, s)
        s = jnp.where(kt * tk + key_off <= q_pos, s, -jnp.inf)
        bits = pltpu.bitcast(s, jnp.int32)
        key_scr[kt] = bits ^ ((bits >> 31) & jnp.int32(0x7FFFFFFF))
        return carry

    lax.fori_loop(0, nk, score_tile, 0)

    def count(pred):
        ways = 4
        def body(kt, acc):
            hit = jnp.where(pred(key_scr[kt], kt * tk + key_off), 1.0, 0.0)
            return acc + jnp.sum(hit.reshape(tk // (8 * ways), ways * 8, tq), axis=0)
        acc = lax.fori_loop(0, nk, body, jnp.zeros((ways * 8, tq), F32))
        return jnp.sum(acc, axis=0, keepdims=True)

    want = jnp.float32(n_sel)
    thr = jnp.where(count(lambda key, pos: key >= 0) >= want, jnp.int32(0), int_min)

    def thr_bit(i, thr):
        cand = thr | jnp.left_shift(jnp.int32(1), 30 - i)
        return jnp.where(count(lambda key, pos: key >= cand) >= want, cand, thr)

    thr = lax.fori_loop(0, 31, thr_bit, thr)

    query_pos = q0 + lax.broadcasted_iota(jnp.int32, (1, tq), 1)
    tied = jnp.logical_and(count(lambda key, pos: key >= thr) > want, query_pos >= n_sel)
    any_tied = jnp.max(jnp.where(tied, 1.0, 0.0)) > 0.0

    def index_cut():
        need = want - count(lambda key, pos: key > thr)

        def pos_bit(i, last):
            cand = last + jnp.left_shift(jnp.int32(1), pos_bits - 1 - i)
            below = count(lambda key, pos: jnp.logical_and(key == thr, pos < cand))
            return jnp.where(below < need, cand, last)

        return lax.fori_loop(0, pos_bits, pos_bit, jnp.zeros((1, tq), jnp.int32))

    last = lax.cond(any_tied, index_cut, lambda: jnp.full((1, tq), 2 ** pos_bits, jnp.int32))

    qt = jnp.dot(wuqt_ref[...], cqt, preferred_element_type=F32)
    q_lat = jnp.concatenate(
        [_dot(wuk_ref[h], qt[h * MLA_HEAD:(h + 1) * MLA_HEAD]) for h in range(MLA_HEADS)],
        axis=1) * MLA_HEAD ** -0.5
    qlat_scr[...] = q_lat.astype(BF16)
    acc_scr[...] = jnp.zeros(acc_scr.shape, F32)
    HG = 2
    GW = HG * tq
    groups = [slice(g * GW, (g + 1) * GW) for g in range(MLA_HEADS // HG)]

    def attend_tile(kt, carry):
        m_old, l_old = carry
        ckv = ckv_ref[0, pl.ds(pl.multiple_of(kt * tk, tk), tk), :]
        ckv_t = ckvt_ref[0, kt]
        key = key_scr[kt]
        pos = kt * tk + key_off
        sel = jnp.logical_or(key > thr, jnp.logical_and(key == thr, pos <= last))
        sel = jnp.logical_and(sel, pos <= q_pos)
        bias = jnp.where(sel, 0.0, NEG_BIG)
        bias = jnp.concatenate([bias] * HG, axis=1)
        logits = [jnp.dot(ckv, qlat_scr[:, g], preferred_element_type=F32) for g in groups]
        m_out, l_out = [], []
        for g, lg in zip(groups, logits):
            lg = lg + bias
            m_new = jnp.maximum(m_old[:, g], jnp.max(lg, axis=0, keepdims=True))
            p = jnp.exp(lg - m_new)
            alpha = jnp.exp(m_old[:, g] - m_new)
            l_out.append(alpha * l_old[:, g] + jnp.sum(p, axis=0, keepdims=True))
            acc_scr[:, g] = alpha * acc_scr[:, g] + jnp.dot(
                ckv_t, p.astype(BF16), preferred_element_type=F32)
            m_out.append(m_new)
        return jnp.concatenate(m_out, axis=1), jnp.concatenate(l_out, axis=1)

    width = MLA_HEADS * tq
    _, l_fin = lax.fori_loop(0, nk, attend_tile,
                             (jnp.full((1, width), NEG_BIG, F32), jnp.zeros((1, width), F32)))
    o_lat = acc_scr[...] / l_fin
    o_ref[0] = jnp.concatenate(
        [_dot_tn(o_lat[:, h * tq:(h + 1) * tq], wuv_ref[h]) for h in range(MLA_HEADS)],
        axis=1)


def _dsa_attention_t(c_q, w_i, k_i, c_kv, w_qi, w_uq, w_uk, w_uv, tq=128, tk=512):
    B, T, _ = c_q.shape
    tq = _pick_tile(T, tq)
    tk = _pick_tile(T, tk)
    n_sel = min(TOPK_MAX, T // 4)
    assert tq % LANES == 0 and tk >= n_sel and tk % tq == 0
    H = MLA_HEADS
    full = lambda *shape: pl.BlockSpec(shape, lambda b, i: (0,) * len(shape))
    c_qt = jnp.transpose(c_q, (0, 2, 1))
    w_it = jnp.transpose(w_i, (0, 2, 1))
    c_kvt = jnp.transpose(c_kv.reshape(B, T // tk, tk, MLA_KV_RANK), (0, 1, 3, 2))
    return pl.pallas_call(
        functools.partial(_dsa_t_kernel, n_sel=n_sel, tq=tq, tk=tk,
                          pos_bits=max(1, (T - 1).bit_length())),
        grid=(B, T // tq),
        in_specs=[pl.BlockSpec((1, MLA_Q_RANK, tq), lambda b, i: (b, 0, i)),
                  pl.BlockSpec((1, IDX_HEADS, tq), lambda b, i: (b, 0, i)),
                  pl.BlockSpec((1, T, IDX_HEAD), lambda b, i: (b, 0, 0)),
                  pl.BlockSpec((1, T, MLA_KV_RANK), lambda b, i: (b, 0, 0)),
                  pl.BlockSpec((1, T // tk, MLA_KV_RANK, tk), lambda b, i: (b, 0, 0, 0)),
                  full(IDX_HEADS * IDX_HEAD, MLA_Q_RANK),
                  full(H * MLA_HEAD, MLA_Q_RANK),
                  full(H, MLA_KV_RANK, MLA_HEAD),
                  full(H, MLA_KV_RANK, MLA_HEAD)],
        out_specs=pl.BlockSpec((1, tq, H * MLA_HEAD), lambda b, i: (b, i, 0)),
        out_shape=jax.ShapeDtypeStruct((B, T, H * MLA_HEAD), F32),
        scratch_shapes=[pltpu.VMEM((T // tk, tk, tq), jnp.int32),
                        pltpu.VMEM((MLA_KV_RANK, H * tq), F32),
                        pltpu.VMEM((MLA_KV_RANK, H * tq), BF16)],
        compiler_params=_cparams("parallel", "arbitrary"),
        name="dsa",
    )(c_qt, w_it, k_i, c_kv, c_kvt, w_qi.T.astype(BF16), w_uq.T.astype(BF16),
      jnp.transpose(w_uk, (1, 0, 2)).astype(BF16), jnp.transpose(w_uv, (1, 0, 2)).astype(BF16))


def _rms(x, g):
    return x * lax.rsqrt(jnp.mean(x * x, axis=-1, keepdims=True) + EPS) * g


def _dsa_prep_kernel(pc_ref, qn_ref, kvn_ref, kig_ref, kib_ref, cq_o, ckv_o, ki_o, wi_o):
    x = pc_ref[...].astype(F32)
    cq_o[...] = _rms(x[:, :MLA_Q_RANK], qn_ref[...]).astype(BF16)
    ckv_o[...] = _rms(x[:, MLA_Q_RANK:MLA_Q_RANK + MLA_KV_RANK], kvn_ref[...]).astype(BF16)
    off = MLA_Q_RANK + MLA_KV_RANK
    ki = x[:, off:off + IDX_HEAD]
    mu = jnp.mean(ki, axis=-1, keepdims=True)
    var = jnp.mean(jnp.square(ki - mu), axis=-1, keepdims=True)
    ki_o[...] = ((ki - mu) * lax.rsqrt(var + EPS) * kig_ref[...] + kib_ref[...]).astype(BF16)
    wi_o[...] = x[:, off + IDX_HEAD:off + IDX_HEAD + IDX_HEADS] * (IDX_HEADS ** -0.5 * IDX_HEAD ** -0.5)


def _dsa_branch(p_c, q_norm, w_uq, kv_norm, w_uk, w_uv, w_qi, ki_g, ki_b, tm=512):
    B, T, W = p_c.shape
    M = B * T
    tm = _pick_tile(M, tm)
    row = lambda n: pl.BlockSpec((1, n), lambda i: (0, 0))
    out = lambda n: pl.BlockSpec((tm, n), lambda i: (i, 0))
    c_q, c_kv, k_i, w_i = pl.pallas_call(
        _dsa_prep_kernel,
        grid=(M // tm,),
        in_specs=[pl.BlockSpec((tm, W), lambda i: (i, 0)), row(MLA_Q_RANK), row(MLA_KV_RANK),
                  row(IDX_HEAD), row(IDX_HEAD)],
        out_specs=[out(MLA_Q_RANK), out(MLA_KV_RANK), out(IDX_HEAD), out(IDX_HEADS)],
        out_shape=[jax.ShapeDtypeStruct((M, MLA_Q_RANK), BF16),
                   jax.ShapeDtypeStruct((M, MLA_KV_RANK), BF16),
                   jax.ShapeDtypeStruct((M, IDX_HEAD), BF16),
                   jax.ShapeDtypeStruct((M, IDX_HEADS), F32)],
        compiler_params=_cparams("parallel"),
        name="dsa_prep",
    )(p_c.reshape(M, W), q_norm.reshape(1, -1), kv_norm.reshape(1, -1), ki_g.reshape(1, -1),
      ki_b.reshape(1, -1))
    return _dsa_attention_t(c_q.reshape(B, T, -1), w_i.reshape(B, T, -1), k_i.reshape(B, T, -1),
                            c_kv.reshape(B, T, -1), w_qi, w_uq, w_uk, w_uv)


def _mix_kernel(ya_ref, yb_ref, yc_ref, ga_ref, gb_ref, gc_ref, w_ref, x_ref, o_ref):
    gate = lambda ref: jax.nn.sigmoid(ref[...].astype(F32))
    mix = gate(ga_ref) * ya_ref[...] + gate(gb_ref) * yb_ref[...] + gate(gc_ref) * yc_ref[...]
    o_ref[...] = x_ref[...] + jnp.dot(mix.astype(BF16), w_ref[...], preferred_element_type=F32)


def _mix_layer(x, y_a, y_b, y_c, p_g, w_out, tm=512):
    M, D = x.shape
    tm = _pick_tile(M, tm)
    tile = pl.BlockSpec((tm, D), lambda i: (i, 0))
    gate = lambda c: pl.BlockSpec((tm, D), lambda i: (i, c))
    return pl.pallas_call(
        _mix_kernel,
        grid=(M // tm,),
        in_specs=[tile, tile, tile, gate(0), gate(1), gate(2),
                  pl.BlockSpec((D, D), lambda i: (0, 0)), tile],
        out_specs=tile,
        out_shape=jax.ShapeDtypeStruct((M, D), F32),
        compiler_params=_cparams("parallel"),
        name="mix",
    )(y_a, y_b, y_c, p_g, p_g, p_g, w_out.astype(BF16), x)


def _xattn_kernel(x_ref, g_ref, wq_ref, k_ref, v_ref, wo_ref, o_ref):
    x = x_ref[0]
    h = x * lax.rsqrt(jnp.mean(x * x, axis=-1, keepdims=True) + EPS) * g_ref[...]
    q = jnp.dot(h.astype(BF16), wq_ref[...], preferred_element_type=F32)
    k = k_ref[0]
    v = v_ref[0]
    outs = []
    for hd in range(XA_HEADS):
        cs = slice(hd * XA_HEAD, (hd + 1) * XA_HEAD)
        logits = _dot_nt(q[:, cs], k[:, cs]) * XA_HEAD ** -0.5
        p = jnp.exp(logits - jnp.max(logits, axis=-1, keepdims=True))
        outs.append(_dot(p, v[:, cs]) / jnp.sum(p, axis=-1, keepdims=True))
    o = jnp.concatenate(outs, axis=-1)
    o_ref[0] = x + jnp.dot(o.astype(BF16), wo_ref[...], preferred_element_type=F32)


def _xattn_layer(x, mem, mem_norm, norm_g, w_q, w_kv, w_o, tq=512):
    B, T, D = x.shape
    Mm = mem.shape[1]
    kv = _mm(mem.reshape(B * Mm, D), w_kv, norm_g=mem_norm).reshape(B, Mm, 2 * D)
    tq = _pick_tile(T, tq)
    tile = pl.BlockSpec((1, tq, D), lambda b, i: (b, i, 0))
    wspec = pl.BlockSpec((D, D), lambda b, i: (0, 0))
    return pl.pallas_call(
        _xattn_kernel,
        grid=(B, T // tq),
        in_specs=[tile, pl.BlockSpec((1, D), lambda b, i: (0, 0)), wspec,
                  pl.BlockSpec((1, Mm, D), lambda b, i: (b, 0, 0)),
                  pl.BlockSpec((1, Mm, D), lambda b, i: (b, 0, 1)), wspec],
        out_specs=tile,
        out_shape=jax.ShapeDtypeStruct((B, T, D), F32),
        compiler_params=_cparams("parallel", "parallel"),
        name="xattn",
    )(x, norm_g.reshape(1, D), w_q.astype(BF16), kv, kv, w_o.astype(BF16))


def _ffn_kernel(x_ref, g_ref, wg_ref, wu_ref, wd_ref, o_ref, h_scr, acc_scr):
    j = pl.program_id(1)

    @pl.when(j == 0)
    def _():
        x = x_ref[...]
        h = x * lax.rsqrt(jnp.mean(x * x, axis=-1, keepdims=True) + EPS) * g_ref[...]
        h_scr[...] = h.astype(BF16)
        acc_scr[...] = jnp.zeros_like(acc_scr)

    h = h_scr[...]
    gate = jnp.dot(h, wg_ref[...], preferred_element_type=F32)
    up = jnp.dot(h, wu_ref[...], preferred_element_type=F32)
    act = gate * jax.nn.sigmoid(gate) * up
    acc_scr[...] += jnp.dot(act.astype(BF16), wd_ref[...], preferred_element_type=F32)

    @pl.when(j == pl.num_programs(1) - 1)
    def _():
        o_ref[...] = x_ref[...] + acc_scr[...]


def _ffn_tiles(M, F, tm, tf):
    tm = _pick_tile(M, tm)
    tf = min(F, tf)
    while F % tf or tf % LANES:
        tf -= LANES
    return tm, tf


def _ffn_layer(x, norm_g, w_gate, w_up, w_down, tm=512, tf=1408):
    B, T, D = x.shape
    M = B * T
    F = w_gate.shape[-1]
    tm, tf = _ffn_tiles(M, F, tm, tf)
    tile = pl.BlockSpec((tm, D), lambda i, j: (i, 0))
    out = pl.pallas_call(
        _ffn_kernel,
        grid=(M // tm, F // tf),
        in_specs=[tile, pl.BlockSpec((1, D), lambda i, j: (0, 0)),
                  pl.BlockSpec((D, tf), lambda i, j: (0, j)),
                  pl.BlockSpec((D, tf), lambda i, j: (0, j)),
                  pl.BlockSpec((tf, D), lambda i, j: (j, 0))],
        out_specs=tile,
        out_shape=jax.ShapeDtypeStruct((M, D), F32),
        scratch_shapes=[pltpu.VMEM((tm, D), BF16), pltpu.VMEM((tm, D), F32)],
        compiler_params=_cparams("parallel", "arbitrary"),
        name="ffn",
    )(x.reshape(M, D), norm_g.reshape(1, D), w_gate.astype(BF16), w_up.astype(BF16), w_down.astype(BF16))
    return out.reshape(B, T, D)


def _router_kernel(x_ref, g_ref, w_ref, h_o, route_o):
    x = x_ref[...]
    h = x * lax.rsqrt(jnp.mean(x * x, axis=-1, keepdims=True) + EPS) * g_ref[...]
    h_o[...] = h.astype(BF16)
    logits = _dot_hi(h, w_ref[...])
    lane = lax.broadcasted_iota(jnp.int32, logits.shape, 1).astype(F32)
    logits = jnp.where(lane < N_EXPERTS, logits, -jnp.inf)
    top = []
    for _ in range(TOP_K):
        m = jnp.max(logits, axis=-1, keepdims=True)
        idx = jnp.min(jnp.where(logits == m, lane, float(LANES)), axis=-1, keepdims=True)
        top.append((m, idx))
        logits = jnp.where(lane == idx, -jnp.inf, logits)
    m0 = top[0][0]
    ex = [jnp.exp(m - m0) for m, _ in top]
    denom = sum(ex)
    route = jnp.zeros_like(logits)
    for k, ((m, idx), e) in enumerate(zip(top, ex)):
        route = route + jnp.where(lane == k, idx, 0.0) + jnp.where(lane == TOP_K + k, e / denom, 0.0)
    route_o[...] = route


def _grouped_ffn_kernel(te_ref, nt_ref, x_ref, gate_ref, wg_ref, wu_ref, wd_ref, o_ref, acc_scr):
    i, j = pl.program_id(0), pl.program_id(1)

    @pl.when(j == 0)
    def _():
        acc_scr[...] = jnp.zeros_like(acc_scr)

    @pl.when(i < nt_ref[0])
    def _():
        h = x_ref[...]
        gate = jnp.dot(h, wg_ref[0], preferred_element_type=F32)
        up = jnp.dot(h, wu_ref[0], preferred_element_type=F32)
        act = gate * jax.nn.sigmoid(gate) * up
        acc_scr[...] += jnp.dot(act.astype(BF16), wd_ref[0], preferred_element_type=F32)

    @pl.when(j == pl.num_programs(1) - 1)
    def _():
        o_ref[...] = acc_scr[...] * gate_ref[...]


def _moe_layer(x, norm_g, router, w_gate, w_up, w_down, tm=512, tf=1792):
    B, T, D = x.shape
    M = B * T
    E, _, F = w_gate.shape
    tm, tf = _ffn_tiles(M, F, tm, tf)
    x2 = x.reshape(M, D)
    h, route = pl.pallas_call(
        _router_kernel,
        grid=(M // tm,),
        in_specs=[pl.BlockSpec((tm, D), lambda i: (i, 0)), pl.BlockSpec((1, D), lambda i: (0, 0)),
                  pl.BlockSpec((D, LANES), lambda i: (0, 0))],
        out_specs=[pl.BlockSpec((tm, D), lambda i: (i, 0)), pl.BlockSpec((tm, LANES), lambda i: (i, 0))],
        out_shape=[jax.ShapeDtypeStruct((M, D), BF16), jax.ShapeDtypeStruct((M, LANES), F32)],
        compiler_params=_cparams("parallel"),
        name="router",
    )(x2, norm_g.reshape(1, D), jnp.pad(router, ((0, 0), (0, LANES - E))))

    n_asg = TOP_K * M
    eid = route[:, :TOP_K].astype(jnp.int32).T.reshape(n_asg)
    prob = route[:, TOP_K:2 * TOP_K].T.reshape(n_asg)
    order = jnp.argsort(eid, stable=True)
    eid_s = eid[order]
    counts = jnp.sum(eid[None, :] == jnp.arange(E, dtype=jnp.int32)[:, None], axis=1).astype(jnp.int32)
    tiles_per = (counts + tm - 1) // tm
    tiles_end = jnp.cumsum(tiles_per)
    row0 = (tiles_end - tiles_per) * tm
    first = jnp.cumsum(counts) - counts
    rows = row0[eid_s] + jnp.arange(n_asg, dtype=jnp.int32) - first[eid_s]
    row_of = rows[jnp.argsort(order)]
    n_rows = n_asg + E * tm
    n_tiles = n_rows // tm
    tile_expert = jnp.minimum(
        jnp.searchsorted(tiles_end, jnp.arange(n_tiles, dtype=jnp.int32), side="right"),
        E - 1).astype(jnp.int32)
    used_tiles = tiles_end[-1:].astype(jnp.int32)
    row_expert = jnp.repeat(tile_expert, tm)
    rank = jnp.arange(n_rows, dtype=jnp.int32) - row0[row_expert]
    live = rank < counts[row_expert]
    slot = jnp.clip(first[row_expert] + rank, 0, n_asg - 1)
    src = jnp.where(live, (order % M).astype(jnp.int32)[slot], 0)
    row_gate = jnp.where(live, prob[order][slot], 0.0)

    ys = pl.pallas_call(
        _grouped_ffn_kernel,
        grid_spec=pltpu.PrefetchScalarGridSpec(
            num_scalar_prefetch=2,
            grid=(n_tiles, F // tf),
            in_specs=[pl.BlockSpec((tm, D), lambda i, j, te, nt: (i, 0)),
                      pl.BlockSpec((tm, 1), lambda i, j, te, nt: (i, 0)),
                      pl.BlockSpec((1, D, tf), lambda i, j, te, nt: (te[i], 0, j)),
                      pl.BlockSpec((1, D, tf), lambda i, j, te, nt: (te[i], 0, j)),
                      pl.BlockSpec((1, tf, D), lambda i, j, te, nt: (te[i], j, 0))],
            out_specs=pl.BlockSpec((tm, D), lambda i, j, te, nt: (i, 0)),
            scratch_shapes=[pltpu.VMEM((tm, D), F32)]),
        out_shape=jax.ShapeDtypeStruct((n_rows, D), F32),
        compiler_params=_cparams("parallel", "arbitrary"),
        name="moe",
    )(tile_expert, used_tiles, h[src], row_gate[:, None], w_gate.astype(BF16), w_up.astype(BF16),
      w_down.astype(BF16))
    out = x2
    for k in range(TOP_K):
        out = out + ys[row_of[k * M:(k + 1) * M]]
    return out.reshape(B, T, D)


def _final_norm_kernel(x_ref, g_ref, o_ref):
    x = x_ref[...]
    o_ref[...] = x * lax.rsqrt(jnp.mean(x * x, axis=-1, keepdims=True) + EPS) * g_ref[...]


def _final_norm(x, g, tm=1024):
    M, D = x.shape
    tm = _pick_tile(M, tm)
    return pl.pallas_call(
        _final_norm_kernel,
        grid=(M // tm,),
        in_specs=[pl.BlockSpec((tm, D), lambda i: (i, 0)), pl.BlockSpec((1, D), lambda i: (0, 0))],
        out_specs=pl.BlockSpec((tm, D), lambda i: (i, 0)),
        out_shape=jax.ShapeDtypeStruct((M, D), F32),
        compiler_params=_cparams("parallel"),
        name="final_norm",
    )(x, g.reshape(1, D))


def kernel(x, mem, norm_mix, w_in, rw_mu, rw_w0, rw_w2, rw_a0, rw_a2, rw_k_k, rw_k_a, rw_r_k,
           rw_gn_g, rw_gn_b, rw_v0, rw_v1, rw_v2, gdn_conv, gdn_A_log, gdn_dt_bias, gdn_norm_g,
           mla_q_norm, mla_w_uq, mla_kv_norm, mla_w_uk, mla_w_uv, idx_w_q, idx_k_g, idx_k_b,
           w_mix_out, mem_norm, norm_xattn, xa_w_q, xa_w_kv, xa_w_o, norm_ffn, ffn_w_gate,
           ffn_w_up, ffn_w_down, moe_router, moe_w_gate, moe_w_up, moe_w_down, final_norm):
    B, T, D = x.shape
    M = B * T
    depth = w_in.shape[0]
    v_first = None
    col0 = (0, A_IN, A_IN + B_IN, A_IN + B_IN + C_IN)
    widths = (A_IN, B_IN, C_IN, G_IN)
    for l in range(depth):
        x2 = x.reshape(M, D)
        p_a, p_b, p_c, p_g = (
            _mm(x2, w_in[l][:, c:c + w], norm_g=norm_mix[l], keep_pad=True, out_dtype=BF16)
            for c, w in zip(col0, widths))
        vres = None if l == 0 else (rw_v0[l - 1], rw_v1[l - 1], rw_v2[l - 1])
        y_a, v_first = _rwkv_branch(p_a.reshape(B, T, -1), v_first, rw_mu[l], rw_w0[l], rw_w2[l],
                                    rw_a0[l], rw_a2[l], rw_k_k[l], rw_k_a[l], rw_r_k[l],
                                    rw_gn_g[l], rw_gn_b[l], vres)
        y_b = _gdn_branch(p_b.reshape(B, T, -1), gdn_conv[l], gdn_A_log[l], gdn_dt_bias[l],
                          gdn_norm_g[l])
        y_c = _dsa_branch(p_c.reshape(B, T, -1), mla_q_norm[l], mla_w_uq[l], mla_kv_norm[l],
                          mla_w_uk[l], mla_w_uv[l], idx_w_q[l], idx_k_g[l], idx_k_b[l])
        x = _mix_layer(x2, y_a.reshape(M, D), y_b.reshape(M, D), y_c.reshape(M, D), p_g,
                       w_mix_out[l]).reshape(B, T, D)
        x = _xattn_layer(x, mem, mem_norm, norm_xattn[l], xa_w_q[l], xa_w_kv[l], xa_w_o[l])
        i = l // 2
        if l % 2 == 0:
            x = _ffn_layer(x, norm_ffn[l], ffn_w_gate[i], ffn_w_up[i], ffn_w_down[i])
        else:
            x = _moe_layer(x, norm_ffn[l], moe_router[i], moe_w_gate[i], moe_w_up[i], moe_w_down[i])
    return _final_norm(x.reshape(M, D), final_norm).reshape(B, T, D)
```

```python
import functools
import math

import jax
import jax.numpy as jnp
from jax import lax
from jax.experimental import pallas as pl
from jax.experimental.pallas import tpu as pltpu

F32 = jnp.float32
BF16 = jnp.bfloat16
HIGHEST = lax.Precision.HIGHEST

D_MODEL = 1024
EPS = 1e-6
LANES = 128
VMEM_LIMIT = 48 * 1024 * 1024

RW_HEAD = 64
RW_HEADS = D_MODEL // RW_HEAD
RW_DECAY_LORA = 64
RW_AAA_LORA = 64
RW_MV_LORA = 32
RW_GN_EPS = 64e-5
RW_CHUNK = 64

GDN_HEAD = 128
GDN_HEADS = D_MODEL // GDN_HEAD
GDN_CONV = 4
GDN_CHUNK = 64

MLA_HEADS = 8
MLA_HEAD = D_MODEL // MLA_HEADS
MLA_Q_RANK = 256
MLA_KV_RANK = 256
IDX_HEADS = 8
IDX_HEAD = 64
TOPK_MAX = 256

XA_HEADS = 4
XA_HEAD = D_MODEL // XA_HEADS

N_EXPERTS = 8
TOP_K = 2

A_SPLITS = (D_MODEL, D_MODEL, D_MODEL, RW_DECAY_LORA, RW_AAA_LORA)
B_SPLITS = (3 * D_MODEL, GDN_HEADS, GDN_HEADS)
C_SPLITS = (MLA_Q_RANK, MLA_KV_RANK, IDX_HEAD, IDX_HEADS)
A_IN = sum(A_SPLITS)
B_IN = sum(B_SPLITS)
C_IN = sum(C_SPLITS)
G_IN = 3 * D_MODEL

NEG_BIG = -1e30


def _cparams(*sem):
    return pltpu.CompilerParams(dimension_semantics=sem, vmem_limit_bytes=VMEM_LIMIT)


def _round_up(n, m):
    return (n + m - 1) // m * m


def _pick_tile(n, pref):
    t = min(n, pref)
    while n % t:
        t -= 8
    return t


def _dot(a, b):
    return jnp.dot(a.astype(BF16), b.astype(BF16), preferred_element_type=F32)


def _dot_nt(a, b):
    return lax.dot_general(a.astype(BF16), b.astype(BF16), (((1,), (1,)), ((), ())),
                           preferred_element_type=F32)


def _dot_tn(a, b):
    return lax.dot_general(a.astype(BF16), b.astype(BF16), (((0,), (0,)), ((), ())),
                           preferred_element_type=F32)


def _dot_hi(a, b):
    return jnp.dot(a, b, preferred_element_type=F32, precision=HIGHEST)


def _split_bf16(a):
    hi = a.astype(BF16)
    return hi, (a - hi.astype(F32)).astype(BF16)


def _dot_split(ah, al, bh, bl):
    n = ah.shape[0]
    top = jnp.dot(jnp.concatenate([ah, al], axis=0), bh, preferred_element_type=F32)
    return top[:n] + top[n:] + jnp.dot(ah, bl, preferred_element_type=F32)


def _nilpotent_inverse(ms, eye, size):
    row = lax.broadcasted_iota(jnp.int32, (size, size), 0)
    col = lax.broadcasted_iota(jnp.int32, (size, size), 1)
    base = 4
    in_block = row // base == col // base
    ds = [jnp.where(in_block, m, 0.0) for m in ms]
    dsplit = [_split_bf16(d) for d in ds]
    d2 = [_dot_split(h, l, h, l) for h, l in dsplit]
    xs = [eye + d for d in ds]
    xs = [x + _dot_split(*_split_bf16(x), *_split_bf16(p)) for x, p in zip(xs, d2)]
    b = base
    while b < size:
        below = jnp.logical_and(row // (2 * b) == col // (2 * b), row // b != col // b)
        es = [_split_bf16(jnp.where(below, m, 0.0)) for m in ms]
        xsplit = [_split_bf16(x) for x in xs]
        ys = [_dot_split(eh, el, xh, xl) for (eh, el), (xh, xl) in zip(es, xsplit)]
        xs = [x + _dot_split(xh, xl, *_split_bf16(y)) for x, (xh, xl), y in zip(xs, xsplit, ys)]
        b *= 2
    return xs


def _cumsum_rows(tril_bf16, x):
    hi = x.astype(BF16)
    r1 = x - hi.astype(F32)
    mid = r1.astype(BF16)
    lo = (r1 - mid.astype(F32)).astype(BF16)
    n = x.shape[1]
    parts = jnp.dot(tril_bf16, jnp.concatenate([hi, mid, lo], axis=1), preferred_element_type=F32)
    return parts[:, :n] + (parts[:, n:2 * n] + parts[:, 2 * n:])


def _mm_kernel(*refs, has_norm, has_res):
    it = iter(refs)
    x_ref = next(it)
    w_ref = next(it)
    g_ref = next(it) if has_norm else None
    r_ref = next(it) if has_res else None
    o_ref = next(it)
    xn_ref = next(it)

    @pl.when(pl.program_id(1) == 0)
    def _():
        x = x_ref[...].astype(F32)
        if has_norm:
            x = x * lax.rsqrt(jnp.mean(x * x, axis=-1, keepdims=True) + EPS) * g_ref[...]
        xn_ref[...] = x.astype(BF16)

    acc = jnp.dot(xn_ref[...], w_ref[...], preferred_element_type=F32)
    if has_res:
        acc = acc + r_ref[...]
    o_ref[...] = acc.astype(o_ref.dtype)


def _mm(x, w, norm_g=None, residual=None, tm=1024, tn=1024, keep_pad=False, out_dtype=F32):
    M, K = x.shape
    N = w.shape[1]
    Np = _round_up(N, LANES)
    wb = w.astype(BF16)
    if Np != N:
        wb = jnp.pad(wb, ((0, 0), (0, Np - N)))
    tm = _pick_tile(M, tm)
    tn = min(Np, tn)
    while Np % tn:
        tn -= LANES
    args = [x, wb]
    in_specs = [pl.BlockSpec((tm, K), lambda i, j: (i, 0)),
                pl.BlockSpec((K, tn), lambda i, j: (0, j))]
    if norm_g is not None:
        args.append(norm_g.reshape(1, K).astype(F32))
        in_specs.append(pl.BlockSpec((1, K), lambda i, j: (0, 0)))
    if residual is not None:
        assert Np == N
        args.append(residual)
        in_specs.append(pl.BlockSpec((tm, tn), lambda i, j: (i, j)))
    out = pl.pallas_call(
        functools.partial(_mm_kernel, has_norm=norm_g is not None, has_res=residual is not None),
        grid=(M // tm, Np // tn),
        in_specs=in_specs,
        out_specs=pl.BlockSpec((tm, tn), lambda i, j: (i, j)),
        out_shape=jax.ShapeDtypeStruct((M, Np), out_dtype),
        scratch_shapes=[pltpu.VMEM((tm, K), BF16)],
        compiler_params=_cparams("parallel", "arbitrary"),
        name="mm",
    )(*args)
    return out if (Np == N or keep_pad) else out[:, :N]


def _rwkv_kernel(r_ref, lw_ref, k_ref, v_ref, kk_ref, a_ref, rk_ref, gg_ref, gb_ref, o_ref, s_ref,
                 *, chunk, heads_per_block):
    C, HB, N = chunk, heads_per_block, RW_HEAD

    @pl.when(pl.program_id(2) == 0)
    def _():
        s_ref[...] = jnp.zeros_like(s_ref)

    NP = HB // 2
    W = 2 * N
    n_chunks = r_ref.shape[1] // C
    row = lax.broadcasted_iota(jnp.int32, (C, C), 0)
    col = lax.broadcasted_iota(jnp.int32, (C, C), 1)
    incl = row >= col
    strict = row > col
    strict4 = jnp.concatenate([strict, strict, incl, incl], axis=0)
    tril_b = incl.astype(BF16)
    eye = (row == col).astype(F32)
    lane = lax.broadcasted_iota(jnp.int32, (1, W), 1)
    in_a = lane < N
    mask_a = in_a.astype(F32)
    mask_b = 1.0 - mask_a
    srow = lax.broadcasted_iota(jnp.int32, (W, W), 0)
    scol = lax.broadcasted_iota(jnp.int32, (W, W), 1)
    block_diag = ((srow < N) == (scol < N)).astype(F32)

    def halves(stacked):
        return jnp.where(in_a, stacked[:C], stacked[C:])

    def head_sum(x):
        sa = jnp.sum(x * mask_a, axis=-1, keepdims=True)
        sb = jnp.sum(x * mask_b, axis=-1, keepdims=True)
        return jnp.where(in_a, sa, sb)

    incl2 = jnp.concatenate([incl, incl], axis=0)

    def chunk_body(c, carry):
        sl = pl.ds(pl.multiple_of(c * C, C), C)
        P = range(NP)
        cols = [slice(p * W, (p + 1) * W) for p in P]
        lw = [lw_ref[0, sl, cs] for cs in cols]
        cum = [_cumsum_rows(tril_b, x) for x in lw]
        g = [jnp.exp(x) for x in cum]
        g_prev = [jnp.exp(x - y) for x, y in zip(cum, lw)]
        g_inv = [jnp.exp(-x) for x in cum]
        g_end = [x[C - 1:C, :] for x in g]
        kk = [kk_ref[0, sl, cs] for cs in cols]
        a_bar = [-x * y for x, y in zip(kk, g_prev)]
        b_til = [x * a_ref[0, sl, cs] * y for x, cs, y in zip(kk, cols, g_inv)]
        k_til = [k_ref[0, sl, cs].astype(F32) * y for cs, y in zip(cols, g_inv)]
        r_bar = [r_ref[0, sl, cs].astype(F32) * y for cs, y in zip(cols, g)]
        lhs = [jnp.concatenate([x, y], axis=0) for x, y in zip(a_bar, r_bar)]
        rhs = [jnp.concatenate([x, y], axis=0) for x, y in zip(b_til, k_til)]
        pair = [_dot_nt(jnp.concatenate([x * mask_a, x * mask_b], axis=0), y)
                for x, y in zip(lhs, rhs)]
        l_ab = [jnp.where(strict, pr[base:base + C, :C], 0.0) for pr in pair for base in (0, 2 * C)]
        t_inv = _nilpotent_inverse(l_ab, eye, C)
        on_v = [jnp.where(strict4, jnp.concatenate(
            [pr[:C, C:], pr[2 * C:3 * C, C:], pr[C:2 * C, C:], pr[3 * C:, C:]], axis=0), 0.0)
            for pr in pair]
        v = [v_ref[0, sl, cs] for cs in cols]
        from_v = [_dot(x, y) for x, y in zip(on_v, v)]
        s = [s_ref[p] for p in P]
        from_state = [_dot_nt(x, y) for x, y in zip(lhs, s)]
        u = [halves(_dot(jnp.concatenate([t_inv[2 * p], t_inv[2 * p + 1]], axis=0),
                         from_state[p][:C] + halves(from_v[p][:2 * C]))) for p in P]
        a_rb = [jnp.where(incl2, jnp.concatenate([pr[C:2 * C, :C], pr[3 * C:, :C]], axis=0), 0.0)
                for pr in pair]
        from_u = [_dot(x, y) for x, y in zip(a_rb, u)]
        upd = [_dot_tn(jnp.concatenate([v[p], u[p]], axis=0),
                       jnp.concatenate([k_til[p] * g_end[p], b_til[p] * g_end[p]], axis=0))
               for p in P]
        for p in P:
            s_ref[p] = s[p] * g_end[p] + upd[p] * block_diag
        for p in P:
            cs = cols[p]
            o = from_state[p][C:] + halves(from_v[p][2 * C:]) + halves(from_u[p])
            mean = head_sum(o) * (1.0 / N)
            cen = o - mean
            var = head_sum(cen * cen) * (1.0 / N)
            o = cen * lax.rsqrt(var + RW_GN_EPS) * gg_ref[:, cs] + gb_ref[:, cs]
            bonus = head_sum(r_ref[0, sl, cs].astype(F32) * k_ref[0, sl, cs].astype(F32)
                             * rk_ref[:, cs]) * v[p]
            o_ref[0, sl, cs] = o + bonus
        return carry

    lax.fori_loop(0, n_chunks, chunk_body, 0)


def _rwkv_recurrence(r, lw, k, v, kk, a, r_k, gn_g, gn_b, tb=256, heads_per_block=16):
    B, T, D = r.shape
    HB = heads_per_block
    W = HB * RW_HEAD
    tb = _pick_tile(T, tb)
    seq = pl.BlockSpec((1, tb, W), lambda b, h, t: (b, t, h))
    vec = pl.BlockSpec((1, W), lambda b, h, t: (0, h))
    return pl.pallas_call(
        functools.partial(_rwkv_kernel, chunk=min(RW_CHUNK, tb), heads_per_block=HB),
        grid=(B, D // W, T // tb),
        in_specs=[seq] * 6 + [vec] * 3,
        out_specs=seq,
        out_shape=jax.ShapeDtypeStruct((B, T, D), F32),
        scratch_shapes=[pltpu.VMEM((HB // 2, 2 * RW_HEAD, 2 * RW_HEAD), F32)],
        compiler_params=_cparams("parallel", "parallel", "arbitrary"),
        name="rwkv7",
    )(r, lw, k, v, kk, a, r_k.reshape(1, D), gn_g.reshape(1, D), gn_b.reshape(1, D))


def _gdn_kernel(q_ref, k_ref, v_ref, beta_ref, gcol_ref, grow_ref, ng_ref, o_ref, s_ref, *, chunk):
    C = chunk

    @pl.when(pl.program_id(1) == 0)
    def _():
        s_ref[...] = jnp.zeros_like(s_ref)

    n_chunks = q_ref.shape[1] // C
    row = lax.broadcasted_iota(jnp.int32, (C, C), 0)
    col = lax.broadcasted_iota(jnp.int32, (C, C), 1)
    incl = row >= col
    strict = row > col
    eye = (row == col).astype(F32)
    ng = ng_ref[...]

    def chunk_body(c, carry):
        sl = pl.ds(pl.multiple_of(c * C, C), C)
        beta_all = beta_ref[0, sl, :]
        gcol_all = gcol_ref[0, sl, :]
        grow_all = grow_ref[0, 0, c]
        H = range(GDN_HEADS)
        cols = [slice(h * GDN_HEAD, (h + 1) * GDN_HEAD) for h in H]
        beta = [beta_all[:, h:h + 1] for h in H]
        gcol = [gcol_all[:, h:h + 1] for h in H]
        k = [k_ref[0, sl, cs].astype(F32) for cs in cols]
        q = [q_ref[0, sl, cs].astype(F32) for cs in cols]
        kb = [x * y for x, y in zip(k, beta)]
        decay = [jnp.exp(jnp.where(incl, gcol[h] - grow_all[h:h + 1, :], -jnp.inf)) for h in H]
        pair = [_dot_nt(jnp.concatenate([kb[h], q[h]], axis=0), k[h]) for h in H]
        t_inv = _nilpotent_inverse(
            [-jnp.where(strict, pair[h][:C] * decay[h], 0.0) for h in H], eye, C)
        eg = [jnp.exp(x) for x in gcol]
        sol = [_dot(t_inv[h], jnp.concatenate(
            [v_ref[0, sl, cols[h]].astype(F32) * beta[h], kb[h] * eg[h]], axis=-1))
            for h in H]
        s = [s_ref[h] for h in H]
        from_state = [_dot(jnp.concatenate([sol[h][:, GDN_HEAD:], q[h] * eg[h]], axis=0), s[h])
                      for h in H]
        v_new = [sol[h][:, :GDN_HEAD] - from_state[h][:C] for h in H]
        intra = [_dot(pair[h][C:] * decay[h], v_new[h]) for h in H]
        g_end = [x[C - 1:C, :] for x in gcol]
        upd = [_dot_tn(k[h] * jnp.exp(g_end[h] - gcol[h]), v_new[h]) for h in H]
        for h in H:
            s_ref[h] = s[h] * jnp.exp(g_end[h]) + upd[h]
        for h in H:
            o = from_state[h][C:] + intra[h]
            o = o * lax.rsqrt(jnp.mean(o * o, axis=-1, keepdims=True) + EPS) * ng
            o_ref[0, sl, cols[h]] = o
        return carry

    lax.fori_loop(0, n_chunks, chunk_body, 0)


def _gdn_recurrence(q, k, v, beta, gcum, norm_g, tb=256):
    B, T, D = q.shape
    H, Dh = GDN_HEADS, GDN_HEAD
    tb = _pick_tile(T, tb)
    C = min(GDN_CHUNK, tb)
    grow = jnp.transpose(gcum.reshape(B, T // tb, tb // C, C, H), (0, 1, 2, 4, 3))
    seq = pl.BlockSpec((1, tb, D), lambda b, t: (b, t, 0))
    colspec = pl.BlockSpec((1, tb, H), lambda b, t: (b, t, 0))
    rowspec = pl.BlockSpec((1, 1, tb // C, H, C), lambda b, t: (b, t, 0, 0, 0))
    return pl.pallas_call(
        functools.partial(_gdn_kernel, chunk=C),
        grid=(B, T // tb),
        in_specs=[seq, seq, seq, colspec, colspec, rowspec,
                  pl.BlockSpec((1, Dh), lambda b, t: (0, 0))],
        out_specs=seq,
        out_shape=jax.ShapeDtypeStruct((B, T, D), F32),
        scratch_shapes=[pltpu.VMEM((H, Dh, Dh), F32)],
        compiler_params=_cparams("parallel", "arbitrary"),
        name="gdn",
    )(q, k, v, beta, gcum, grow, norm_g.reshape(1, Dh))


def _split(t, sizes):
    offs = []
    acc = 0
    for s in sizes[:-1]:
        acc += s
        offs.append(acc)
    return jnp.split(t, offs, axis=-1)


def _prev_rows(p, tm, rows, width):
    B, T, W = p.shape
    tail = p.reshape(B, T // tm, tm, W)[:, :-1, tm - rows:, :width].astype(F32)
    return jnp.pad(tail, ((0, 0), (1, 0), (0, 0), (0, 0)))


def _group_sumsq(x, ones_bd):
    hi, lo = _split_bf16(x * x)
    return (jnp.dot(hi, ones_bd, preferred_element_type=F32)
            + jnp.dot(lo, ones_bd, preferred_element_type=F32))


def _shift_rows(x, halo, s):
    rolled = pltpu.roll(x, s, 0)
    row = lax.broadcasted_iota(jnp.int32, (8, x.shape[1]), 0)
    top = jnp.where(row < s, pltpu.roll(halo, s, 0), rolled[:8])
    return jnp.concatenate([top, rolled[8:]], axis=0)


def _rwkv_prep_kernel(*refs, has_vres):
    D = D_MODEL
    if has_vres:
        (pa_ref, prev_ref, mu_ref, w0_ref, w2_ref, a0_ref, a2_ref, kk_ref, ka_ref,
         vf_ref, v0_ref, v1_ref, v2_ref, r_o, lw_o, k_o, v_o, kkn_o, a_o) = refs
    else:
        (pa_ref, prev_ref, mu_ref, w0_ref, w2_ref, a0_ref, a2_ref, kk_ref, ka_ref,
         r_o, lw_o, k_o, v_o, kkn_o, a_o) = refs
    x = pa_ref[0].astype(F32)
    x = x + (_shift_rows(x, prev_ref[0, 0], 1) - x) * mu_ref[...]
    r = x[:, :D]
    k = x[:, D:2 * D]
    v = x[:, 2 * D:3 * D]
    xw = x[:, 3 * D:3 * D + RW_DECAY_LORA]
    xa = x[:, 3 * D + RW_DECAY_LORA:]
    lw_o[0] = -math.exp(-0.5) * jax.nn.sigmoid(w0_ref[...] + _dot(jnp.tanh(xw), w2_ref[...]))
    a = jax.nn.sigmoid(a0_ref[...] + _dot(xa, a2_ref[...]))
    if has_vres:
        gate = jax.nn.sigmoid(v0_ref[...] + _dot(_dot(v, v1_ref[...]), v2_ref[...]))
        v = v + (vf_ref[0] - v) * gate
    r_o[0] = r.astype(r_o.dtype)
    v_o[0] = v
    a_o[0] = a
    k_o[0] = (k * (1.0 + (a - 1.0) * ka_ref[...])).astype(k_o.dtype)
    lane_r = lax.broadcasted_iota(jnp.int32, (LANES, LANES), 0) // RW_HEAD
    lane_c = lax.broadcasted_iota(jnp.int32, (LANES, LANES), 1) // RW_HEAD
    ones_bd = (lane_r == lane_c).astype(BF16)
    for j in range(D // LANES):
        cs = slice(j * LANES, (j + 1) * LANES)
        kx = k[:, cs] * kk_ref[:, cs]
        kkn_o[0, :, cs] = kx * lax.rsqrt(_group_sumsq(kx, ones_bd) + EPS)


def _rwkv_branch(p_a, v_first, mu, w0, w2, a0, a2, k_k, k_a, r_k, gn_g, gn_b, vres, tm=256):
    B, T, _ = p_a.shape
    D = D_MODEL
    tm = _pick_tile(T, tm)
    row = lambda n: pl.BlockSpec((1, n), lambda b, i: (0, 0))
    mat = lambda a, b_: pl.BlockSpec((a, b_), lambda b, i: (0, 0))
    seq = pl.BlockSpec((1, tm, D), lambda b, i: (b, i, 0))
    args = [p_a, _prev_rows(p_a, tm, 8, A_IN), mu.reshape(1, A_IN), w0.reshape(1, D), w2.astype(BF16),
            a0.reshape(1, D), a2.astype(BF16), k_k.reshape(1, D), k_a.reshape(1, D)]
    in_specs = [pl.BlockSpec((1, tm, A_IN), lambda b, i: (b, i, 0)),
                pl.BlockSpec((1, 1, 8, A_IN), lambda b, i: (b, i, 0, 0)),
                row(A_IN), row(D), mat(RW_DECAY_LORA, D), row(D), mat(RW_AAA_LORA, D), row(D), row(D)]
    if vres is not None:
        v0, v1, v2 = vres
        pad = LANES - RW_MV_LORA
        args += [v_first, v0.reshape(1, D), jnp.pad(v1, ((0, 0), (0, pad))).astype(BF16),
                 jnp.pad(v2, ((0, pad), (0, 0))).astype(BF16)]
        in_specs += [seq, row(D), mat(D, LANES), mat(LANES, D)]
    r, lw, k, v, kk, a = pl.pallas_call(
        functools.partial(_rwkv_prep_kernel, has_vres=vres is not None),
        grid=(B, T // tm),
        in_specs=in_specs,
        out_specs=[seq] * 6,
        out_shape=[jax.ShapeDtypeStruct((B, T, D), dt) for dt in (BF16, F32, BF16, F32, F32, F32)],
        compiler_params=_cparams("parallel", "parallel"),
        name="rwkv_prep",
    )(*args)
    if vres is None:
        v_first = v
    return _rwkv_recurrence(r, lw, k, v, kk, a, r_k, gn_g, gn_b), v_first


def _gdn_prep_kernel(pb_ref, halo_ref, cw_ref, q_o, k_o, v_o):
    D = D_MODEL
    x = pb_ref[0].astype(F32)
    halo = halo_ref[0, 0]
    acc = x * cw_ref[GDN_CONV - 1:GDN_CONV, :]
    for s in range(1, GDN_CONV):
        acc = acc + _shift_rows(x, halo, s) * cw_ref[GDN_CONV - 1 - s:GDN_CONV - s, :]
    y = acc * jax.nn.sigmoid(acc)
    v_o[0] = y[:, 2 * D:].astype(BF16)
    ones = jnp.ones((LANES, LANES), BF16)
    for j in range(D // GDN_HEAD):
        cs = slice(j * GDN_HEAD, (j + 1) * GDN_HEAD)
        q = y[:, cs]
        k = y[:, D + j * GDN_HEAD:D + (j + 1) * GDN_HEAD]
        q_o[0, :, cs] = (q * (lax.rsqrt(_group_sumsq(q, ones) + EPS) * GDN_HEAD ** -0.5)).astype(BF16)
        k_o[0, :, cs] = (k * lax.rsqrt(_group_sumsq(k, ones) + EPS)).astype(BF16)


def _gdn_branch(p_b, conv_w, A_log, dt_bias, norm_g, tm=256):
    B, T, _ = p_b.shape
    D, H, C = D_MODEL, GDN_HEADS, GDN_CHUNK
    tm = _pick_tile(T, tm)
    seq = pl.BlockSpec((1, tm, D), lambda b, i: (b, i, 0))
    q, k, v = pl.pallas_call(
        _gdn_prep_kernel,
        grid=(B, T // tm),
        in_specs=[pl.BlockSpec((1, tm, 3 * D), lambda b, i: (b, i, 0)),
                  pl.BlockSpec((1, 1, 8, 3 * D), lambda b, i: (b, i, 0, 0)),
                  pl.BlockSpec((GDN_CONV, 3 * D), lambda b, i: (0, 0))],
        out_specs=[seq] * 3,
        out_shape=[jax.ShapeDtypeStruct((B, T, D), BF16)] * 3,
        compiler_params=_cparams("parallel", "parallel"),
        name="gdn_prep",
    )(p_b, _prev_rows(p_b, tm, 8, 3 * D), conv_w)
    a_in = p_b[..., 3 * D:3 * D + H].astype(F32)
    b_in = p_b[..., 3 * D + H:3 * D + 2 * H].astype(F32)
    beta = jax.nn.sigmoid(b_in)
    g = -jnp.exp(A_log) * jax.nn.softplus(a_in + dt_bias)
    Cc = min(C, T)
    gcum = jnp.cumsum(g.reshape(B, T // Cc, Cc, H), axis=2).reshape(B, T, H)
    return _gdn_recurrence(q, k, v, beta, gcum, norm_g)


def _dsa_t_kernel(cqt_ref, wit_ref, ki_ref, ckv_ref, ckvt_ref, wqit_ref, wuqt_ref, wuk_ref, wuv_ref,
                  o_ref, key_scr, acc_scr, qlat_scr, *, n_sel, tq, tk, pos_bits):
    H = IDX_HEADS
    q0 = pl.program_id(1) * tq
    nk = (q0 + tq + tk - 1) // tk
    int_min = jnp.int32(-2 ** 31)

    cqt = cqt_ref[0]
    qit = jnp.dot(wqit_ref[...], cqt, preferred_element_type=F32)
    qi_cat = jnp.concatenate([qit[h * IDX_HEAD:(h + 1) * IDX_HEAD] for h in range(H)],
                             axis=1).astype(BF16)
    wit = wit_ref[0]
    wi_cat = jnp.concatenate([wit[h:h + 1] for h in range(H)], axis=1)
    key_off = lax.broadcasted_iota(jnp.int32, (tk, tq), 0)
    q_pos = q0 + lax.broadcasted_iota(jnp.int32, (tk, tq), 1)

    def score_tile(kt, carry):
        ki = ki_ref[0, pl.ds(pl.multiple_of(kt * tk, tk), tk), :]
        z = jnp.maximum(jnp.dot(ki, qi_cat, preferred_element_type=F32), 0.0) * wi_cat
        s = z[:, :tq]
        for h in range(1, H):
            s = s + z[:, h * tq:(h + 1) * tq]
        s = jnp.where(s == 0.0, 0.0, s)
        s = jnp.where(kt * tk + key_off <= q_pos, s, -jnp.inf)
        bits = pltpu.bitcast(s, jnp.int32)
        key_scr[kt] = bits ^ ((bits >> 31) & jnp.int32(0x7FFFFFFF))
        return carry

    lax.fori_loop(0, nk, score_tile, 0)

    def count(pred):
        ways = 4
        def body(kt, acc):
            hit = jnp.where(pred(key_scr[kt], kt * tk + key_off), 1.0, 0.0)
            return acc + jnp.sum(hit.reshape(tk // (8 * ways), ways * 8, tq), axis=0)
        acc = lax.fori_loop(0, nk, body, jnp.zeros((ways * 8, tq), F32))
        return jnp.sum(acc, axis=0, keepdims=True)

    want = jnp.float32(n_sel)
    thr = jnp.where(count(lambda key, pos: key >= 0) >= want, jnp.int32(0), int_min)

    def thr_bit(i, thr):
        cand = thr | jnp.left_shift(jnp.int32(1), 30 - i)
        return jnp.where(count(lambda key, pos: key >= cand) >= want, cand, thr)

    thr = lax.fori_loop(0, 31, thr_bit, thr)

    query_pos = q0 + lax.broadcasted_iota(jnp.int32, (1, tq), 1)
    tied = jnp.logical_and(count(lambda key, pos: key >= thr) > want, query_pos >= n_sel)
    any_tied = jnp.max(jnp.where(tied, 1.0, 0.0)) > 0.0

    def index_cut():
        need = want - count(lambda key, pos: key > thr)

        def pos_bit(i, last):
            cand = last + jnp.left_shift(jnp.int32(1), pos_bits - 1 - i)
            below = count(lambda key, pos: jnp.logical_and(key == thr, pos < cand))
            return jnp.where(below < need, cand, last)

        return lax.fori_loop(0, pos_bits, pos_bit, jnp.zeros((1, tq), jnp.int32))

    last = lax.cond(any_tied, index_cut, lambda: jnp.full((1, tq), 2 ** pos_bits, jnp.int32))

    qt = jnp.dot(wuqt_ref[...], cqt, preferred_element_type=F32)
    q_lat = jnp.concatenate(
        [_dot(wuk_ref[h], qt[h * MLA_HEAD:(h + 1) * MLA_HEAD]) for h in range(MLA_HEADS)],
        axis=1) * MLA_HEAD ** -0.5
    qlat_scr[...] = q_lat.astype(BF16)
    acc_scr[...] = jnp.zeros(acc_scr.shape, F32)
    HG = 2
    GW = HG * tq
    groups = [slice(g * GW, (g + 1) * GW) for g in range(MLA_HEADS // HG)]

    def attend_tile(kt, carry):
        m_old, l_old = carry
        ckv = ckv_ref[0, pl.ds(pl.multiple_of(kt * tk, tk), tk), :]
        ckv_t = ckvt_ref[0, kt]
        key = key_scr[kt]
        pos = kt * tk + key_off
        sel = jnp.logical_or(key > thr, jnp.logical_and(key == thr, pos <= last))
        sel = jnp.logical_and(sel, pos <= q_pos)
        bias = jnp.where(sel, 0.0, NEG_BIG)
        bias = jnp.concatenate([bias] * HG, axis=1)
        logits = [jnp.dot(ckv, qlat_scr[:, g], preferred_element_type=F32) for g in groups]
        m_out, l_out = [], []
        for g, lg in zip(groups, logits):
            lg = lg + bias
            m_new = jnp.maximum(m_old[:, g], jnp.max(lg, axis=0, keepdims=True))
            p = jnp.exp(lg - m_new)
            alpha = jnp.exp(m_old[:, g] - m_new)
            l_out.append(alpha * l_old[:, g] + jnp.sum(p, axis=0, keepdims=True))
            acc_scr[:, g] = alpha * acc_scr[:, g] + jnp.dot(
                ckv_t, p.astype(BF16), preferred_element_type=F32)
            m_out.append(m_new)
        return jnp.concatenate(m_out, axis=1), jnp.concatenate(l_out, axis=1)

    width = MLA_HEADS * tq
    _, l_fin = lax.fori_loop(0, nk, attend_tile,
                             (jnp.full((1, width), NEG_BIG, F32), jnp.zeros((1, width), F32)))
    o_lat = acc_scr[...] / l_fin
    o_ref[0] = jnp.concatenate(
        [_dot_tn(o_lat[:, h * tq:(h + 1) * tq], wuv_ref[h]) for h in range(MLA_HEADS)],
        axis=1)


def _dsa_attention_t(c_q, w_i, k_i, c_kv, w_qi, w_uq, w_uk, w_uv, tq=128, tk=512):
    B, T, _ = c_q.shape
    tq = _pick_tile(T, tq)
    tk = _pick_tile(T, tk)
    n_sel = min(TOPK_MAX, T // 4)
    assert tq % LANES == 0 and tk >= n_sel and tk % tq == 0
    H = MLA_HEADS
    full = lambda *shape: pl.BlockSpec(shape, lambda b, i: (0,) * len(shape))
    c_qt = jnp.transpose(c_q, (0, 2, 1))
    w_it = jnp.transpose(w_i, (0, 2, 1))
    c_kvt = jnp.transpose(c_kv.reshape(B, T // tk, tk, MLA_KV_RANK), (0, 1, 3, 2))
    return pl.pallas_call(
        functools.partial(_dsa_t_kernel, n_sel=n_sel, tq=tq, tk=tk,
                          pos_bits=max(1, (T - 1).bit_length())),
        grid=(B, T // tq),
        in_specs=[pl.BlockSpec((1, MLA_Q_RANK, tq), lambda b, i: (b, 0, i)),
                  pl.BlockSpec((1, IDX_HEADS, tq), lambda b, i: (b, 0, i)),
                  pl.BlockSpec((1, T, IDX_HEAD), lambda b, i: (b, 0, 0)),
                  pl.BlockSpec((1, T, MLA_KV_RANK), lambda b, i: (b, 0, 0)),
                  pl.BlockSpec((1, T // tk, MLA_KV_RANK, tk), lambda b, i: (b, 0, 0, 0)),
                  full(IDX_HEADS * IDX_HEAD, MLA_Q_RANK),
                  full(H * MLA_HEAD, MLA_Q_RANK),
                  full(H, MLA_KV_RANK, MLA_HEAD),
                  full(H, MLA_KV_RANK, MLA_HEAD)],
        out_specs=pl.BlockSpec((1, tq, H * MLA_HEAD), lambda b, i: (b, i, 0)),
        out_shape=jax.ShapeDtypeStruct((B, T, H * MLA_HEAD), F32),
        scratch_shapes=[pltpu.VMEM((T // tk, tk, tq), jnp.int32),
                        pltpu.VMEM((MLA_KV_RANK, H * tq), F32),
                        pltpu.VMEM((MLA_KV_RANK, H * tq), BF16)],
        compiler_params=_cparams("parallel", "arbitrary"),
        name="dsa",
    )(c_qt, w_it, k_i, c_kv, c_kvt, w_qi.T.astype(BF16), w_uq.T.astype(BF16),
      jnp.transpose(w_uk, (1, 0, 2)).astype(BF16), jnp.transpose(w_uv, (1, 0, 2)).astype(BF16))


def _rms(x, g):
    return x * lax.rsqrt(jnp.mean(x * x, axis=-1, keepdims=True) + EPS) * g


def _dsa_prep_kernel(pc_ref, qn_ref, kvn_ref, kig_ref, kib_ref, cq_o, ckv_o, ki_o, wi_o):
    x = pc_ref[...].astype(F32)
    cq_o[...] = _rms(x[:, :MLA_Q_RANK], qn_ref[...]).astype(BF16)
    ckv_o[...] = _rms(x[:, MLA_Q_RANK:MLA_Q_RANK + MLA_KV_RANK], kvn_ref[...]).astype(BF16)
    off = MLA_Q_RANK + MLA_KV_RANK
    ki = x[:, off:off + IDX_HEAD]
    mu = jnp.mean(ki, axis=-1, keepdims=True)
    var = jnp.mean(jnp.square(ki - mu), axis=-1, keepdims=True)
    ki_o[...] = ((ki - mu) * lax.rsqrt(var + EPS) * kig_ref[...] + kib_ref[...]).astype(BF16)
    wi_o[...] = x[:, off + IDX_HEAD:off + IDX_HEAD + IDX_HEADS] * (IDX_HEADS ** -0.5 * IDX_HEAD ** -0.5)


def _dsa_branch(p_c, q_norm, w_uq, kv_norm, w_uk, w_uv, w_qi, ki_g, ki_b, tm=512):
    B, T, W = p_c.shape
    M = B * T
    tm = _pick_tile(M, tm)
    row = lambda n: pl.BlockSpec((1, n), lambda i: (0, 0))
    out = lambda n: pl.BlockSpec((tm, n), lambda i: (i, 0))
    c_q, c_kv, k_i, w_i = pl.pallas_call(
        _dsa_prep_kernel,
        grid=(M // tm,),
        in_specs=[pl.BlockSpec((tm, W), lambda i: (i, 0)), row(MLA_Q_RANK), row(MLA_KV_RANK),
                  row(IDX_HEAD), row(IDX_HEAD)],
        out_specs=[out(MLA_Q_RANK), out(MLA_KV_RANK), out(IDX_HEAD), out(IDX_HEADS)],
        out_shape=[jax.ShapeDtypeStruct((M, MLA_Q_RANK), BF16),
                   jax.ShapeDtypeStruct((M, MLA_KV_RANK), BF16),
                   jax.ShapeDtypeStruct((M, IDX_HEAD), BF16),
                   jax.ShapeDtypeStruct((M, IDX_HEADS), F32)],
        compiler_params=_cparams("parallel"),
        name="dsa_prep",
    )(p_c.reshape(M, W), q_norm.reshape(1, -1), kv_norm.reshape(1, -1), ki_g.reshape(1, -1),
      ki_b.reshape(1, -1))
    return _dsa_attention_t(c_q.reshape(B, T, -1), w_i.reshape(B, T, -1), k_i.reshape(B, T, -1),
                            c_kv.reshape(B, T, -1), w_qi, w_uq, w_uk, w_uv)


def _mix_kernel(ya_ref, yb_ref, yc_ref, ga_ref, gb_ref, gc_ref, w_ref, x_ref, o_ref):
    gate = lambda ref: jax.nn.sigmoid(ref[...].astype(F32))
    mix = gate(ga_ref) * ya_ref[...] + gate(gb_ref) * yb_ref[...] + gate(gc_ref) * yc_ref[...]
    o_ref[...] = x_ref[...] + jnp.dot(mix.astype(BF16), w_ref[...], preferred_element_type=F32)


def _mix_layer(x, y_a, y_b, y_c, p_g, w_out, tm=512):
    M, D = x.shape
    tm = _pick_tile(M, tm)
    tile = pl.BlockSpec((tm, D), lambda i: (i, 0))
    gate = lambda c: pl.BlockSpec((tm, D), lambda i: (i, c))
    return pl.pallas_call(
        _mix_kernel,
        grid=(M // tm,),
        in_specs=[tile, tile, tile, gate(0), gate(1), gate(2),
                  pl.BlockSpec((D, D), lambda i: (0, 0)), tile],
        out_specs=tile,
        out_shape=jax.ShapeDtypeStruct((M, D), F32),
        compiler_params=_cparams("parallel"),
        name="mix",
    )(y_a, y_b, y_c, p_g, p_g, p_g, w_out.astype(BF16), x)


def _xattn_kernel(x_ref, g_ref, wq_ref, k_ref, v_ref, wo_ref, o_ref):
    x = x_ref[0]
    h = x * lax.rsqrt(jnp.mean(x * x, axis=-1, keepdims=True) + EPS) * g_ref[...]
    q = jnp.dot(h.astype(BF16), wq_ref[...], preferred_element_type=F32)
    k = k_ref[0]
    v = v_ref[0]
    outs = []
    for hd in range(XA_HEADS):
        cs = slice(hd * XA_HEAD, (hd + 1) * XA_HEAD)
        logits = _dot_nt(q[:, cs], k[:, cs]) * XA_HEAD ** -0.5
        p = jnp.exp(logits - jnp.max(logits, axis=-1, keepdims=True))
        outs.append(_dot(p, v[:, cs]) / jnp.sum(p, axis=-1, keepdims=True))
    o = jnp.concatenate(outs, axis=-1)
    o_ref[0] = x + jnp.dot(o.astype(BF16), wo_ref[...], preferred_element_type=F32)


def _xattn_layer(x, mem, mem_norm, norm_g, w_q, w_kv, w_o, tq=512):
    B, T, D = x.shape
    Mm = mem.shape[1]
    kv = _mm(mem.reshape(B * Mm, D), w_kv, norm_g=mem_norm).reshape(B, Mm, 2 * D)
    tq = _pick_tile(T, tq)
    tile = pl.BlockSpec((1, tq, D), lambda b, i: (b, i, 0))
    wspec = pl.BlockSpec((D, D), lambda b, i: (0, 0))
    return pl.pallas_call(
        _xattn_kernel,
        grid=(B, T // tq),
        in_specs=[tile, pl.BlockSpec((1, D), lambda b, i: (0, 0)), wspec,
                  pl.BlockSpec((1, Mm, D), lambda b, i: (b, 0, 0)),
                  pl.BlockSpec((1, Mm, D), lambda b, i: (b, 0, 1)), wspec],
        out_specs=tile,
        out_shape=jax.ShapeDtypeStruct((B, T, D), F32),
        compiler_params=_cparams("parallel", "parallel"),
        name="xattn",
    )(x, norm_g.reshape(1, D), w_q.astype(BF16), kv, kv, w_o.astype(BF16))


def _ffn_kernel(x_ref, g_ref, wg_ref, wu_ref, wd_ref, o_ref, h_scr, acc_scr):
    j = pl.program_id(1)

    @pl.when(j == 0)
    def _():
        x = x_ref[...]
        h = x * lax.rsqrt(jnp.mean(x * x, axis=-1, keepdims=True) + EPS) * g_ref[...]
        h_scr[...] = h.astype(BF16)
        acc_scr[...] = jnp.zeros_like(acc_scr)

    h = h_scr[...]
    gate = jnp.dot(h, wg_ref[...], preferred_element_type=F32)
    up = jnp.dot(h, wu_ref[...], preferred_element_type=F32)
    act = gate * jax.nn.sigmoid(gate) * up
    acc_scr[...] += jnp.dot(act.astype(BF16), wd_ref[...], preferred_element_type=F32)

    @pl.when(j == pl.num_programs(1) - 1)
    def _():
        o_ref[...] = x_ref[...] + acc_scr[...]


def _ffn_tiles(M, F, tm, tf):
    tm = _pick_tile(M, tm)
    tf = min(F, tf)
    while F % tf or tf % LANES:
        tf -= LANES
    return tm, tf


def _ffn_layer(x, norm_g, w_gate, w_up, w_down, tm=512, tf=1408):
    B, T, D = x.shape
    M = B * T
    F = w_gate.shape[-1]
    tm, tf = _ffn_tiles(M, F, tm, tf)
    tile = pl.BlockSpec((tm, D), lambda i, j: (i, 0))
    out = pl.pallas_call(
        _ffn_kernel,
        grid=(M // tm, F // tf),
        in_specs=[tile, pl.BlockSpec((1, D), lambda i, j: (0, 0)),
                  pl.BlockSpec((D, tf), lambda i, j: (0, j)),
                  pl.BlockSpec((D, tf), lambda i, j: (0, j)),
                  pl.BlockSpec((tf, D), lambda i, j: (j, 0))],
        out_specs=tile,
        out_shape=jax.ShapeDtypeStruct((M, D), F32),
        scratch_shapes=[pltpu.VMEM((tm, D), BF16), pltpu.VMEM((tm, D), F32)],
        compiler_params=_cparams("parallel", "arbitrary"),
        name="ffn",
    )(x.reshape(M, D), norm_g.reshape(1, D), w_gate.astype(BF16), w_up.astype(BF16), w_down.astype(BF16))
    return out.reshape(B, T, D)


def _router_kernel(x_ref, g_ref, w_ref, h_o, route_o):
    x = x_ref[...]
    h = x * lax.rsqrt(jnp.mean(x * x, axis=-1, keepdims=True) + EPS) * g_ref[...]
    h_o[...] = h.astype(BF16)
    logits = _dot_hi(h, w_ref[...])
    lane = lax.broadcasted_iota(jnp.int32, logits.shape, 1).astype(F32)
    logits = jnp.where(lane < N_EXPERTS, logits, -jnp.inf)
    top = []
    for _ in range(TOP_K):
        m = jnp.max(logits, axis=-1, keepdims=True)
        idx = jnp.min(jnp.where(logits == m, lane, float(LANES)), axis=-1, keepdims=True)
        top.append((m, idx))
        logits = jnp.where(lane == idx, -jnp.inf, logits)
    m0 = top[0][0]
    ex = [jnp.exp(m - m0) for m, _ in top]
    denom = sum(ex)
    route = jnp.zeros_like(logits)
    for k, ((m, idx), e) in enumerate(zip(top, ex)):
        route = route + jnp.where(lane == k, idx, 0.0) + jnp.where(lane == TOP_K + k, e / denom, 0.0)
    route_o[...] = route


def _grouped_ffn_kernel(te_ref, nt_ref, x_ref, gate_ref, wg_ref, wu_ref, wd_ref, o_ref, acc_scr):
    i, j = pl.program_id(0), pl.program_id(1)

    @pl.when(j == 0)
    def _():
        acc_scr[...] = jnp.zeros_like(acc_scr)

    @pl.when(i < nt_ref[0])
    def _():
        h = x_ref[...]
        gate = jnp.dot(h, wg_ref[0], preferred_element_type=F32)
        up = jnp.dot(h, wu_ref[0], preferred_element_type=F32)
        act = gate * jax.nn.sigmoid(gate) * up
        acc_scr[...] += jnp.dot(act.astype(BF16), wd_ref[0], preferred_element_type=F32)

    @pl.when(j == pl.num_programs(1) - 1)
    def _():
        o_ref[...] = acc_scr[...] * gate_ref[...]


def _moe_layer(x, norm_g, router, w_gate, w_up, w_down, tm=512, tf=1792):
    B, T, D = x.shape
    M = B * T
    E, _, F = w_gate.shape
    tm, tf = _ffn_tiles(M, F, tm, tf)
    x2 = x.reshape(M, D)
    h, route = pl.pallas_call(
        _router_kernel,
        grid=(M // tm,),
        in_specs=[pl.BlockSpec((tm, D), lambda i: (i, 0)), pl.BlockSpec((1, D), lambda i: (0, 0)),
                  pl.BlockSpec((D, LANES), lambda i: (0, 0))],
        out_specs=[pl.BlockSpec((tm, D), lambda i: (i, 0)), pl.BlockSpec((tm, LANES), lambda i: (i, 0))],
        out_shape=[jax.ShapeDtypeStruct((M, D), BF16), jax.ShapeDtypeStruct((M, LANES), F32)],
        compiler_params=_cparams("parallel"),
        name="router",
    )(x2, norm_g.reshape(1, D), jnp.pad(router, ((0, 0), (0, LANES - E))))

    n_asg = TOP_K * M
    eid = route[:, :TOP_K].astype(jnp.int32).T.reshape(n_asg)
    prob = route[:, TOP_K:2 * TOP_K].T.reshape(n_asg)
    order = jnp.argsort(eid, stable=True)
    eid_s = eid[order]
    counts = jnp.sum(eid[None, :] == jnp.arange(E, dtype=jnp.int32)[:, None], axis=1).astype(jnp.int32)
    tiles_per = (counts + tm - 1) // tm
    tiles_end = jnp.cumsum(tiles_per)
    row0 = (tiles_end - tiles_per) * tm
    first = jnp.cumsum(counts) - counts
    rows = row0[eid_s] + jnp.arange(n_asg, dtype=jnp.int32) - first[eid_s]
    row_of = rows[jnp.argsort(order)]
    n_rows = n_asg + E * tm
    n_tiles = n_rows // tm
    tile_expert = jnp.minimum(
        jnp.searchsorted(tiles_end, jnp.arange(n_tiles, dtype=jnp.int32), side="right"),
        E - 1).astype(jnp.int32)
    used_tiles = tiles_end[-1:].astype(jnp.int32)
    row_expert = jnp.repeat(tile_expert, tm)
    rank = jnp.arange(n_rows, dtype=jnp.int32) - row0[row_expert]
    live = rank < counts[row_expert]
    slot = jnp.clip(first[row_expert] + rank, 0, n_asg - 1)
    src = jnp.where(live, (order % M).astype(jnp.int32)[slot], 0)
    row_gate = jnp.where(live, prob[order][slot], 0.0)

    ys = pl.pallas_call(
        _grouped_ffn_kernel,
        grid_spec=pltpu.PrefetchScalarGridSpec(
            num_scalar_prefetch=2,
            grid=(n_tiles, F // tf),
            in_specs=[pl.BlockSpec((tm, D), lambda i, j, te, nt: (i, 0)),
                      pl.BlockSpec((tm, 1), lambda i, j, te, nt: (i, 0)),
                      pl.BlockSpec((1, D, tf), lambda i, j, te, nt: (te[i], 0, j)),
                      pl.BlockSpec((1, D, tf), lambda i, j, te, nt: (te[i], 0, j)),
                      pl.BlockSpec((1, tf, D), lambda i, j, te, nt: (te[i], j, 0))],
            out_specs=pl.BlockSpec((tm, D), lambda i, j, te, nt: (i, 0)),
            scratch_shapes=[pltpu.VMEM((tm, D), F32)]),
        out_shape=jax.ShapeDtypeStruct((n_rows, D), F32),
        compiler_params=_cparams("parallel", "arbitrary"),
        name="moe",
    )(tile_expert, used_tiles, h[src], row_gate[:, None], w_gate.astype(BF16), w_up.astype(BF16),
      w_down.astype(BF16))
    out = x2
    for k in range(TOP_K):
        out = out + ys[row_of[k * M:(k + 1) * M]]
    return out.reshape(B, T, D)


def _rms_norm_kernel(x_ref, g_ref, o_ref):
    x = x_ref[...]
    y = x * lax.rsqrt(jnp.mean(x * x, axis=-1, keepdims=True) + EPS) * g_ref[...]
    o_ref[...] = y.astype(o_ref.dtype)


def _rms_norm(x, g, out_dtype=F32, tm=1024):
    M, D = x.shape
    tm = _pick_tile(M, tm)
    return pl.pallas_call(
        _rms_norm_kernel,
        grid=(M // tm,),
        in_specs=[pl.BlockSpec((tm, D), lambda i: (i, 0)), pl.BlockSpec((1, D), lambda i: (0, 0))],
        out_specs=pl.BlockSpec((tm, D), lambda i: (i, 0)),
        out_shape=jax.ShapeDtypeStruct((M, D), out_dtype),
        compiler_params=_cparams("parallel"),
        name="rms_norm",
    )(x, g.reshape(1, D))


def kernel(x, mem, norm_mix, w_in, rw_mu, rw_w0, rw_w2, rw_a0, rw_a2, rw_k_k, rw_k_a, rw_r_k,
           rw_gn_g, rw_gn_b, rw_v0, rw_v1, rw_v2, gdn_conv, gdn_A_log, gdn_dt_bias, gdn_norm_g,
           mla_q_norm, mla_w_uq, mla_kv_norm, mla_w_uk, mla_w_uv, idx_w_q, idx_k_g, idx_k_b,
           w_mix_out, mem_norm, norm_xattn, xa_w_q, xa_w_kv, xa_w_o, norm_ffn, ffn_w_gate,
           ffn_w_up, ffn_w_down, moe_router, moe_w_gate, moe_w_up, moe_w_down, final_norm):
    B, T, D = x.shape
    M = B * T
    depth = w_in.shape[0]
    v_first = None
    col0 = (0, A_IN, A_IN + B_IN, A_IN + B_IN + C_IN)
    widths = (A_IN, B_IN, C_IN, G_IN)
    for l in range(depth):
        x2 = x.reshape(M, D)
        h = _rms_norm(x2, norm_mix[l], out_dtype=BF16)
        p_a, p_b, p_c, p_g = (
            _mm(h, w_in[l][:, c:c + w], keep_pad=True, out_dtype=BF16)
            for c, w in zip(col0, widths))
        vres = None if l == 0 else (rw_v0[l - 1], rw_v1[l - 1], rw_v2[l - 1])
        y_a, v_first = _rwkv_branch(p_a.reshape(B, T, -1), v_first, rw_mu[l], rw_w0[l], rw_w2[l],
                                    rw_a0[l], rw_a2[l], rw_k_k[l], rw_k_a[l], rw_r_k[l],
                                    rw_gn_g[l], rw_gn_b[l], vres)
        y_b = _gdn_branch(p_b.reshape(B, T, -1), gdn_conv[l], gdn_A_log[l], gdn_dt_bias[l],
                          gdn_norm_g[l])
        y_c = _dsa_branch(p_c.reshape(B, T, -1), mla_q_norm[l], mla_w_uq[l], mla_kv_norm[l],
                          mla_w_uk[l], mla_w_uv[l], idx_w_q[l], idx_k_g[l], idx_k_b[l])
        x = _mix_layer(x2, y_a.reshape(M, D), y_b.reshape(M, D), y_c.reshape(M, D), p_g,
                       w_mix_out[l]).reshape(B, T, D)
        x = _xattn_layer(x, mem, mem_norm, norm_xattn[l], xa_w_q[l], xa_w_kv[l], xa_w_o[l])
        i = l // 2
        if l % 2 == 0:
            x = _ffn_layer(x, norm_ffn[l], ffn_w_gate[i], ffn_w_up[i], ffn_w_down[i])
        else:
            x = _moe_layer(x, norm_ffn[l], moe_router[i], moe_w_gate[i], moe_w_up[i], moe_w_down[i])
    return _rms_norm(x.reshape(M, D), final_norm).reshape(B, T, D)
```

```python
import functools
import math

import jax
import jax.numpy as jnp
from jax import lax
from jax.experimental import pallas as pl
from jax.experimental.pallas import tpu as pltpu

F32 = jnp.float32
BF16 = jnp.bfloat16
HIGHEST = lax.Precision.HIGHEST

D_MODEL = 1024
EPS = 1e-6
LANES = 128
VMEM_LIMIT = 48 * 1024 * 1024

RW_HEAD = 64
RW_HEADS = D_MODEL // RW_HEAD
RW_DECAY_LORA = 64
RW_AAA_LORA = 64
RW_MV_LORA = 32
RW_GN_EPS = 64e-5
RW_CHUNK = 64

GDN_HEAD = 128
GDN_HEADS = D_MODEL // GDN_HEAD
GDN_CONV = 4
GDN_CHUNK = 64

MLA_HEADS = 8
MLA_HEAD = D_MODEL // MLA_HEADS
MLA_Q_RANK = 256
MLA_KV_RANK = 256
IDX_HEADS = 8
IDX_HEAD = 64
TOPK_MAX = 256

XA_HEADS = 4
XA_HEAD = D_MODEL // XA_HEADS

N_EXPERTS = 8
TOP_K = 2

A_SPLITS = (D_MODEL, D_MODEL, D_MODEL, RW_DECAY_LORA, RW_AAA_LORA)
B_SPLITS = (3 * D_MODEL, GDN_HEADS, GDN_HEADS)
C_SPLITS = (MLA_Q_RANK, MLA_KV_RANK, IDX_HEAD, IDX_HEADS)
A_IN = sum(A_SPLITS)
B_IN = sum(B_SPLITS)
C_IN = sum(C_SPLITS)
G_IN = 3 * D_MODEL

NEG_BIG = -1e30


def _cparams(*sem):
    return pltpu.CompilerParams(dimension_semantics=sem, vmem_limit_bytes=VMEM_LIMIT)


def _round_up(n, m):
    return (n + m - 1) // m * m


def _pick_tile(n, pref):
    t = min(n, pref)
    while n % t:
        t -= 8
    return t


def _dot(a, b):
    return jnp.dot(a.astype(BF16), b.astype(BF16), preferred_element_type=F32)


def _dot_nt(a, b):
    return lax.dot_general(a.astype(BF16), b.astype(BF16), (((1,), (1,)), ((), ())),
                           preferred_element_type=F32)


def _dot_tn(a, b):
    return lax.dot_general(a.astype(BF16), b.astype(BF16), (((0,), (0,)), ((), ())),
                           preferred_element_type=F32)


def _dot_hi(a, b):
    return jnp.dot(a, b, preferred_element_type=F32, precision=HIGHEST)


def _split_bf16(a):
    hi = a.astype(BF16)
    return hi, (a - hi.astype(F32)).astype(BF16)


def _dot_split(ah, al, bh, bl):
    n = ah.shape[0]
    top = jnp.dot(jnp.concatenate([ah, al], axis=0), bh, preferred_element_type=F32)
    return top[:n] + top[n:] + jnp.dot(ah, bl, preferred_element_type=F32)


def _nilpotent_inverse(ms, eye, size):
    row = lax.broadcasted_iota(jnp.int32, (size, size), 0)
    col = lax.broadcasted_iota(jnp.int32, (size, size), 1)
    base = 4
    in_block = row // base == col // base
    ds = [jnp.where(in_block, m, 0.0) for m in ms]
    dsplit = [_split_bf16(d) for d in ds]
    d2 = [_dot_split(h, l, h, l) for h, l in dsplit]
    xs = [eye + d for d in ds]
    xs = [x + _dot_split(*_split_bf16(x), *_split_bf16(p)) for x, p in zip(xs, d2)]
    b = base
    while b < size:
        below = jnp.logical_and(row // (2 * b) == col // (2 * b), row // b != col // b)
        es = [_split_bf16(jnp.where(below, m, 0.0)) for m in ms]
        xsplit = [_split_bf16(x) for x in xs]
        ys = [_dot_split(eh, el, xh, xl) for (eh, el), (xh, xl) in zip(es, xsplit)]
        xs = [x + _dot_split(xh, xl, *_split_bf16(y)) for x, (xh, xl), y in zip(xs, xsplit, ys)]
        b *= 2
    return xs


def _cumsum_rows(tril_bf16, x):
    hi = x.astype(BF16)
    r1 = x - hi.astype(F32)
    mid = r1.astype(BF16)
    lo = (r1 - mid.astype(F32)).astype(BF16)
    n = x.shape[1]
    parts = jnp.dot(tril_bf16, jnp.concatenate([hi, mid, lo], axis=1), preferred_element_type=F32)
    return parts[:, :n] + (parts[:, n:2 * n] + parts[:, 2 * n:])


def _mm_kernel(*refs, has_norm, has_res):
    it = iter(refs)
    x_ref = next(it)
    w_ref = next(it)
    g_ref = next(it) if has_norm else None
    r_ref = next(it) if has_res else None
    o_ref = next(it)
    xn_ref = next(it)

    @pl.when(pl.program_id(1) == 0)
    def _():
        x = x_ref[...].astype(F32)
        if has_norm:
            x = x * lax.rsqrt(jnp.mean(x * x, axis=-1, keepdims=True) + EPS) * g_ref[...]
        xn_ref[...] = x.astype(BF16)

    acc = jnp.dot(xn_ref[...], w_ref[...], preferred_element_type=F32)
    if has_res:
        acc = acc + r_ref[...]
    o_ref[...] = acc.astype(o_ref.dtype)


def _mm(x, w, norm_g=None, residual=None, tm=1024, tn=1024, keep_pad=False, out_dtype=F32):
    M, K = x.shape
    N = w.shape[1]
    Np = _round_up(N, LANES)
    wb = w.astype(BF16)
    if Np != N:
        wb = jnp.pad(wb, ((0, 0), (0, Np - N)))
    tm = _pick_tile(M, tm)
    tn = min(Np, tn)
    while Np % tn:
        tn -= LANES
    args = [x, wb]
    in_specs = [pl.BlockSpec((tm, K), lambda i, j: (i, 0)),
                pl.BlockSpec((K, tn), lambda i, j: (0, j))]
    if norm_g is not None:
        args.append(norm_g.reshape(1, K).astype(F32))
        in_specs.append(pl.BlockSpec((1, K), lambda i, j: (0, 0)))
    if residual is not None:
        assert Np == N
        args.append(residual)
        in_specs.append(pl.BlockSpec((tm, tn), lambda i, j: (i, j)))
    out = pl.pallas_call(
        functools.partial(_mm_kernel, has_norm=norm_g is not None, has_res=residual is not None),
        grid=(M // tm, Np // tn),
        in_specs=in_specs,
        out_specs=pl.BlockSpec((tm, tn), lambda i, j: (i, j)),
        out_shape=jax.ShapeDtypeStruct((M, Np), out_dtype),
        scratch_shapes=[pltpu.VMEM((tm, K), BF16)],
        compiler_params=_cparams("parallel", "arbitrary"),
        name="mm",
    )(*args)
    return out if (Np == N or keep_pad) else out[:, :N]


def _rwkv_kernel(r_ref, lw_ref, k_ref, v_ref, kk_ref, a_ref, rk_ref, gg_ref, gb_ref, o_ref, s_ref,
                 *, chunk, heads_per_block):
    C, HB, N = chunk, heads_per_block, RW_HEAD

    @pl.when(pl.program_id(2) == 0)
    def _():
        s_ref[...] = jnp.zeros_like(s_ref)

    NP = HB // 2
    W = 2 * N
    n_chunks = r_ref.shape[1] // C
    row = lax.broadcasted_iota(jnp.int32, (C, C), 0)
    col = lax.broadcasted_iota(jnp.int32, (C, C), 1)
    incl = row >= col
    strict = row > col
    strict4 = jnp.concatenate([strict, strict, incl, incl], axis=0)
    tril_b = incl.astype(BF16)
    eye = (row == col).astype(F32)
    lane = lax.broadcasted_iota(jnp.int32, (1, W), 1)
    in_a = lane < N
    mask_a = in_a.astype(F32)
    mask_b = 1.0 - mask_a
    srow = lax.broadcasted_iota(jnp.int32, (W, W), 0)
    scol = lax.broadcasted_iota(jnp.int32, (W, W), 1)
    block_diag = ((srow < N) == (scol < N)).astype(F32)

    def halves(stacked):
        return jnp.where(in_a, stacked[:C], stacked[C:])

    def head_sum(x):
        sa = jnp.sum(x * mask_a, axis=-1, keepdims=True)
        sb = jnp.sum(x * mask_b, axis=-1, keepdims=True)
        return jnp.where(in_a, sa, sb)

    incl2 = jnp.concatenate([incl, incl], axis=0)

    def chunk_body(c, carry):
        sl = pl.ds(pl.multiple_of(c * C, C), C)
        P = range(NP)
        cols = [slice(p * W, (p + 1) * W) for p in P]
        lw = [lw_ref[0, sl, cs] for cs in cols]
        cum = [_cumsum_rows(tril_b, x) for x in lw]
        g = [jnp.exp(x) for x in cum]
        g_prev = [jnp.exp(x - y) for x, y in zip(cum, lw)]
        g_inv = [jnp.exp(-x) for x in cum]
        g_end = [x[C - 1:C, :] for x in g]
        kk = [kk_ref[0, sl, cs] for cs in cols]
        a_bar = [-x * y for x, y in zip(kk, g_prev)]
        b_til = [x * a_ref[0, sl, cs] * y for x, cs, y in zip(kk, cols, g_inv)]
        k_til = [k_ref[0, sl, cs].astype(F32) * y for cs, y in zip(cols, g_inv)]
        r_bar = [r_ref[0, sl, cs].astype(F32) * y for cs, y in zip(cols, g)]
        lhs = [jnp.concatenate([x, y], axis=0) for x, y in zip(a_bar, r_bar)]
        rhs = [jnp.concatenate([x, y], axis=0) for x, y in zip(b_til, k_til)]
        pair = [_dot_nt(jnp.concatenate([x * mask_a, x * mask_b], axis=0), y)
                for x, y in zip(lhs, rhs)]
        l_ab = [jnp.where(strict, pr[base:base + C, :C], 0.0) for pr in pair for base in (0, 2 * C)]
        t_inv = _nilpotent_inverse(l_ab, eye, C)
        on_v = [jnp.where(strict4, jnp.concatenate(
            [pr[:C, C:], pr[2 * C:3 * C, C:], pr[C:2 * C, C:], pr[3 * C:, C:]], axis=0), 0.0)
            for pr in pair]
        v = [v_ref[0, sl, cs] for cs in cols]
        from_v = [_dot(x, y) for x, y in zip(on_v, v)]
        s = [s_ref[p] for p in P]
        from_state = [_dot_nt(x, y) for x, y in zip(lhs, s)]
        u = [halves(_dot(jnp.concatenate([t_inv[2 * p], t_inv[2 * p + 1]], axis=0),
                         from_state[p][:C] + halves(from_v[p][:2 * C]))) for p in P]
        a_rb = [jnp.where(incl2, jnp.concatenate([pr[C:2 * C, :C], pr[3 * C:, :C]], axis=0), 0.0)
                for pr in pair]
        from_u = [_dot(x, y) for x, y in zip(a_rb, u)]
        upd = [_dot_tn(jnp.concatenate([v[p], u[p]], axis=0),
                       jnp.concatenate([k_til[p] * g_end[p], b_til[p] * g_end[p]], axis=0))
               for p in P]
        for p in P:
            s_ref[p] = s[p] * g_end[p] + upd[p] * block_diag
        for p in P:
            cs = cols[p]
            o = from_state[p][C:] + halves(from_v[p][2 * C:]) + halves(from_u[p])
            mean = head_sum(o) * (1.0 / N)
            cen = o - mean
            var = head_sum(cen * cen) * (1.0 / N)
            o = cen * lax.rsqrt(var + RW_GN_EPS) * gg_ref[:, cs] + gb_ref[:, cs]
            bonus = head_sum(r_ref[0, sl, cs].astype(F32) * k_ref[0, sl, cs].astype(F32)
                             * rk_ref[:, cs]) * v[p]
            o_ref[0, sl, cs] = o + bonus
        return carry

    lax.fori_loop(0, n_chunks, chunk_body, 0)


def _rwkv_recurrence(r, lw, k, v, kk, a, r_k, gn_g, gn_b, tb=256, heads_per_block=16):
    B, T, D = r.shape
    HB = heads_per_block
    W = HB * RW_HEAD
    tb = _pick_tile(T, tb)
    seq = pl.BlockSpec((1, tb, W), lambda b, h, t: (b, t, h))
    vec = pl.BlockSpec((1, W), lambda b, h, t: (0, h))
    return pl.pallas_call(
        functools.partial(_rwkv_kernel, chunk=min(RW_CHUNK, tb), heads_per_block=HB),
        grid=(B, D // W, T // tb),
        in_specs=[seq] * 6 + [vec] * 3,
        out_specs=seq,
        out_shape=jax.ShapeDtypeStruct((B, T, D), F32),
        scratch_shapes=[pltpu.VMEM((HB // 2, 2 * RW_HEAD, 2 * RW_HEAD), F32)],
        compiler_params=_cparams("parallel", "parallel", "arbitrary"),
        name="rwkv7",
    )(r, lw, k, v, kk, a, r_k.reshape(1, D), gn_g.reshape(1, D), gn_b.reshape(1, D))


def _gdn_kernel(q_ref, k_ref, v_ref, beta_ref, gcol_ref, grow_ref, ng_ref, o_ref, s_ref, *, chunk):
    C = chunk

    @pl.when(pl.program_id(1) == 0)
    def _():
        s_ref[...] = jnp.zeros_like(s_ref)

    n_chunks = q_ref.shape[1] // C
    row = lax.broadcasted_iota(jnp.int32, (C, C), 0)
    col = lax.broadcasted_iota(jnp.int32, (C, C), 1)
    incl = row >= col
    strict = row > col
    eye = (row == col).astype(F32)
    ng = ng_ref[...]

    def chunk_body(c, carry):
        sl = pl.ds(pl.multiple_of(c * C, C), C)
        beta_all = beta_ref[0, sl, :]
        gcol_all = gcol_ref[0, sl, :]
        grow_all = grow_ref[0, 0, c]
        H = range(GDN_HEADS)
        cols = [slice(h * GDN_HEAD, (h + 1) * GDN_HEAD) for h in H]
        beta = [beta_all[:, h:h + 1] for h in H]
        gcol = [gcol_all[:, h:h + 1] for h in H]
        k = [k_ref[0, sl, cs].astype(F32) for cs in cols]
        q = [q_ref[0, sl, cs].astype(F32) for cs in cols]
        kb = [x * y for x, y in zip(k, beta)]
        decay = [jnp.exp(jnp.where(incl, gcol[h] - grow_all[h:h + 1, :], -jnp.inf)) for h in H]
        pair = [_dot_nt(jnp.concatenate([kb[h], q[h]], axis=0), k[h]) for h in H]
        t_inv = _nilpotent_inverse(
            [-jnp.where(strict, pair[h][:C] * decay[h], 0.0) for h in H], eye, C)
        eg = [jnp.exp(x) for x in gcol]
        sol = [_dot(t_inv[h], jnp.concatenate(
            [v_ref[0, sl, cols[h]].astype(F32) * beta[h], kb[h] * eg[h]], axis=-1))
            for h in H]
        s = [s_ref[h] for h in H]
        from_state = [_dot(jnp.concatenate([sol[h][:, GDN_HEAD:], q[h] * eg[h]], axis=0), s[h])
                      for h in H]
        v_new = [sol[h][:, :GDN_HEAD] - from_state[h][:C] for h in H]
        intra = [_dot(pair[h][C:] * decay[h], v_new[h]) for h in H]
        g_end = [x[C - 1:C, :] for x in gcol]
        upd = [_dot_tn(k[h] * jnp.exp(g_end[h] - gcol[h]), v_new[h]) for h in H]
        for h in H:
            s_ref[h] = s[h] * jnp.exp(g_end[h]) + upd[h]
        for h in H:
            o = from_state[h][C:] + intra[h]
            o = o * lax.rsqrt(jnp.mean(o * o, axis=-1, keepdims=True) + EPS) * ng
            o_ref[0, sl, cols[h]] = o
        return carry

    lax.fori_loop(0, n_chunks, chunk_body, 0)


def _gdn_recurrence(q, k, v, beta, gcum, norm_g, tb=256):
    B, T, D = q.shape
    H, Dh = GDN_HEADS, GDN_HEAD
    tb = _pick_tile(T, tb)
    C = min(GDN_CHUNK, tb)
    grow = jnp.transpose(gcum.reshape(B, T // tb, tb // C, C, H), (0, 1, 2, 4, 3))
    seq = pl.BlockSpec((1, tb, D), lambda b, t: (b, t, 0))
    colspec = pl.BlockSpec((1, tb, H), lambda b, t: (b, t, 0))
    rowspec = pl.BlockSpec((1, 1, tb // C, H, C), lambda b, t: (b, t, 0, 0, 0))
    return pl.pallas_call(
        functools.partial(_gdn_kernel, chunk=C),
        grid=(B, T // tb),
        in_specs=[seq, seq, seq, colspec, colspec, rowspec,
                  pl.BlockSpec((1, Dh), lambda b, t: (0, 0))],
        out_specs=seq,
        out_shape=jax.ShapeDtypeStruct((B, T, D), F32),
        scratch_shapes=[pltpu.VMEM((H, Dh, Dh), F32)],
        compiler_params=_cparams("parallel", "arbitrary"),
        name="gdn",
    )(q, k, v, beta, gcum, grow, norm_g.reshape(1, Dh))


def _split(t, sizes):
    offs = []
    acc = 0
    for s in sizes[:-1]:
        acc += s
        offs.append(acc)
    return jnp.split(t, offs, axis=-1)


def _prev_rows(p, tm, rows, width):
    B, T, W = p.shape
    tail = p.reshape(B, T // tm, tm, W)[:, :-1, tm - rows:, :width].astype(F32)
    return jnp.pad(tail, ((0, 0), (1, 0), (0, 0), (0, 0)))


def _group_sumsq(x, ones_bd):
    hi, lo = _split_bf16(x * x)
    return (jnp.dot(hi, ones_bd, preferred_element_type=F32)
            + jnp.dot(lo, ones_bd, preferred_element_type=F32))


def _shift_rows(x, halo, s):
    rolled = pltpu.roll(x, s, 0)
    row = lax.broadcasted_iota(jnp.int32, (8, x.shape[1]), 0)
    top = jnp.where(row < s, pltpu.roll(halo, s, 0), rolled[:8])
    return jnp.concatenate([top, rolled[8:]], axis=0)


def _rwkv_prep_kernel(*refs, has_vres):
    D = D_MODEL
    if has_vres:
        (pa_ref, prev_ref, mu_ref, w0_ref, w2_ref, a0_ref, a2_ref, kk_ref, ka_ref,
         vf_ref, v0_ref, v1_ref, v2_ref, r_o, lw_o, k_o, v_o, kkn_o, a_o) = refs
    else:
        (pa_ref, prev_ref, mu_ref, w0_ref, w2_ref, a0_ref, a2_ref, kk_ref, ka_ref,
         r_o, lw_o, k_o, v_o, kkn_o, a_o) = refs
    x = pa_ref[0].astype(F32)
    x = x + (_shift_rows(x, prev_ref[0, 0], 1) - x) * mu_ref[...]
    r = x[:, :D]
    k = x[:, D:2 * D]
    v = x[:, 2 * D:3 * D]
    xw = x[:, 3 * D:3 * D + RW_DECAY_LORA]
    xa = x[:, 3 * D + RW_DECAY_LORA:]
    lw_o[0] = -math.exp(-0.5) * jax.nn.sigmoid(w0_ref[...] + _dot(jnp.tanh(xw), w2_ref[...]))
    a = jax.nn.sigmoid(a0_ref[...] + _dot(xa, a2_ref[...]))
    if has_vres:
        gate = jax.nn.sigmoid(v0_ref[...] + _dot(_dot(v, v1_ref[...]), v2_ref[...]))
        v = v + (vf_ref[0] - v) * gate
    r_o[0] = r.astype(r_o.dtype)
    v_o[0] = v
    a_o[0] = a
    k_o[0] = (k * (1.0 + (a - 1.0) * ka_ref[...])).astype(k_o.dtype)
    lane_r = lax.broadcasted_iota(jnp.int32, (LANES, LANES), 0) // RW_HEAD
    lane_c = lax.broadcasted_iota(jnp.int32, (LANES, LANES), 1) // RW_HEAD
    ones_bd = (lane_r == lane_c).astype(BF16)
    for j in range(D // LANES):
        cs = slice(j * LANES, (j + 1) * LANES)
        kx = k[:, cs] * kk_ref[:, cs]
        kkn_o[0, :, cs] = kx * lax.rsqrt(_group_sumsq(kx, ones_bd) + EPS)


def _rwkv_branch(p_a, v_first, mu, w0, w2, a0, a2, k_k, k_a, r_k, gn_g, gn_b, vres, tm=256):
    B, T, _ = p_a.shape
    D = D_MODEL
    tm = _pick_tile(T, tm)
    row = lambda n: pl.BlockSpec((1, n), lambda b, i: (0, 0))
    mat = lambda a, b_: pl.BlockSpec((a, b_), lambda b, i: (0, 0))
    seq = pl.BlockSpec((1, tm, D), lambda b, i: (b, i, 0))
    args = [p_a, _prev_rows(p_a, tm, 8, A_IN), mu.reshape(1, A_IN), w0.reshape(1, D), w2.astype(BF16),
            a0.reshape(1, D), a2.astype(BF16), k_k.reshape(1, D), k_a.reshape(1, D)]
    in_specs = [pl.BlockSpec((1, tm, A_IN), lambda b, i: (b, i, 0)),
                pl.BlockSpec((1, 1, 8, A_IN), lambda b, i: (b, i, 0, 0)),
                row(A_IN), row(D), mat(RW_DECAY_LORA, D), row(D), mat(RW_AAA_LORA, D), row(D), row(D)]
    if vres is not None:
        v0, v1, v2 = vres
        pad = LANES - RW_MV_LORA
        args += [v_first, v0.reshape(1, D), jnp.pad(v1, ((0, 0), (0, pad))).astype(BF16),
                 jnp.pad(v2, ((0, pad), (0, 0))).astype(BF16)]
        in_specs += [seq, row(D), mat(D, LANES), mat(LANES, D)]
    r, lw, k, v, kk, a = pl.pallas_call(
        functools.partial(_rwkv_prep_kernel, has_vres=vres is not None),
        grid=(B, T // tm),
        in_specs=in_specs,
        out_specs=[seq] * 6,
        out_shape=[jax.ShapeDtypeStruct((B, T, D), dt) for dt in (BF16, F32, BF16, F32, F32, F32)],
        compiler_params=_cparams("parallel", "parallel"),
        name="rwkv_prep",
    )(*args)
    if vres is None:
        v_first = v
    return _rwkv_recurrence(r, lw, k, v, kk, a, r_k, gn_g, gn_b), v_first


def _gdn_prep_kernel(pb_ref, halo_ref, cw_ref, q_o, k_o, v_o):
    D = D_MODEL
    x = pb_ref[0].astype(F32)
    halo = halo_ref[0, 0]
    acc = x * cw_ref[GDN_CONV - 1:GDN_CONV, :]
    for s in range(1, GDN_CONV):
        acc = acc + _shift_rows(x, halo, s) * cw_ref[GDN_CONV - 1 - s:GDN_CONV - s, :]
    y = acc * jax.nn.sigmoid(acc)
    v_o[0] = y[:, 2 * D:].astype(BF16)
    ones = jnp.ones((LANES, LANES), BF16)
    for j in range(D // GDN_HEAD):
        cs = slice(j * GDN_HEAD, (j + 1) * GDN_HEAD)
        q = y[:, cs]
        k = y[:, D + j * GDN_HEAD:D + (j + 1) * GDN_HEAD]
        q_o[0, :, cs] = (q * (lax.rsqrt(_group_sumsq(q, ones) + EPS) * GDN_HEAD ** -0.5)).astype(BF16)
        k_o[0, :, cs] = (k * lax.rsqrt(_group_sumsq(k, ones) + EPS)).astype(BF16)


def _gdn_branch(p_b, conv_w, A_log, dt_bias, norm_g, tm=256):
    B, T, _ = p_b.shape
    D, H, C = D_MODEL, GDN_HEADS, GDN_CHUNK
    tm = _pick_tile(T, tm)
    seq = pl.BlockSpec((1, tm, D), lambda b, i: (b, i, 0))
    q, k, v = pl.pallas_call(
        _gdn_prep_kernel,
        grid=(B, T // tm),
        in_specs=[pl.BlockSpec((1, tm, 3 * D), lambda b, i: (b, i, 0)),
                  pl.BlockSpec((1, 1, 8, 3 * D), lambda b, i: (b, i, 0, 0)),
                  pl.BlockSpec((GDN_CONV, 3 * D), lambda b, i: (0, 0))],
        out_specs=[seq] * 3,
        out_shape=[jax.ShapeDtypeStruct((B, T, D), BF16)] * 3,
        compiler_params=_cparams("parallel", "parallel"),
        name="gdn_prep",
    )(p_b, _prev_rows(p_b, tm, 8, 3 * D), conv_w)
    a_in = p_b[..., 3 * D:3 * D + H].astype(F32)
    b_in = p_b[..., 3 * D + H:3 * D + 2 * H].astype(F32)
    beta = jax.nn.sigmoid(b_in)
    g = -jnp.exp(A_log) * jax.nn.softplus(a_in + dt_bias)
    Cc = min(C, T)
    gcum = jnp.cumsum(g.reshape(B, T // Cc, Cc, H), axis=2).reshape(B, T, H)
    return _gdn_recurrence(q, k, v, beta, gcum, norm_g)


def _dsa_t_kernel(cqt_ref, wit_ref, ki_ref, ckv_ref, ckvt_ref, wqit_ref, wuqt_ref, wuk_ref, wuv_ref,
                  o_ref, key_scr, acc_scr, qlat_scr, *, n_sel, tq, tk, pos_bits):
    H = IDX_HEADS
    q0 = pl.program_id(1) * tq
    nk = (q0 + tq + tk - 1) // tk
    int_min = jnp.int32(-2 ** 31)

    cqt = cqt_ref[0]
    qit = jnp.dot(wqit_ref[...], cqt, preferred_element_type=F32)
    qi_cat = jnp.concatenate([qit[h * IDX_HEAD:(h + 1) * IDX_HEAD] for h in range(H)],
                             axis=1).astype(BF16)
    wit = wit_ref[0]
    wi_cat = jnp.concatenate([wit[h:h + 1] for h in range(H)], axis=1)
    key_off = lax.broadcasted_iota(jnp.int32, (tk, tq), 0)
    q_pos = q0 + lax.broadcasted_iota(jnp.int32, (tk, tq), 1)

    def score_tile(kt, carry):
        ki = ki_ref[0, pl.ds(pl.multiple_of(kt * tk, tk), tk), :]
        z = jnp.maximum(jnp.dot(ki, qi_cat, preferred_element_type=F32), 0.0) * wi_cat
        s = z[:, :tq]
        for h in range(1, H):
            s = s + z[:, h * tq:(h + 1) * tq]
        s = jnp.where(s == 0.0, 0.0, s)
        s = jnp.where(kt * tk + key_off <= q_pos, s, -jnp.inf)
        bits = pltpu.bitcast(s, jnp.int32)
        key_scr[kt] = bits ^ ((bits >> 31) & jnp.int32(0x7FFFFFFF))
        return carry

    lax.fori_loop(0, nk, score_tile, 0)

    def count(pred):
        ways = 4
        def body(kt, acc):
            hit = jnp.where(pred(key_scr[kt], kt * tk + key_off), 1.0, 0.0)
            return acc + jnp.sum(hit.reshape(tk // (8 * ways), ways * 8, tq), axis=0)
        acc = lax.fori_loop(0, nk, body, jnp.zeros((ways * 8, tq), F32))
        return jnp.sum(acc, axis=0, keepdims=True)

    want = jnp.float32(n_sel)
    thr = jnp.where(count(lambda key, pos: key >= 0) >= want, jnp.int32(0), int_min)

    def thr_bit(i, thr):
        cand = thr | jnp.left_shift(jnp.int32(1), 30 - i)
        return jnp.where(count(lambda key, pos: key >= cand) >= want, cand, thr)

    thr = lax.fori_loop(0, 31, thr_bit, thr)

    query_pos = q0 + lax.broadcasted_iota(jnp.int32, (1, tq), 1)
    tied = jnp.logical_and(count(lambda key, pos: key >= thr) > want, query_pos >= n_sel)
    any_tied = jnp.max(jnp.where(tied, 1.0, 0.0)) > 0.0

    def index_cut():
        need = want - count(lambda key, pos: key > thr)

        def pos_bit(i, last):
            cand = last + jnp.left_shift(jnp.int32(1), pos_bits - 1 - i)
            below = count(lambda key, pos: jnp.logical_and(key == thr, pos < cand))
            return jnp.where(below < need, cand, last)

        return lax.fori_loop(0, pos_bits, pos_bit, jnp.zeros((1, tq), jnp.int32))

    last = lax.cond(any_tied, index_cut, lambda: jnp.full((1, tq), 2 ** pos_bits, jnp.int32))

    qt = jnp.dot(wuqt_ref[...], cqt, preferred_element_type=F32)
    q_lat = jnp.concatenate(
        [_dot(wuk_ref[h], qt[h * MLA_HEAD:(h + 1) * MLA_HEAD]) for h in range(MLA_HEADS)],
        axis=1) * MLA_HEAD ** -0.5
    qlat_scr[...] = q_lat.astype(BF16)
    acc_scr[...] = jnp.zeros(acc_scr.shape, F32)
    HG = 2
    GW = HG * tq
    groups = [slice(g * GW, (g + 1) * GW) for g in range(MLA_HEADS // HG)]

    def attend_tile(kt, carry):
        m_old, l_old = carry
        ckv = ckv_ref[0, pl.ds(pl.multiple_of(kt * tk, tk), tk), :]
        ckv_t = ckvt_ref[0, kt]
        key = key_scr[kt]
        pos = kt * tk + key_off
        sel = jnp.logical_or(key > thr, jnp.logical_and(key == thr, pos <= last))
        sel = jnp.logical_and(sel, pos <= q_pos)
        bias = jnp.where(sel, 0.0, NEG_BIG)
        bias = jnp.concatenate([bias] * HG, axis=1)
        logits = [jnp.dot(ckv, qlat_scr[:, g], preferred_element_type=F32) for g in groups]
        m_out, l_out = [], []
        for g, lg in zip(groups, logits):
            lg = lg + bias
            m_new = jnp.maximum(m_old[:, g], jnp.max(lg, axis=0, keepdims=True))
            p = jnp.exp(lg - m_new)
            alpha = jnp.exp(m_old[:, g] - m_new)
            pv = jnp.dot(ckv_t, p.astype(BF16), preferred_element_type=F32)
            l_out.append(alpha * l_old[:, g] + pv[MLA_KV_RANK:MLA_KV_RANK + 1])
            acc_scr[:, g] = alpha * acc_scr[:, g] + pv[:MLA_KV_RANK]
            m_out.append(m_new)
        return jnp.concatenate(m_out, axis=1), jnp.concatenate(l_out, axis=1)

    width = MLA_HEADS * tq
    _, l_fin = lax.fori_loop(0, nk, attend_tile,
                             (jnp.full((1, width), NEG_BIG, F32), jnp.zeros((1, width), F32)))
    o_lat = acc_scr[...] / l_fin
    o_ref[0] = jnp.concatenate(
        [_dot_tn(o_lat[:, h * tq:(h + 1) * tq], wuv_ref[h]) for h in range(MLA_HEADS)],
        axis=1)


def _dsa_attention_t(c_q, w_i, k_i, c_kv, w_qi, w_uq, w_uk, w_uv, tq=128, tk=512):
    B, T, _ = c_q.shape
    tq = _pick_tile(T, tq)
    tk = _pick_tile(T, tk)
    n_sel = min(TOPK_MAX, T // 4)
    assert tq % LANES == 0 and tk >= n_sel and tk % tq == 0
    H = MLA_HEADS
    full = lambda *shape: pl.BlockSpec(shape, lambda b, i: (0,) * len(shape))
    c_qt = jnp.transpose(c_q, (0, 2, 1))
    w_it = jnp.transpose(w_i, (0, 2, 1))
    c_kvt = jnp.transpose(c_kv.reshape(B, T // tk, tk, MLA_KV_RANK), (0, 1, 3, 2))
    ones_rows = 16
    c_kvt = jnp.concatenate([c_kvt, jnp.ones((B, T // tk, ones_rows, tk), c_kvt.dtype)], axis=2)
    return pl.pallas_call(
        functools.partial(_dsa_t_kernel, n_sel=n_sel, tq=tq, tk=tk,
                          pos_bits=max(1, (T - 1).bit_length())),
        grid=(B, T // tq),
        in_specs=[pl.BlockSpec((1, MLA_Q_RANK, tq), lambda b, i: (b, 0, i)),
                  pl.BlockSpec((1, IDX_HEADS, tq), lambda b, i: (b, 0, i)),
                  pl.BlockSpec((1, T, IDX_HEAD), lambda b, i: (b, 0, 0)),
                  pl.BlockSpec((1, T, MLA_KV_RANK), lambda b, i: (b, 0, 0)),
                  pl.BlockSpec((1, T // tk, MLA_KV_RANK + ones_rows, tk), lambda b, i: (b, 0, 0, 0)),
                  full(IDX_HEADS * IDX_HEAD, MLA_Q_RANK),
                  full(H * MLA_HEAD, MLA_Q_RANK),
                  full(H, MLA_KV_RANK, MLA_HEAD),
                  full(H, MLA_KV_RANK, MLA_HEAD)],
        out_specs=pl.BlockSpec((1, tq, H * MLA_HEAD), lambda b, i: (b, i, 0)),
        out_shape=jax.ShapeDtypeStruct((B, T, H * MLA_HEAD), F32),
        scratch_shapes=[pltpu.VMEM((T // tk, tk, tq), jnp.int32),
                        pltpu.VMEM((MLA_KV_RANK, H * tq), F32),
                        pltpu.VMEM((MLA_KV_RANK, H * tq), BF16)],
        compiler_params=_cparams("parallel", "arbitrary"),
        name="dsa",
    )(c_qt, w_it, k_i, c_kv, c_kvt, w_qi.T.astype(BF16), w_uq.T.astype(BF16),
      jnp.transpose(w_uk, (1, 0, 2)).astype(BF16), jnp.transpose(w_uv, (1, 0, 2)).astype(BF16))


def _rms(x, g):
    return x * lax.rsqrt(jnp.mean(x * x, axis=-1, keepdims=True) + EPS) * g


def _dsa_prep_kernel(pc_ref, qn_ref, kvn_ref, kig_ref, kib_ref, cq_o, ckv_o, ki_o, wi_o):
    x = pc_ref[...].astype(F32)
    cq_o[...] = _rms(x[:, :MLA_Q_RANK], qn_ref[...]).astype(BF16)
    ckv_o[...] = _rms(x[:, MLA_Q_RANK:MLA_Q_RANK + MLA_KV_RANK], kvn_ref[...]).astype(BF16)
    off = MLA_Q_RANK + MLA_KV_RANK
    ki = x[:, off:off + IDX_HEAD]
    mu = jnp.mean(ki, axis=-1, keepdims=True)
    var = jnp.mean(jnp.square(ki - mu), axis=-1, keepdims=True)
    ki_o[...] = ((ki - mu) * lax.rsqrt(var + EPS) * kig_ref[...] + kib_ref[...]).astype(BF16)
    wi_o[...] = x[:, off + IDX_HEAD:off + IDX_HEAD + IDX_HEADS] * (IDX_HEADS ** -0.5 * IDX_HEAD ** -0.5)


def _dsa_branch(p_c, q_norm, w_uq, kv_norm, w_uk, w_uv, w_qi, ki_g, ki_b, tm=512):
    B, T, W = p_c.shape
    M = B * T
    tm = _pick_tile(M, tm)
    row = lambda n: pl.BlockSpec((1, n), lambda i: (0, 0))
    out = lambda n: pl.BlockSpec((tm, n), lambda i: (i, 0))
    c_q, c_kv, k_i, w_i = pl.pallas_call(
        _dsa_prep_kernel,
        grid=(M // tm,),
        in_specs=[pl.BlockSpec((tm, W), lambda i: (i, 0)), row(MLA_Q_RANK), row(MLA_KV_RANK),
                  row(IDX_HEAD), row(IDX_HEAD)],
        out_specs=[out(MLA_Q_RANK), out(MLA_KV_RANK), out(IDX_HEAD), out(IDX_HEADS)],
        out_shape=[jax.ShapeDtypeStruct((M, MLA_Q_RANK), BF16),
                   jax.ShapeDtypeStruct((M, MLA_KV_RANK), BF16),
                   jax.ShapeDtypeStruct((M, IDX_HEAD), BF16),
                   jax.ShapeDtypeStruct((M, IDX_HEADS), F32)],
        compiler_params=_cparams("parallel"),
        name="dsa_prep",
    )(p_c.reshape(M, W), q_norm.reshape(1, -1), kv_norm.reshape(1, -1), ki_g.reshape(1, -1),
      ki_b.reshape(1, -1))
    return _dsa_attention_t(c_q.reshape(B, T, -1), w_i.reshape(B, T, -1), k_i.reshape(B, T, -1),
                            c_kv.reshape(B, T, -1), w_qi, w_uq, w_uk, w_uv)


def _mix_kernel(ya_ref, yb_ref, yc_ref, ga_ref, gb_ref, gc_ref, w_ref, x_ref, o_ref):
    gate = lambda ref: jax.nn.sigmoid(ref[...].astype(F32))
    mix = gate(ga_ref) * ya_ref[...] + gate(gb_ref) * yb_ref[...] + gate(gc_ref) * yc_ref[...]
    o_ref[...] = x_ref[...] + jnp.dot(mix.astype(BF16), w_ref[...], preferred_element_type=F32)


def _mix_layer(x, y_a, y_b, y_c, p_g, w_out, tm=512):
    M, D = x.shape
    tm = _pick_tile(M, tm)
    tile = pl.BlockSpec((tm, D), lambda i: (i, 0))
    gate = lambda c: pl.BlockSpec((tm, D), lambda i: (i, c))
    return pl.pallas_call(
        _mix_kernel,
        grid=(M // tm,),
        in_specs=[tile, tile, tile, gate(0), gate(1), gate(2),
                  pl.BlockSpec((D, D), lambda i: (0, 0)), tile],
        out_specs=tile,
        out_shape=jax.ShapeDtypeStruct((M, D), F32),
        compiler_params=_cparams("parallel"),
        name="mix",
    )(y_a, y_b, y_c, p_g, p_g, p_g, w_out.astype(BF16), x)


def _xattn_kernel(x_ref, g_ref, wq_ref, k_ref, v_ref, wo_ref, o_ref):
    x = x_ref[0]
    h = x * lax.rsqrt(jnp.mean(x * x, axis=-1, keepdims=True) + EPS) * g_ref[...]
    q = jnp.dot(h.astype(BF16), wq_ref[...], preferred_element_type=F32)
    k = k_ref[0]
    v = v_ref[0]
    outs = []
    for hd in range(XA_HEADS):
        cs = slice(hd * XA_HEAD, (hd + 1) * XA_HEAD)
        logits = _dot_nt(q[:, cs], k[:, cs]) * XA_HEAD ** -0.5
        p = jnp.exp(logits - jnp.max(logits, axis=-1, keepdims=True))
        outs.append(_dot(p, v[:, cs]) / jnp.sum(p, axis=-1, keepdims=True))
    o = jnp.concatenate(outs, axis=-1)
    o_ref[0] = x + jnp.dot(o.astype(BF16), wo_ref[...], preferred_element_type=F32)


def _xattn_layer(x, mem, mem_norm, norm_g, w_q, w_kv, w_o, tq=512):
    B, T, D = x.shape
    Mm = mem.shape[1]
    kv = _mm(mem.reshape(B * Mm, D), w_kv, norm_g=mem_norm).reshape(B, Mm, 2 * D)
    tq = _pick_tile(T, tq)
    tile = pl.BlockSpec((1, tq, D), lambda b, i: (b, i, 0))
    wspec = pl.BlockSpec((D, D), lambda b, i: (0, 0))
    return pl.pallas_call(
        _xattn_kernel,
        grid=(B, T // tq),
        in_specs=[tile, pl.BlockSpec((1, D), lambda b, i: (0, 0)), wspec,
                  pl.BlockSpec((1, Mm, D), lambda b, i: (b, 0, 0)),
                  pl.BlockSpec((1, Mm, D), lambda b, i: (b, 0, 1)), wspec],
        out_specs=tile,
        out_shape=jax.ShapeDtypeStruct((B, T, D), F32),
        compiler_params=_cparams("parallel", "parallel"),
        name="xattn",
    )(x, norm_g.reshape(1, D), w_q.astype(BF16), kv, kv, w_o.astype(BF16))


def _ffn_kernel(x_ref, g_ref, wg_ref, wu_ref, wd_ref, o_ref, h_scr, acc_scr):
    j = pl.program_id(1)

    @pl.when(j == 0)
    def _():
        x = x_ref[...]
        h = x * lax.rsqrt(jnp.mean(x * x, axis=-1, keepdims=True) + EPS) * g_ref[...]
        h_scr[...] = h.astype(BF16)
        acc_scr[...] = jnp.zeros_like(acc_scr)

    h = h_scr[...]
    gate = jnp.dot(h, wg_ref[...], preferred_element_type=F32)
    up = jnp.dot(h, wu_ref[...], preferred_element_type=F32)
    act = gate * jax.nn.sigmoid(gate) * up
    acc_scr[...] += jnp.dot(act.astype(BF16), wd_ref[...], preferred_element_type=F32)

    @pl.when(j == pl.num_programs(1) - 1)
    def _():
        o_ref[...] = x_ref[...] + acc_scr[...]


def _ffn_tiles(M, F, tm, tf):
    tm = _pick_tile(M, tm)
    tf = min(F, tf)
    while F % tf or tf % LANES:
        tf -= LANES
    return tm, tf


def _ffn_layer(x, norm_g, w_gate, w_up, w_down, tm=512, tf=1408):
    B, T, D = x.shape
    M = B * T
    F = w_gate.shape[-1]
    tm, tf = _ffn_tiles(M, F, tm, tf)
    tile = pl.BlockSpec((tm, D), lambda i, j: (i, 0))
    out = pl.pallas_call(
        _ffn_kernel,
        grid=(M // tm, F // tf),
        in_specs=[tile, pl.BlockSpec((1, D), lambda i, j: (0, 0)),
                  pl.BlockSpec((D, tf), lambda i, j: (0, j)),
                  pl.BlockSpec((D, tf), lambda i, j: (0, j)),
                  pl.BlockSpec((tf, D), lambda i, j: (j, 0))],
        out_specs=tile,
        out_shape=jax.ShapeDtypeStruct((M, D), F32),
        scratch_shapes=[pltpu.VMEM((tm, D), BF16), pltpu.VMEM((tm, D), F32)],
        compiler_params=_cparams("parallel", "arbitrary"),
        name="ffn",
    )(x.reshape(M, D), norm_g.reshape(1, D), w_gate.astype(BF16), w_up.astype(BF16), w_down.astype(BF16))
    return out.reshape(B, T, D)


def _router_kernel(x_ref, g_ref, w_ref, h_o, route_o):
    x = x_ref[...]
    h = x * lax.rsqrt(jnp.mean(x * x, axis=-1, keepdims=True) + EPS) * g_ref[...]
    h_o[...] = h.astype(BF16)
    logits = _dot_hi(h, w_ref[...])
    lane = lax.broadcasted_iota(jnp.int32, logits.shape, 1).astype(F32)
    logits = jnp.where(lane < N_EXPERTS, logits, -jnp.inf)
    top = []
    for _ in range(TOP_K):
        m = jnp.max(logits, axis=-1, keepdims=True)
        idx = jnp.min(jnp.where(logits == m, lane, float(LANES)), axis=-1, keepdims=True)
        top.append((m, idx))
        logits = jnp.where(lane == idx, -jnp.inf, logits)
    m0 = top[0][0]
    ex = [jnp.exp(m - m0) for m, _ in top]
    denom = sum(ex)
    route = jnp.zeros_like(logits)
    for k, ((m, idx), e) in enumerate(zip(top, ex)):
        route = route + jnp.where(lane == k, idx, 0.0) + jnp.where(lane == TOP_K + k, e / denom, 0.0)
    route_o[...] = route


def _grouped_ffn_kernel(te_ref, nt_ref, x_ref, gate_ref, wg_ref, wu_ref, wd_ref, o_ref, acc_scr):
    i, j = pl.program_id(0), pl.program_id(1)

    @pl.when(j == 0)
    def _():
        acc_scr[...] = jnp.zeros_like(acc_scr)

    @pl.when(i < nt_ref[0])
    def _():
        h = x_ref[...]
        gate = jnp.dot(h, wg_ref[0], preferred_element_type=F32)
        up = jnp.dot(h, wu_ref[0], preferred_element_type=F32)
        act = gate * jax.nn.sigmoid(gate) * up
        acc_scr[...] += jnp.dot(act.astype(BF16), wd_ref[0], preferred_element_type=F32)

    @pl.when(j == pl.num_programs(1) - 1)
    def _():
        o_ref[...] = acc_scr[...] * gate_ref[...]


def _moe_layer(x, norm_g, router, w_gate, w_up, w_down, tm=512, tf=1792):
    B, T, D = x.shape
    M = B * T
    E, _, F = w_gate.shape
    tm, tf = _ffn_tiles(M, F, tm, tf)
    x2 = x.reshape(M, D)
    h, route = pl.pallas_call(
        _router_kernel,
        grid=(M // tm,),
        in_specs=[pl.BlockSpec((tm, D), lambda i: (i, 0)), pl.BlockSpec((1, D), lambda i: (0, 0)),
                  pl.BlockSpec((D, LANES), lambda i: (0, 0))],
        out_specs=[pl.BlockSpec((tm, D), lambda i: (i, 0)), pl.BlockSpec((tm, LANES), lambda i: (i, 0))],
        out_shape=[jax.ShapeDtypeStruct((M, D), BF16), jax.ShapeDtypeStruct((M, LANES), F32)],
        compiler_params=_cparams("parallel"),
        name="router",
    )(x2, norm_g.reshape(1, D), jnp.pad(router, ((0, 0), (0, LANES - E))))

    n_asg = TOP_K * M
    eid = route[:, :TOP_K].astype(jnp.int32).T.reshape(n_asg)
    prob = route[:, TOP_K:2 * TOP_K].T.reshape(n_asg)
    order = jnp.argsort(eid, stable=True)
    eid_s = eid[order]
    counts = jnp.sum(eid[None, :] == jnp.arange(E, dtype=jnp.int32)[:, None], axis=1).astype(jnp.int32)
    tiles_per = (counts + tm - 1) // tm
    tiles_end = jnp.cumsum(tiles_per)
    row0 = (tiles_end - tiles_per) * tm
    first = jnp.cumsum(counts) - counts
    rows = row0[eid_s] + jnp.arange(n_asg, dtype=jnp.int32) - first[eid_s]
    row_of = rows[jnp.argsort(order)]
    n_rows = n_asg + E * tm
    n_tiles = n_rows // tm
    tile_expert = jnp.minimum(
        jnp.searchsorted(tiles_end, jnp.arange(n_tiles, dtype=jnp.int32), side="right"),
        E - 1).astype(jnp.int32)
    used_tiles = tiles_end[-1:].astype(jnp.int32)
    row_expert = jnp.repeat(tile_expert, tm)
    rank = jnp.arange(n_rows, dtype=jnp.int32) - row0[row_expert]
    live = rank < counts[row_expert]
    slot = jnp.clip(first[row_expert] + rank, 0, n_asg - 1)
    src = jnp.where(live, (order % M).astype(jnp.int32)[slot], 0)
    row_gate = jnp.where(live, prob[order][slot], 0.0)

    ys = pl.pallas_call(
        _grouped_ffn_kernel,
        grid_spec=pltpu.PrefetchScalarGridSpec(
            num_scalar_prefetch=2,
            grid=(n_tiles, F // tf),
            in_specs=[pl.BlockSpec((tm, D), lambda i, j, te, nt: (i, 0)),
                      pl.BlockSpec((tm, 1), lambda i, j, te, nt: (i, 0)),
                      pl.BlockSpec((1, D, tf), lambda i, j, te, nt: (te[i], 0, j)),
                      pl.BlockSpec((1, D, tf), lambda i, j, te, nt: (te[i], 0, j)),
                      pl.BlockSpec((1, tf, D), lambda i, j, te, nt: (te[i], j, 0))],
            out_specs=pl.BlockSpec((tm, D), lambda i, j, te, nt: (i, 0)),
            scratch_shapes=[pltpu.VMEM((tm, D), F32)]),
        out_shape=jax.ShapeDtypeStruct((n_rows, D), F32),
        compiler_params=_cparams("parallel", "arbitrary"),
        name="moe",
    )(tile_expert, used_tiles, h[src], row_gate[:, None], w_gate.astype(BF16), w_up.astype(BF16),
      w_down.astype(BF16))
    out = x2
    for k in range(TOP_K):
        out = out + ys[row_of[k * M:(k + 1) * M]]
    return out.reshape(B, T, D)


def _rms_norm_kernel(x_ref, g_ref, o_ref):
    x = x_ref[...]
    y = x * lax.rsqrt(jnp.mean(x * x, axis=-1, keepdims=True) + EPS) * g_ref[...]
    o_ref[...] = y.astype(o_ref.dtype)


def _rms_norm(x, g, out_dtype=F32, tm=1024):
    M, D = x.shape
    tm = _pick_tile(M, tm)
    return pl.pallas_call(
        _rms_norm_kernel,
        grid=(M // tm,),
        in_specs=[pl.BlockSpec((tm, D), lambda i: (i, 0)), pl.BlockSpec((1, D), lambda i: (0, 0))],
        out_specs=pl.BlockSpec((tm, D), lambda i: (i, 0)),
        out_shape=jax.ShapeDtypeStruct((M, D), out_dtype),
        compiler_params=_cparams("parallel"),
        name="rms_norm",
    )(x, g.reshape(1, D))


def kernel(x, mem, norm_mix, w_in, rw_mu, rw_w0, rw_w2, rw_a0, rw_a2, rw_k_k, rw_k_a, rw_r_k,
           rw_gn_g, rw_gn_b, rw_v0, rw_v1, rw_v2, gdn_conv, gdn_A_log, gdn_dt_bias, gdn_norm_g,
           mla_q_norm, mla_w_uq, mla_kv_norm, mla_w_uk, mla_w_uv, idx_w_q, idx_k_g, idx_k_b,
           w_mix_out, mem_norm, norm_xattn, xa_w_q, xa_w_kv, xa_w_o, norm_ffn, ffn_w_gate,
           ffn_w_up, ffn_w_down, moe_router, moe_w_gate, moe_w_up, moe_w_down, final_norm):
    B, T, D = x.shape
    M = B * T
    depth = w_in.shape[0]
    v_first = None
    col0 = (0, A_IN, A_IN + B_IN, A_IN + B_IN + C_IN)
    widths = (A_IN, B_IN, C_IN, G_IN)
    for l in range(depth):
        x2 = x.reshape(M, D)
        h = _rms_norm(x2, norm_mix[l], out_dtype=BF16)
        p_a, p_b, p_c, p_g = (
            _mm(h, w_in[l][:, c:c + w], keep_pad=True, out_dtype=BF16)
            for c, w in zip(col0, widths))
        vres = None if l == 0 else (rw_v0[l - 1], rw_v1[l - 1], rw_v2[l - 1])
        y_a, v_first = _rwkv_branch(p_a.reshape(B, T, -1), v_first, rw_mu[l], rw_w0[l], rw_w2[l],
                                    rw_a0[l], rw_a2[l], rw_k_k[l], rw_k_a[l], rw_r_k[l],
                                    rw_gn_g[l], rw_gn_b[l], vres)
        y_b = _gdn_branch(p_b.reshape(B, T, -1), gdn_conv[l], gdn_A_log[l], gdn_dt_bias[l],
                          gdn_norm_g[l])
        y_c = _dsa_branch(p_c.reshape(B, T, -1), mla_q_norm[l], mla_w_uq[l], mla_kv_norm[l],
                          mla_w_uk[l], mla_w_uv[l], idx_w_q[l], idx_k_g[l], idx_k_b[l])
        x = _mix_layer(x2, y_a.reshape(M, D), y_b.reshape(M, D), y_c.reshape(M, D), p_g,
                       w_mix_out[l]).reshape(B, T, D)
        x = _xattn_layer(x, mem, mem_norm, norm_xattn[l], xa_w_q[l], xa_w_kv[l], xa_w_o[l])
        i = l // 2
        if l % 2 == 0:
            x = _ffn_layer(x, norm_ffn[l], ffn_w_gate[i], ffn_w_up[i], ffn_w_down[i])
        else:
            x = _moe_layer(x, norm_ffn[l], moe_router[i], moe_w_gate[i], moe_w_up[i], moe_w_down[i])
    return _rms_norm(x.reshape(M, D), final_norm).reshape(B, T, D)
```
